```python
import math
import jax, jax.numpy as jnp
from jax import lax
import numpy as np

D_MODEL = 2048
BATCH = 1
SEQ = 16384
DEPTH = 2
DEC_BATCH = 4
DEC_SEQ = 8192
PAST_LEN = 128

GRID_W = 64
PLE_DIM = 256
NORM_EPS = 1e-6
Q_BLOCK = 128
HA = 8
DA = 128
A_WIDTH = HA * DA
SHORT_CONV = 3
DELTA_CHUNK = 64
HB = 8
KVB = 2
DB = 128
AXIAL_THETA = 10000.0
HC = 8
DC = 64
DVC = 2 * DC
ROT_C = DC // 4
ROPE_THETA = 500000.0
D_FF = 5632
FFN_CONV = 3
IN_SIZES = (3 * A_WIDTH, A_WIDTH, 2 * HA, 2 * HA, HB * DB, KVB * DB, KVB * DB, HC * 2 * DC, HC * 2 * DC, HC * DVC, 3 * D_MODEL)
N_IN = sum(IN_SIZES)

kernel_name = 'hybrid_bidir_encoder_gdn_axialgqa_diffattn'


def rms_norm(x, g):
    xf = x.astype(jnp.float32)
    y = xf * lax.rsqrt(jnp.mean(xf * xf, axis=-1, keepdims=True) + NORM_EPS)
    return (y * g.astype(jnp.float32)).astype(x.dtype)


def l2_normalise(x):
    xf = x.astype(jnp.float32)
    return xf * lax.rsqrt(jnp.sum(xf * xf, axis=-1, keepdims=True) + NORM_EPS)


def dwconv_centred(x, w):
    K = w.shape[0]
    r = K // 2
    S = x.shape[1]
    xp = jnp.pad(x, ((0, 0), (r, r), (0, 0)))
    return sum(xp[:, j:j + S] * w[j] for j in range(K))


def rope_angles(pos, dim, theta):
    inv = theta ** (-jnp.arange(0, dim, 2, dtype=jnp.float32) / dim)
    return pos[:, None] * inv[None, :]


def rotate_half_rope(x, ang):
    d2 = x.shape[-1] // 2
    c = jnp.cos(ang).astype(x.dtype)
    s = jnp.sin(ang).astype(x.dtype)
    x1, x2 = x[..., :d2], x[..., d2:]
    return jnp.concatenate([x1 * c - x2 * s, x2 * c + x1 * s], axis=-1)


def axial_rope(x):
    S = x.shape[1]
    rows = S // GRID_W
    row = jnp.broadcast_to(jnp.arange(rows, dtype=jnp.float32)[:, None], (rows, GRID_W)).reshape(S)
    col = jnp.broadcast_to(jnp.arange(GRID_W, dtype=jnp.float32)[None, :], (rows, GRID_W)).reshape(S)
    half = x.shape[-1] // 2
    ang_r = rope_angles(row, half, AXIAL_THETA)[:, None, :]
    ang_c = rope_angles(col, half, AXIAL_THETA)[:, None, :]
    return jnp.concatenate([rotate_half_rope(x[..., :half], ang_r), rotate_half_rope(x[..., half:], ang_c)], axis=-1)


def sweep_query_blocks(fn, q):
    B, S = q.shape[0], q.shape[1]
    nb = S // Q_BLOCK
    qb = jnp.moveaxis(q.reshape((B, nb, Q_BLOCK) + q.shape[2:]), 1, 0)
    out = jnp.moveaxis(lax.map(fn, qb), 0, 1)
    return out.reshape((B, S) + out.shape[3:])


def gated_delta_chunked(q, k, v, g, beta):
    f32 = jnp.float32
    B, S, H, dk = q.shape
    dv = v.shape[-1]
    C = DELTA_CHUNK
    n = S // C

    def chunks(t):
        return t.astype(f32).reshape(B, n, C, H, -1).transpose(0, 3, 1, 2, 4)

    q = chunks(q) * (dk ** -0.5)
    k = chunks(k)
    v = chunks(v)
    g = jnp.cumsum(g.astype(f32).reshape(B, n, C, H).transpose(0, 3, 1, 2), axis=-1)
    beta = beta.astype(f32).reshape(B, n, C, H).transpose(0, 3, 1, 2)[..., None]
    k_beta = k * beta
    v_beta = v * beta
    incl = jnp.tril(jnp.ones((C, C), dtype=bool))
    strict = jnp.tril(jnp.ones((C, C), dtype=bool), -1)
    diff = g[..., :, None] - g[..., None, :]
    decay = jnp.where(incl, jnp.exp(jnp.where(incl, diff, 0.0)), 0.0)
    lower = jnp.where(strict, jnp.einsum('bhncd,bhnsd->bhncs', k_beta, k) * decay, 0.0)
    a = lower + jnp.eye(C, dtype=f32)
    rhs = jnp.concatenate([v_beta, k_beta * jnp.exp(g)[..., None]], axis=-1)
    sol = lax.linalg.triangular_solve(a, rhs, left_side=True, lower=True, unit_diagonal=True)
    u, w = sol[..., :dv], sol[..., dv:]
    intra = jnp.where(incl, jnp.einsum('bhncd,bhnsd->bhncs', q, k) * decay, 0.0)
    g_last = g[..., -1]
    q_dec = q * jnp.exp(g)[..., None]
    k_dec = k * jnp.exp(g_last[..., None] - g)[..., None]

    def step(state, xs):
        q_n, k_n, u_n, w_n, a_n, gl_n = xs
        v_new = u_n - jnp.einsum('bhcd,bhde->bhce', w_n, state)
        o = jnp.einsum('bhcd,bhde->bhce', q_n, state) + jnp.einsum('bhcs,bhse->bhce', a_n, v_new)
        state = state * jnp.exp(gl_n)[..., None, None] + jnp.einsum('bhcd,bhce->bhde', k_n, v_new)
        return state, o

    xs = tuple(jnp.moveaxis(t, 2, 0) for t in (q_dec, k_dec, u, w, intra, g_last))
    state0 = jnp.zeros((B, H, dk, dv), f32)
    _, o = lax.scan(step, state0, xs)
    return o.transpose(1, 0, 3, 2, 4).reshape(B, S, H, dv)


def gdn_branch(qkv, z, alpha, beta_logit, conv_w, a_log, dt_bias, norm_w):
    B, S, _ = qkv.shape
    qkv = jax.nn.silu(dwconv_centred(qkv, conv_w))
    q, k, v = jnp.split(qkv, 3, axis=-1)
    q = l2_normalise(q.reshape(B, S, HA, DA))
    k = l2_normalise(k.reshape(B, S, HA, DA))
    v = v.reshape(B, S, HA, DA)
    g = -jnp.exp(a_log.astype(jnp.float32)) * jax.nn.softplus(alpha.astype(jnp.float32) + dt_bias.astype(jnp.float32))
    bt = jax.nn.sigmoid(beta_logit.astype(jnp.float32))
    o_fwd = gated_delta_chunked(q, k, v, g[:, :, 0], bt[:, :, 0])
    flip = lambda t: jnp.flip(t, axis=1)
    o_bwd = flip(gated_delta_chunked(flip(q), flip(k), flip(v), flip(g[:, :, 1]), flip(bt[:, :, 1])))
    o = rms_norm(o_fwd + o_bwd, norm_w) * jax.nn.silu(z.reshape(B, S, HA, DA).astype(jnp.float32))
    return o.reshape(B, S, A_WIDTH).astype(qkv.dtype)


def gqa_axial_branch(q, k, v, qn, kn):
    B, S, _ = q.shape
    q = axial_rope(rms_norm(q.reshape(B, S, HB, DB), qn))
    k = axial_rope(rms_norm(k.reshape(B, S, KVB, DB), kn))
    v = v.reshape(B, S, KVB, DB)
    q = q.reshape(B, S, KVB, HB // KVB, DB)
    scale = DB ** -0.5

    def block(qb):
        s = jnp.einsum('bqhgd,bshd->bhgqs', qb, k).astype(jnp.float32) * scale
        pr = jax.nn.softmax(s, axis=-1).astype(v.dtype)
        return jnp.einsum('bhgqs,bshd->bqhgd', pr, v)

    o = sweep_query_blocks(block, q)
    return o.reshape(B, S, HB * DB)


def diff_branch(q, k, v, qn, kn, lambdas, subln, lambda_init):
    B, S, _ = q.shape
    pos = jnp.arange(S, dtype=jnp.float32)
    ang = rope_angles(pos, ROT_C, ROPE_THETA)[:, None, None, :]

    def prep(t, gn):
        t = rms_norm(t.reshape(B, S, HC, 2, DC), gn)
        return jnp.concatenate([rotate_half_rope(t[..., :ROT_C], ang), t[..., ROT_C:]], axis=-1)

    q = prep(q, qn)
    k = prep(k, kn)
    v = v.reshape(B, S, HC, DVC)
    lf = lambdas.astype(jnp.float32)
    lam = jnp.exp(jnp.sum(lf[0] * lf[1])) - jnp.exp(jnp.sum(lf[2] * lf[3])) + lambda_init
    scale = DC ** -0.5

    def block(qb):
        s = jnp.einsum('bqhcd,bshcd->bhcqs', qb, k).astype(jnp.float32) * scale
        pr = jax.nn.softmax(s, axis=-1)
        a = (pr[:, :, 0] - lam * pr[:, :, 1]).astype(v.dtype)
        return jnp.einsum('bhqs,bshd->bqhd', a, v)

    o = sweep_query_blocks(block, q)
    o = rms_norm(o, subln) * (1.0 - lambda_init)
    return o.reshape(B, S, HC * DVC)


def encoder_layer(x, p_emb, prm, i):
    B, S, _ = x.shape
    h = rms_norm(x, prm['norm_mix'][i])
    proj = h @ prm['w_in'][i]
    offs = [int(o) for o in np.cumsum(IN_SIZES)[:-1]]
    a_qkv, a_z, a_alpha, a_beta, b_q, b_k, b_v, c_q, c_k, c_v, gate_logits = jnp.split(proj, offs, axis=-1)
    y_a = gdn_branch(a_qkv, a_z, a_alpha.reshape(B, S, 2, HA), a_beta.reshape(B, S, 2, HA),
                     prm['conv_a'][i], prm['a_log'][i], prm['dt_bias'][i], prm['norm_a'][i]) @ prm['w_o_a'][i]
    y_b = gqa_axial_branch(b_q, b_k, b_v, prm['qn_b'][i], prm['kn_b'][i]) @ prm['w_o_b'][i]
    lambda_init = 0.8 - 0.6 * math.exp(-0.3 * i)
    y_c = diff_branch(c_q, c_k, c_v, prm['qn_c'][i], prm['kn_c'][i], prm['lambdas_c'][i],
                      prm['subln_c'][i], lambda_init) @ prm['w_o_c'][i]
    g_a, g_b, g_c = jnp.split(jax.nn.sigmoid(gate_logits), 3, axis=-1)
    x = x + (g_a * y_a + g_b * y_b + g_c * y_c) @ prm['w_out'][i]
    h2 = rms_norm(x, prm['norm_ffn'][i])
    u = dwconv_centred(h2 @ prm['w_up'][i], prm['conv_ffn'][i]) + prm['conv_ffn_b'][i]
    u_gate, u_val = jnp.split(u, 2, axis=-1)
    x = x + (jax.nn.silu(u_gate) * u_val) @ prm['w_down'][i]
    x = x + jax.nn.sigmoid(x @ prm['w_ple_gate'][i]) * (p_emb @ prm['w_ple'][i])
    return x


def encoder_trunk(x, p, prm):
    for i in range(DEPTH):
        x = encoder_layer(x, p[i], prm, i)
    return x


def setup_inputs(seed: int = 0) -> dict:
    key = jax.random.key(seed)
    ks = iter(jax.random.split(key, 32))
    f32 = jnp.float32
    L = DEPTH

    def nrm(shape, scale):
        return jax.random.normal(next(ks), shape, f32) * scale

    def gain(shape):
        return 1.0 + 0.05 * jax.random.normal(next(ks), shape, f32)

    return {
        'x_prompt': nrm((BATCH, SEQ, D_MODEL), 1.0),
        'x_sample': nrm((DEC_BATCH, DEC_SEQ, D_MODEL), 1.0),
        'p_prompt': nrm((DEPTH, BATCH, SEQ, PLE_DIM), 1.0),
        'p_sample': nrm((DEPTH, DEC_BATCH, DEC_SEQ, PLE_DIM), 1.0),
        'norm_mix': gain((L, D_MODEL)),
        'w_in': nrm((L, D_MODEL, N_IN), D_MODEL ** -0.5),
        'conv_a': nrm((L, SHORT_CONV, 3 * A_WIDTH), SHORT_CONV ** -0.5),
        'a_log': jnp.log(jax.random.uniform(next(ks), (L, 2, HA), f32, 1.0, 16.0)),
        'dt_bias': nrm((L, 2, HA), 0.1),
        'norm_a': gain((L, DA)),
        'qn_b': gain((L, DB)),
        'kn_b': gain((L, DB)),
        'qn_c': gain((L, DC)),
        'kn_c': gain((L, DC)),
        'lambdas_c': nrm((L, 4, DC), 0.1),
        'subln_c': gain((L, DVC)),
        'w_o_a': nrm((L, A_WIDTH, D_MODEL), A_WIDTH ** -0.5),
        'w_o_b': nrm((L, HB * DB, D_MODEL), (HB * DB) ** -0.5),
        'w_o_c': nrm((L, HC * DVC, D_MODEL), (HC * DVC) ** -0.5),
        'w_out': nrm((L, D_MODEL, D_MODEL), D_MODEL ** -0.5),
        'norm_ffn': gain((L, D_MODEL)),
        'w_up': nrm((L, D_MODEL, 2 * D_FF), D_MODEL ** -0.5),
        'conv_ffn': nrm((L, FFN_CONV, 2 * D_FF), FFN_CONV ** -0.5),
        'conv_ffn_b': nrm((L, 2 * D_FF), 0.01),
        'w_down': nrm((L, D_FF, D_MODEL), D_FF ** -0.5),
        'w_ple': nrm((L, PLE_DIM, D_MODEL), PLE_DIM ** -0.5),
        'w_ple_gate': nrm((L, D_MODEL, D_MODEL), D_MODEL ** -0.5),
    }


def reference(x_prompt, x_sample, p_prompt, p_sample, norm_mix, w_in, conv_a, a_log, dt_bias, norm_a,
              qn_b, kn_b, qn_c, kn_c, lambdas_c, subln_c, w_o_a, w_o_b, w_o_c, w_out, norm_ffn,
              w_up, conv_ffn, conv_ffn_b, w_down, w_ple, w_ple_gate):
    prm = dict(norm_mix=norm_mix, w_in=w_in, conv_a=conv_a, a_log=a_log, dt_bias=dt_bias, norm_a=norm_a,
               qn_b=qn_b, kn_b=kn_b, qn_c=qn_c, kn_c=kn_c, lambdas_c=lambdas_c, subln_c=subln_c,
               w_o_a=w_o_a, w_o_b=w_o_b, w_o_c=w_o_c, w_out=w_out, norm_ffn=norm_ffn,
               w_up=w_up, conv_ffn=conv_ffn, conv_ffn_b=conv_ffn_b, w_down=w_down,
               w_ple=w_ple, w_ple_gate=w_ple_gate)
    y_prompt = encoder_trunk(x_prompt, p_prompt, prm)
    y_sample = encoder_trunk(x_sample, p_sample, prm)
    return (y_prompt, y_sample)
```

```python
import functools
import math

import jax
import jax.numpy as jnp
from jax import lax
from jax.experimental import pallas as pl
from jax.experimental.pallas import tpu as pltpu

F32 = jnp.float32
BF16 = jnp.bfloat16
HIGHEST = lax.Precision.HIGHEST

D_MODEL = 2048
DEPTH = 2
GRID_W = 64
PLE_DIM = 256
NORM_EPS = 1e-6
HA = 8
DA = 128
A_WIDTH = HA * DA
DELTA_CHUNK = 64
HB = 8
KVB = 2
DB = 128
AXIAL_THETA = 10000.0
HC = 8
DC = 64
DVC = 2 * DC
ROT_C = DC // 4
ROPE_THETA = 500000.0
D_FF = 5632

LANES = 128
SUBLANES = 8
VMEM_LIMIT_BYTES = 56 * 1024 * 1024

OFF_QKV = 0
OFF_Z = 3072
OFF_AB = 4096
AB_WIDTH = 512
OFF_BQ = 4608
OFF_BV = 5888
OFF_CQ = 6144
OFF_CV = 8192
OFF_GATE = 9216
N_PROJ = 15360
GDN_HEAD_GROUP = 4
GDN_BLOCK = 256


def _cparams(semantics):
    return pltpu.CompilerParams(dimension_semantics=semantics, vmem_limit_bytes=VMEM_LIMIT_BYTES)


def _tile(n, pref):
    t = min(n, pref)
    while n % t:
        t //= 2
    return t


def _sigmoid(x):
    return 1.0 / (1.0 + jnp.exp(-x))


def _softplus(x):
    return jnp.maximum(x, 0.0) + jnp.log(1.0 + jnp.exp(-jnp.abs(x)))


def _norm_matmul_kernel(x_ref, g_ref, w_ref, o_ref, h_ref, *, rows):
    @pl.when(pl.program_id(1) == 0)
    def _():
        def chunk(r, c):
            rs = pl.ds(pl.multiple_of(r * rows, rows), rows)
            x = x_ref[rs, :]
            ms = jnp.mean(x * x, axis=-1, keepdims=True)
            h_ref[rs, :] = (x * lax.rsqrt(ms + NORM_EPS) * g_ref[...]).astype(BF16)
            return c
        lax.fori_loop(0, x_ref.shape[0] // rows, chunk, 0)

    o_ref[...] = jnp.dot(h_ref[...], w_ref[...], preferred_element_type=F32).astype(o_ref.dtype)


def norm_matmul(x, g, w, name):
    m, k = x.shape
    n = w.shape[1]
    tm = _tile(m, 1024)
    tn = 768 if n % 768 == 0 else _tile(n, 1024)
    rows = _tile(tm, 128)
    return pl.pallas_call(
        functools.partial(_norm_matmul_kernel, rows=rows),
        grid=(m // tm, n // tn),
        in_specs=[
            pl.BlockSpec((tm, k), lambda i, j: (i, 0)),
            pl.BlockSpec((1, k), lambda i, j: (0, 0)),
            pl.BlockSpec((k, tn), lambda i, j: (0, j)),
        ],
        out_specs=pl.BlockSpec((tm, tn), lambda i, j: (i, j)),
        out_shape=jax.ShapeDtypeStruct((m, n), BF16),
        scratch_shapes=[pltpu.VMEM((tm, k), BF16)],
        compiler_params=_cparams(("parallel", "arbitrary")),
        name=name,
    )(x, g.reshape(1, k), w)


def _res_matmul_kernel(a_ref, w_ref, r_ref, o_ref):
    o_ref[...] = r_ref[...] + jnp.dot(a_ref[...], w_ref[...], preferred_element_type=F32)


def res_matmul(a, w, res, name):
    m, k = a.shape
    n = w.shape[1]
    tm = _tile(m, 1024)
    tn = _tile(n, 512)
    return pl.pallas_call(
        _res_matmul_kernel,
        grid=(m // tm, n // tn),
        in_specs=[
            pl.BlockSpec((tm, k), lambda i, j: (i, 0)),
            pl.BlockSpec((k, tn), lambda i, j: (0, j)),
            pl.BlockSpec((tm, tn), lambda i, j: (i, j)),
        ],
        out_specs=pl.BlockSpec((tm, tn), lambda i, j: (i, j)),
        out_shape=jax.ShapeDtypeStruct((m, n), F32),
        compiler_params=_cparams(("parallel", "arbitrary")),
        name=name,
    )(a, w, res)


def _merge_kernel(ya_ref, yb_ref, yc_ref, ga_ref, gb_ref, gc_ref, wa_ref, wb_ref, wc_ref, o_ref):
    def branch(y_ref, g_ref, w_ref):
        return _sigmoid(g_ref[...].astype(F32)) * jnp.dot(y_ref[...], w_ref[...], preferred_element_type=F32)

    o_ref[...] = (branch(ya_ref, ga_ref, wa_ref) + branch(yb_ref, gb_ref, wb_ref)
                  + branch(yc_ref, gc_ref, wc_ref)).astype(o_ref.dtype)


def merge_branches(ya, yb, yc, proj, wa, wb, wc, name):
    m, k = ya.shape
    n = wa.shape[1]
    tm = _tile(m, 512)
    tn = _tile(n, 1024)
    nj = n // tn
    gate_blk = OFF_GATE // tn

    def y_spec():
        return pl.BlockSpec((tm, k), lambda j, i: (i, 0))

    def g_spec(which):
        return pl.BlockSpec((tm, tn), lambda j, i, which=which: (i, gate_blk + which * nj + j))

    def w_spec():
        return pl.BlockSpec((k, tn), lambda j, i: (0, j))

    return pl.pallas_call(
        _merge_kernel,
        grid=(nj, m // tm),
        in_specs=[y_spec(), y_spec(), y_spec(), g_spec(0), g_spec(1), g_spec(2), w_spec(), w_spec(), w_spec()],
        out_specs=pl.BlockSpec((tm, tn), lambda j, i: (i, j)),
        out_shape=jax.ShapeDtypeStruct((m, n), BF16),
        compiler_params=_cparams(("parallel", "arbitrary")),
        name=name,
    )(ya, yb, yc, proj, proj, proj, wa, wb, wc)


def _ple_kernel(x_ref, xres_ref, p_ref, wg_ref, we_ref, o_ref, xb_ref, pb_ref):
    @pl.when(pl.program_id(1) == 0)
    def _():
        xb_ref[...] = x_ref[...].astype(BF16)
        pb_ref[...] = p_ref[...].astype(BF16)

    gate = _sigmoid(jnp.dot(xb_ref[...], wg_ref[...], preferred_element_type=F32))
    emb = jnp.dot(pb_ref[...], we_ref[...], preferred_element_type=F32)
    o_ref[...] = xres_ref[...] + gate * emb


def ple_update(x, p, wg, we, name):
    m, k = x.shape
    n = wg.shape[1]
    kp = p.shape[1]
    tm = _tile(m, 1024)
    tn = _tile(n, 512)
    return pl.pallas_call(
        _ple_kernel,
        grid=(m // tm, n // tn),
        in_specs=[
            pl.BlockSpec((tm, k), lambda i, j: (i, 0)),
            pl.BlockSpec((tm, tn), lambda i, j: (i, j)),
            pl.BlockSpec((tm, kp), lambda i, j: (i, 0)),
            pl.BlockSpec((k, tn), lambda i, j: (0, j)),
            pl.BlockSpec((kp, tn), lambda i, j: (0, j)),
        ],
        out_specs=pl.BlockSpec((tm, tn), lambda i, j: (i, j)),
        out_shape=jax.ShapeDtypeStruct((m, n), F32),
        scratch_shapes=[pltpu.VMEM((tm, k), BF16), pltpu.VMEM((tm, kp), BF16)],
        compiler_params=_cparams(("parallel", "arbitrary")),
        name=name,
    )(x, x, p, wg, we)


def _conv3(x, prev_row, next_row, w_ref, lanes):
    ts = x.shape[0]
    r = lax.broadcasted_iota(jnp.int32, (ts, 1), 0)
    xp = jnp.where(r == 0, prev_row, pltpu.roll(x, 1, 0))
    xn = jnp.where(r == ts - 1, next_row, pltpu.roll(x, ts - 1, 0))
    return xp * w_ref[0:1, lanes] + x * w_ref[1:2, lanes] + xn * w_ref[2:3, lanes]


def _halo_specs(batch, seq, ts, width, col_fn):
    n_s = seq // ts
    per_seq = seq // SUBLANES
    per_tile = ts // SUBLANES
    last = batch * per_seq - 1

    def prev_map(b, i, *rest):
        return (jnp.maximum(b * per_seq + i * per_tile - 1, 0), col_fn(*rest))

    def next_map(b, i, *rest):
        return (jnp.minimum(b * per_seq + (i + 1) * per_tile, last), col_fn(*rest))

    return pl.BlockSpec((SUBLANES, width), prev_map), pl.BlockSpec((SUBLANES, width), next_map), n_s


def _ffn_act_kernel(ug_ref, ugp_ref, ugn_ref, uv_ref, uvp_ref, uvn_ref, wg_ref, wv_ref, bg_ref, bv_ref, o_ref, *, n_s):
    i = pl.program_id(1)
    has_prev = (i > 0).astype(F32)
    has_next = (i < n_s - 1).astype(F32)
    all_lanes = slice(None)

    def conv(u_ref, p_ref, n_ref, w_ref, b_ref):
        x = u_ref[...].astype(F32)
        prev_row = p_ref[SUBLANES - 1:SUBLANES, :].astype(F32) * has_prev
        next_row = n_ref[0:1, :].astype(F32) * has_next
        return _conv3(x, prev_row, next_row, w_ref, all_lanes) + b_ref[...]

    gate = conv(ug_ref, ugp_ref, ugn_ref, wg_ref, bg_ref)
    val = conv(uv_ref, uvp_ref, uvn_ref, wv_ref, bv_ref)
    o_ref[...] = (gate * _sigmoid(gate) * val).astype(o_ref.dtype)


def ffn_act(u, conv_w, conv_b, batch, seq, name):
    t, two_f = u.shape
    f = two_f // 2
    ts = _tile(seq, 256)
    tc = 512
    nj = f // tc
    prev_g, next_g, n_s = _halo_specs(batch, seq, ts, tc, lambda j: j)
    prev_v, next_v, _ = _halo_specs(batch, seq, ts, tc, lambda j: nj + j)
    main_g = pl.BlockSpec((ts, tc), lambda b, i, j: (b * n_s + i, j))
    main_v = pl.BlockSpec((ts, tc), lambda b, i, j: (b * n_s + i, nj + j))
    w_g = pl.BlockSpec((3, tc), lambda b, i, j: (0, j))
    w_v = pl.BlockSpec((3, tc), lambda b, i, j: (0, nj + j))
    b_g = pl.BlockSpec((1, tc), lambda b, i, j: (0, j))
    b_v = pl.BlockSpec((1, tc), lambda b, i, j: (0, nj + j))
    cb = conv_b.reshape(1, two_f)
    return pl.pallas_call(
        functools.partial(_ffn_act_kernel, n_s=n_s),
        grid=(batch, n_s, nj),
        in_specs=[main_g, prev_g, next_g, main_v, prev_v, next_v, w_g, w_v, b_g, b_v],
        out_specs=pl.BlockSpec((ts, tc), lambda b, i, j: (b * n_s + i, j)),
        out_shape=jax.ShapeDtypeStruct((t, f), BF16),
        compiler_params=_cparams(("parallel", "parallel", "arbitrary")),
        name=name,
    )(u, u, u, u, u, u, conv_w, conv_w, cb, cb)


def _gdn_prep_kernel(x_ref, xp_ref, xn_ref, ab_ref, w_ref, alog_ref, dtb_ref, q_ref, k_ref, v_ref, gb_ref, *, n_s):
    i = pl.program_id(1)
    has_prev = (i > 0).astype(F32)
    has_next = (i < n_s - 1).astype(F32)
    for c in range(3 * HA):
        lanes = slice(c * DA, (c + 1) * DA)
        x = x_ref[:, lanes].astype(F32)
        prev_row = xp_ref[SUBLANES - 1:SUBLANES, lanes].astype(F32) * has_prev
        next_row = xn_ref[0:1, lanes].astype(F32) * has_next
        y = _conv3(x, prev_row, next_row, w_ref, lanes)
        y = y * _sigmoid(y)
        out_lanes = slice((c % HA) * DA, (c % HA + 1) * DA)
        if c < 2 * HA:
            y = y * lax.rsqrt(jnp.sum(y * y, axis=-1, keepdims=True) + NORM_EPS)
            (q_ref if c < HA else k_ref)[:, out_lanes] = y.astype(BF16)
        else:
            v_ref[:, out_lanes] = y.astype(BF16)
    ab = ab_ref[...].astype(F32)
    lane = lax.broadcasted_iota(jnp.int32, (1, AB_WIDTH), 1) % LANES
    g = -jnp.exp(alog_ref[...]) * _softplus(ab + dtb_ref[...])
    gb_ref[...] = jnp.where(lane < GDN_HEAD_GROUP, g, _sigmoid(ab))


def gdn_prep(proj, conv_w, alog_row, dtb_row, batch, seq, name):
    t = proj.shape[0]
    ts = _tile(seq, 256)
    w3 = 3 * A_WIDTH
    prev_s, next_s, n_s = _halo_specs(batch, seq, ts, w3, lambda: 0)
    row = lambda b, i: (b * n_s + i, 0)
    out_spec = pl.BlockSpec((ts, A_WIDTH), row)
    return pl.pallas_call(
        functools.partial(_gdn_prep_kernel, n_s=n_s),
        grid=(batch, n_s),
        in_specs=[
            pl.BlockSpec((ts, w3), row), prev_s, next_s,
            pl.BlockSpec((ts, AB_WIDTH), lambda b, i: (b * n_s + i, OFF_AB // AB_WIDTH)),
            pl.BlockSpec((3, w3), lambda b, i: (0, 0)),
            pl.BlockSpec((1, AB_WIDTH), lambda b, i: (0, 0)),
            pl.BlockSpec((1, AB_WIDTH), lambda b, i: (0, 0)),
        ],
        out_specs=[out_spec, out_spec, out_spec, pl.BlockSpec((ts, AB_WIDTH), row)],
        out_shape=[jax.ShapeDtypeStruct((t, A_WIDTH), BF16)] * 3 + [jax.ShapeDtypeStruct((t, AB_WIDTH), F32)],
        compiler_params=_cparams(("parallel", "parallel")),
        name=name,
    )(proj, proj, proj, proj, conv_w, alog_row, dtb_row)


def _gdn_scan_kernel(q_ref, k_ref, v_ref, gb_ref, o_ref, s_ref, *, n_chunks):
    c_len = DELTA_CHUNK
    d = pl.program_id(1)

    @pl.when(pl.program_id(3) == 0)
    def _():
        s_ref[...] = jnp.zeros_like(s_ref)

    sgn = 1 - 2 * d
    row = lax.broadcasted_iota(jnp.int32, (c_len, c_len), 0)
    col = lax.broadcasted_iota(jnp.int32, (c_len, c_len), 1)
    order = (row - col) * sgn
    incl = order >= 0
    strict = order > 0
    cum_mat = incl.astype(F32)
    eye = (row == col).astype(F32)
    sel = (lax.broadcasted_iota(jnp.int32, (SUBLANES, LANES), 0)
           == lax.broadcasted_iota(jnp.int32, (SUBLANES, LANES), 1)).astype(F32)
    scale = DA ** -0.5
    nt = (((1,), (1,)), ((), ()))
    tn = (((0,), (0,)), ((), ()))

    for j in range(n_chunks):
        c = j + d * (n_chunks - 1 - 2 * j)
        rows = pl.ds(pl.multiple_of(c * c_len, c_len), c_len)
        gblk = gb_ref[rows, :]
        gcol = jnp.dot(cum_mat, gblk, precision=HIGHEST, preferred_element_type=F32)
        grow = lax.dot_general(sel, gcol, nt, precision=HIGHEST, preferred_element_type=F32)
        gtot = jnp.sum(gblk, axis=0, keepdims=True)
        exp_g = jnp.exp(gcol)
        exp_rest = jnp.exp(gtot - gcol)
        exp_tot = jnp.exp(gtot)
        for h in range(GDN_HEAD_GROUP):
            lanes = slice(h * DA, (h + 1) * DA)
            qf = q_ref[rows, lanes].astype(F32)
            kf = k_ref[rows, lanes].astype(F32)
            vf = v_ref[rows, lanes].astype(F32)
            beta = gblk[:, GDN_HEAD_GROUP + h:GDN_HEAD_GROUP + h + 1]
            eg = exp_g[:, h:h + 1]
            diff = gcol[:, h:h + 1] - grow[h:h + 1, :]
            decay = jnp.where(incl, jnp.exp(jnp.where(incl, diff, 0.0)), 0.0)
            kb = kf * beta
            kbf = k_ref[rows, lanes]
            kk = lax.dot_general(kb.astype(BF16), kbf, nt, preferred_element_type=F32)
            low = jnp.where(strict, kk * decay, 0.0)
            qk = lax.dot_general((qf * scale).astype(BF16), kbf, nt, preferred_element_type=F32)
            intra = jnp.where(incl, qk * decay, 0.0)
            inv = eye - low
            power = low
            for _ in range(5):
                pb = power.astype(BF16)
                power = jnp.dot(pb, pb, preferred_element_type=F32)
                inv = inv + jnp.dot(inv.astype(BF16), power.astype(BF16), preferred_element_type=F32)
            rhs = jnp.concatenate([vf * beta, kb * eg], axis=1).astype(BF16)
            uw = jnp.dot(inv.astype(BF16), rhs, preferred_element_type=F32)
            u = uw[:, :DA]
            w = uw[:, DA:]
            state = s_ref[h]
            sb = state.astype(BF16)
            v_new = u - jnp.dot(w.astype(BF16), sb, preferred_element_type=F32)
            vnb = v_new.astype(BF16)
            q_dec = (qf * scale * eg).astype(BF16)
            out = (jnp.dot(q_dec, sb, preferred_element_type=F32)
                   + jnp.dot(intra.astype(BF16), vnb, preferred_element_type=F32))
            k_dec = (kf * exp_rest[:, h:h + 1]).astype(BF16)
            s_ref[h] = state * exp_tot[:, h:h + 1] + lax.dot_general(k_dec, vnb, tn, preferred_element_type=F32)
            o_ref[0, rows, lanes] = out.astype(o_ref.dtype)


def gdn_scan(q, k, v, gb, batch, seq, name):
    t = q.shape[0]
    blk = GDN_BLOCK
    n_blk = seq // blk
    width = GDN_HEAD_GROUP * DA
    n_hg = HA // GDN_HEAD_GROUP

    def rblk(b, d, i):
        return b * n_blk + i + d * (n_blk - 1 - 2 * i)

    qkv_spec = pl.BlockSpec((blk, width), lambda b, d, g, i: (rblk(b, d, i), g))
    return pl.pallas_call(
        functools.partial(_gdn_scan_kernel, n_chunks=blk // DELTA_CHUNK),
        grid=(batch, 2, n_hg, n_blk),
        in_specs=[qkv_spec, qkv_spec, qkv_spec,
                  pl.BlockSpec((blk, LANES), lambda b, d, g, i: (rblk(b, d, i), d * n_hg + g))],
        out_specs=pl.BlockSpec((1, blk, width), lambda b, d, g, i: (d, rblk(b, d, i), g)),
        out_shape=jax.ShapeDtypeStruct((2, t, A_WIDTH), BF16),
        scratch_shapes=[pltpu.VMEM((GDN_HEAD_GROUP, DA, DA), F32)],
        compiler_params=_cparams(("parallel", "parallel", "parallel", "arbitrary")),
        name=name,
    )(q, k, v, gb)


def _gdn_post_kernel(o_ref, z_ref, g_ref, y_ref):
    for h in range(HA):
        lanes = slice(h * DA, (h + 1) * DA)
        o = o_ref[0, :, lanes].astype(F32) + o_ref[1, :, lanes].astype(F32)
        z = z_ref[:, lanes].astype(F32)
        ms = jnp.mean(o * o, axis=-1, keepdims=True)
        y = o * lax.rsqrt(ms + NORM_EPS) * g_ref[...]
        y_ref[:, lanes] = (y * (z * _sigmoid(z))).astype(y_ref.dtype)


def gdn_post(o2, proj, norm_w, name):
    t = proj.shape[0]
    ts = _tile(t, 512)
    return pl.pallas_call(
        _gdn_post_kernel,
        grid=(t // ts,),
        in_specs=[
            pl.BlockSpec((2, ts, A_WIDTH), lambda i: (0, i, 0)),
            pl.BlockSpec((ts, A_WIDTH), lambda i: (i, OFF_Z // A_WIDTH)),
            pl.BlockSpec((1, DA), lambda i: (0, 0)),
        ],
        out_specs=pl.BlockSpec((ts, A_WIDTH), lambda i: (i, 0)),
        out_shape=jax.ShapeDtypeStruct((t, A_WIDTH), BF16),
        compiler_params=_cparams(("parallel",)),
        name=name,
    )(o2, proj, norm_w.reshape(1, DA))


def _qk_prep_kernel(x_ref, cos_ref, sa_ref, sb_ref, gq_ref, gk_ref, o_ref, *, n_q, seg, shift, q_scale, split):
    is_q = pl.program_id(1) < n_q
    x = x_ref[...].astype(F32)
    sq = x * x
    if seg == LANES:
        ms = jnp.mean(sq, axis=-1, keepdims=True)
    else:
        grp_r = lax.broadcasted_iota(jnp.int32, (LANES, LANES), 0) // seg
        grp_c = lax.broadcasted_iota(jnp.int32, (LANES, LANES), 1) // seg
        ms = jnp.dot(sq, (grp_r == grp_c).astype(F32), precision=HIGHEST, preferred_element_type=F32) * (1.0 / seg)
    gain = jnp.where(is_q, gq_ref[...], gk_ref[...])
    y = x * lax.rsqrt(ms + NORM_EPS) * gain
    y = y * cos_ref[...] + pltpu.roll(y, LANES - shift, 1) * sa_ref[...] + pltpu.roll(y, shift, 1) * sb_ref[...]
    y = y * jnp.where(is_q, q_scale, 1.0)
    if not split:
        o_ref[0, 0] = y.astype(o_ref.dtype)
    else:
        lane = lax.broadcasted_iota(jnp.int32, (1, LANES), 1)
        o_ref[0, 0] = jnp.where(jnp.logical_or(lane < seg, jnp.logical_not(is_q)), y, 0.0).astype(o_ref.dtype)
        o_ref[0, 1] = jnp.where(jnp.logical_and(lane >= seg, is_q), y, 0.0).astype(o_ref.dtype)


def qk_prep(proj, tables, gq, gk, batch, seq, *, col_off, n_q, n_k, seg, shift, q_scale, split, name):
    ts = _tile(seq, 512)
    n_s = seq // ts
    per = 2 if split else 1
    col0 = col_off // LANES
    tab = pl.BlockSpec((ts, LANES), lambda b, h, i: (i, 0))
    gain = pl.BlockSpec((1, LANES), lambda b, h, i: (0, 0))
    return pl.pallas_call(
        functools.partial(_qk_prep_kernel, n_q=n_q, seg=seg, shift=shift, q_scale=q_scale, split=split),
        grid=(batch, n_q + n_k, n_s),
        in_specs=[pl.BlockSpec((ts, LANES), lambda b, h, i: (b * n_s + i, col0 + h)), tab, tab, tab, gain, gain],
        out_specs=pl.BlockSpec((1, per, ts, LANES), lambda b, h, i: (b, h, i, 0)),
        out_shape=jax.ShapeDtypeStruct((batch, per * (n_q + n_k), seq, LANES), BF16),
        compiler_params=_cparams(("parallel", "parallel", "parallel")),
        name=name,
    )(proj, tables[0], tables[1], tables[2], gq.reshape(1, LANES), gk.reshape(1, LANES))


def _flash_kernel(*refs, groups, tq, tk, seq, diff, lambda_init):
    if diff:
        q_ref, k_ref, v_ref, lam_ref, sub_ref, o_ref, m_ref, l_ref, acc_ref = refs
    else:
        q_ref, k_ref, v_ref, o_ref, m_ref, l_ref, acc_ref = refs
    rows = groups * tq
    q = q_ref[0].reshape(rows, LANES)
    m_ref[...] = jnp.full(m_ref.shape, -jnp.inf, F32)
    l_ref[...] = jnp.zeros(l_ref.shape, F32)
    acc_ref[...] = jnp.zeros(acc_ref.shape, F32)
    nt = (((1,), (1,)), ((), ()))

    def body(c, carry):
        ks = pl.ds(pl.multiple_of(c * tk, tk), tk)
        kc = k_ref[0, 0, ks, :]
        vc = v_ref[ks, :]
        s = lax.dot_general(q, kc, nt, preferred_element_type=F32)
        m_prev = m_ref[...]
        m_new = jnp.maximum(m_prev, jnp.max(s, axis=-1, keepdims=True))
        p = jnp.exp(s - m_new)
        alpha = jnp.exp(m_prev - m_new)
        l_ref[...] = alpha * l_ref[...] + jnp.sum(p, axis=-1, keepdims=True)
        acc_ref[...] = alpha * acc_ref[...] + jnp.dot(p.astype(BF16), vc, preferred_element_type=F32)
        m_ref[...] = m_new
        return carry

    lax.fori_loop(0, seq // tk, body, 0)
    o = acc_ref[...] / l_ref[...]
    if not diff:
        for g in range(groups):
            o_ref[:, g * LANES:(g + 1) * LANES] = o[g * tq:(g + 1) * tq].astype(o_ref.dtype)
    else:
        lf = lam_ref[...]
        lam = (jnp.exp(jnp.sum(lf[0:1] * lf[1:2], axis=-1, keepdims=True))
               - jnp.exp(jnp.sum(lf[2:3] * lf[3:4], axis=-1, keepdims=True)) + lambda_init)
        dlt = o[:tq] - lam * o[tq:]
        ms = jnp.mean(dlt * dlt, axis=-1, keepdims=True)
        y = dlt * lax.rsqrt(ms + NORM_EPS) * sub_ref[...] * (1.0 - lambda_init)
        o_ref[...] = y.astype(o_ref.dtype)


def flash_attention(qk, proj, batch, seq, *, kv_heads, groups, k_slot0, k_stride, v_off, diff=False,
                    lambdas=None, subln=None, lambda_init=0.0, name):
    t = proj.shape[0]
    tq = _tile(seq, 1024 // groups)
    tk = _tile(seq, 512)
    n_q = seq // tq
    rows = groups * tq
    out_w = (1 if diff else groups) * LANES
    v_col0 = v_off // LANES
    in_specs = [
        pl.BlockSpec((1, groups, tq, LANES), lambda b, h, i: (b, h, i, 0)),
        pl.BlockSpec((1, 1, seq, LANES), lambda b, h, i: (b, k_slot0 + k_stride * h, 0, 0)),
        pl.BlockSpec((seq, LANES), lambda b, h, i: (b, v_col0 + h)),
    ]
    args = [qk, qk, proj]
    if diff:
        in_specs += [pl.BlockSpec(lambdas.shape, lambda b, h, i: (0, 0)),
                     pl.BlockSpec((1, LANES), lambda b, h, i: (0, 0))]
        args += [lambdas, subln.reshape(1, LANES)]
    return pl.pallas_call(
        functools.partial(_flash_kernel, groups=groups, tq=tq, tk=tk, seq=seq, diff=diff, lambda_init=lambda_init),
        grid=(batch, kv_heads, n_q),
        in_specs=in_specs,
        out_specs=pl.BlockSpec((tq, out_w), lambda b, h, i: (b * n_q + i, h)),
        out_shape=jax.ShapeDtypeStruct((t, kv_heads * out_w), BF16),
        scratch_shapes=[pltpu.VMEM((rows, 1), F32), pltpu.VMEM((rows, 1), F32), pltpu.VMEM((rows, LANES), F32)],
        compiler_params=_cparams(("parallel", "parallel", "arbitrary")),
        name=name,
    )(*args)


def _rope_angles(pos, dim, theta):
    inv = theta ** (-jnp.arange(0, dim, 2, dtype=F32) / dim)
    return pos[:, None] * inv[None, :]


def _axial_tables(seq):
    t = jnp.arange(seq)
    row = (t // GRID_W).astype(F32)
    col = (t % GRID_W).astype(F32)
    half = DB // 2
    ang_r = _rope_angles(row, half, AXIAL_THETA)
    ang_c = _rope_angles(col, half, AXIAL_THETA)
    ang = jnp.concatenate([ang_r, ang_r, ang_c, ang_c], axis=-1)
    cos, sin = jnp.cos(ang), jnp.sin(ang)
    first = (jnp.arange(LANES) % half) < (half // 2)
    return cos, jnp.where(first, -sin, 0.0), jnp.where(first, 0.0, sin)


def _partial_tables(seq):
    pos = jnp.arange(seq, dtype=F32)
    ang8 = _rope_angles(pos, ROT_C, ROPE_THETA)
    lane = jnp.arange(LANES) % DC
    ang = jnp.take(ang8, lane % (ROT_C // 2), axis=1)
    rot = lane < ROT_C
    first = lane < ROT_C // 2
    cos = jnp.where(rot, jnp.cos(ang), 1.0)
    sin = jnp.where(rot, jnp.sin(ang), 0.0)
    return cos, jnp.where(first, -sin, 0.0), jnp.where(first, 0.0, sin)


def _permute_w_in(w):
    k = w.shape[0]
    alpha0 = 4 * A_WIDTH
    beta0 = alpha0 + 2 * HA
    blocks = []
    for d in range(2):
        for g in range(HA // GDN_HEAD_GROUP):
            a0 = alpha0 + d * HA + g * GDN_HEAD_GROUP
            b0 = beta0 + d * HA + g * GDN_HEAD_GROUP
            blocks += [w[:, a0:a0 + GDN_HEAD_GROUP], w[:, b0:b0 + GDN_HEAD_GROUP],
                       jnp.zeros((k, LANES - 2 * GDN_HEAD_GROUP), w.dtype)]
    return jnp.concatenate([w[:, :alpha0]] + blocks + [w[:, beta0 + 2 * HA:]], axis=1)


def _gdn_param_row(p):
    blocks = []
    for d in range(2):
        for g in range(HA // GDN_HEAD_GROUP):
            blocks += [p[d, g * GDN_HEAD_GROUP:(g + 1) * GDN_HEAD_GROUP], jnp.zeros((LANES - GDN_HEAD_GROUP,), p.dtype)]
    return jnp.concatenate(blocks).reshape(1, AB_WIDTH)


def _encoder_layer(x, p_emb, w, li, batch, seq, tabs_b, tabs_c, tag):
    lambda_init = 0.8 - 0.6 * math.exp(-0.3 * li)
    proj = norm_matmul(x, w['norm_mix'], w['w_in'], f"in_proj_{tag}")

    q_a, k_a, v_a, gb = gdn_prep(proj, w['conv_a'], w['alog_row'], w['dtb_row'], batch, seq, f"gdn_prep_{tag}")
    o2 = gdn_scan(q_a, k_a, v_a, gb, batch, seq, f"gdn_scan_{tag}")
    y_a = gdn_post(o2, proj, w['norm_a'], f"gdn_post_{tag}")

    qk_b = qk_prep(proj, tabs_b, w['qn_b'], w['kn_b'], batch, seq, col_off=OFF_BQ, n_q=HB, n_k=KVB, seg=DB,
                   shift=DB // 4, q_scale=DB ** -0.5, split=False, name=f"prep_b_{tag}")
    y_b = flash_attention(qk_b, proj, batch, seq, kv_heads=KVB, groups=HB // KVB, k_slot0=HB, k_stride=1,
                          v_off=OFF_BV, name=f"attn_b_{tag}")

    qk_c = qk_prep(proj, tabs_c, w['qn_c2'], w['kn_c2'], batch, seq, col_off=OFF_CQ, n_q=HC, n_k=HC, seg=DC,
                   shift=ROT_C // 2, q_scale=DC ** -0.5, split=True, name=f"prep_c_{tag}")
    y_c = flash_attention(qk_c, proj, batch, seq, kv_heads=HC, groups=2, k_slot0=2 * HC, k_stride=2,
                          v_off=OFF_CV, diff=True, lambdas=w['lambdas_c'], subln=w['subln_c'],
                          lambda_init=lambda_init, name=f"attn_c_{tag}")

    mixed = merge_branches(y_a, y_b, y_c, proj, w['w_o_a'], w['w_o_b'], w['w_o_c'], f"merge_{tag}")
    x = res_matmul(mixed, w['w_out'], x, f"out_proj_{tag}")

    u = norm_matmul(x, w['norm_ffn'], w['w_up'], f"ffn_up_{tag}")
    act = ffn_act(u, w['conv_ffn'], w['conv_ffn_b'], batch, seq, f"ffn_act_{tag}")
    x = res_matmul(act, w['w_down'], x, f"ffn_down_{tag}")
    return ple_update(x, p_emb, w['w_ple_gate'], w['w_ple'], f"ple_{tag}")


def kernel(x_prompt, x_sample, p_prompt, p_sample, norm_mix, w_in, conv_a, a_log, dt_bias, norm_a, qn_b, kn_b, qn_c, kn_c, lambdas_c, subln_c, w_o_a, w_o_b, w_o_c, w_out, norm_ffn, w_up, conv_ffn, conv_ffn_b, w_down, w_ple, w_ple_gate):
    depth = w_in.shape[0]
    layers = []
    for i in range(depth):
        layers.append(dict(
            norm_mix=norm_mix[i], w_in=_permute_w_in(w_in[i]).astype(BF16), conv_a=conv_a[i],
            alog_row=_gdn_param_row(a_log[i]), dtb_row=_gdn_param_row(dt_bias[i]), norm_a=norm_a[i],
            qn_b=qn_b[i], kn_b=kn_b[i], qn_c2=jnp.tile(qn_c[i], 2), kn_c2=jnp.tile(kn_c[i], 2),
            lambdas_c=lambdas_c[i], subln_c=subln_c[i],
            w_o_a=w_o_a[i].astype(BF16), w_o_b=w_o_b[i].astype(BF16), w_o_c=w_o_c[i].astype(BF16),
            w_out=w_out[i].astype(BF16), norm_ffn=norm_ffn[i], w_up=w_up[i].astype(BF16),
            conv_ffn=conv_ffn[i], conv_ffn_b=conv_ffn_b[i], w_down=w_down[i].astype(BF16),
            w_ple=w_ple[i].astype(BF16), w_ple_gate=w_ple_gate[i].astype(BF16)))

    outs = []
    for tag, x, p in (("p", x_prompt, p_prompt), ("s", x_sample, p_sample)):
        batch, seq, d = x.shape
        tabs_b = _axial_tables(seq)
        tabs_c = _partial_tables(seq)
        h = x.reshape(batch * seq, d)
        for i in range(depth):
            h = _encoder_layer(h, p[i].reshape(batch * seq, -1), layers[i], i, batch, seq, tabs_b, tabs_c, f"{tag}{i}")
        outs.append(h.reshape(batch, seq, d))
    return tuple(outs)
```

```python
import functools
import math

import jax
import jax.numpy as jnp
from jax import lax
from jax.experimental import pallas as pl
from jax.experimental.pallas import tpu as pltpu

F32 = jnp.float32
BF16 = jnp.bfloat16
HIGHEST = lax.Precision.HIGHEST

D_MODEL = 2048
DEPTH = 2
GRID_W = 64
PLE_DIM = 256
NORM_EPS = 1e-6
HA = 8
DA = 128
A_WIDTH = HA * DA
DELTA_CHUNK = 64
HB = 8
KVB = 2
DB = 128
AXIAL_THETA = 10000.0
HC = 8
DC = 64
DVC = 2 * DC
ROT_C = DC // 4
ROPE_THETA = 500000.0
D_FF = 5632

LANES = 128
SUBLANES = 8
VMEM_LIMIT_BYTES = 56 * 1024 * 1024

OFF_QKV = 0
OFF_Z = 3072
OFF_AB = 4096
AB_WIDTH = 512
OFF_BQ = 4608
OFF_BV = 5888
OFF_CQ = 6144
OFF_CV = 8192
OFF_GATE = 9216
N_PROJ = 15360
GDN_HEAD_GROUP = 4
GDN_BLOCK = 256
ATTN_KV_CHUNK = 512
LOG2_E = math.log2(math.e)


def _cparams(semantics):
    return pltpu.CompilerParams(dimension_semantics=semantics, vmem_limit_bytes=VMEM_LIMIT_BYTES)


def _tile(n, pref):
    t = min(n, pref)
    while n % t:
        t //= 2
    return t


def _sigmoid(x):
    return 1.0 / (1.0 + jnp.exp(-x))


def _softplus(x):
    return jnp.maximum(x, 0.0) + jnp.log(1.0 + jnp.exp(-jnp.abs(x)))


def _norm_matmul_kernel(x_ref, g_ref, w_ref, o_ref, h_ref, *, rows):
    @pl.when(pl.program_id(1) == 0)
    def _():
        def chunk(r, c):
            rs = pl.ds(pl.multiple_of(r * rows, rows), rows)
            x = x_ref[rs, :]
            ms = jnp.mean(x * x, axis=-1, keepdims=True)
            h_ref[rs, :] = (x * lax.rsqrt(ms + NORM_EPS) * g_ref[...]).astype(BF16)
            return c
        lax.fori_loop(0, x_ref.shape[0] // rows, chunk, 0)

    o_ref[...] = jnp.dot(h_ref[...], w_ref[...], preferred_element_type=F32).astype(o_ref.dtype)


def norm_matmul(x, g, w, name):
    m, k = x.shape
    n = w.shape[1]
    tm = _tile(m, 1024)
    tn = 768 if n % 768 == 0 else _tile(n, 1024)
    rows = _tile(tm, 128)
    return pl.pallas_call(
        functools.partial(_norm_matmul_kernel, rows=rows),
        grid=(m // tm, n // tn),
        in_specs=[
            pl.BlockSpec((tm, k), lambda i, j: (i, 0)),
            pl.BlockSpec((1, k), lambda i, j: (0, 0)),
            pl.BlockSpec((k, tn), lambda i, j: (0, j)),
        ],
        out_specs=pl.BlockSpec((tm, tn), lambda i, j: (i, j)),
        out_shape=jax.ShapeDtypeStruct((m, n), BF16),
        scratch_shapes=[pltpu.VMEM((tm, k), BF16)],
        compiler_params=_cparams(("parallel", "arbitrary")),
        name=name,
    )(x, g.reshape(1, k), w)


def _res_matmul_kernel(a_ref, w_ref, r_ref, o_ref):
    o_ref[...] = r_ref[...] + jnp.dot(a_ref[...], w_ref[...], preferred_element_type=F32)


def res_matmul(a, w, res, name):
    m, k = a.shape
    n = w.shape[1]
    tm = _tile(m, 1024)
    tn = _tile(n, 512)
    return pl.pallas_call(
        _res_matmul_kernel,
        grid=(m // tm, n // tn),
        in_specs=[
            pl.BlockSpec((tm, k), lambda i, j: (i, 0)),
            pl.BlockSpec((k, tn), lambda i, j: (0, j)),
            pl.BlockSpec((tm, tn), lambda i, j: (i, j)),
        ],
        out_specs=pl.BlockSpec((tm, tn), lambda i, j: (i, j)),
        out_shape=jax.ShapeDtypeStruct((m, n), F32),
        compiler_params=_cparams(("parallel", "arbitrary")),
        name=name,
    )(a, w, res)


def _merge_kernel(ya_ref, yb_ref, yc_ref, ga_ref, gb_ref, gc_ref, wa_ref, wb_ref, wc_ref, o_ref):
    def branch(y_ref, g_ref, w_ref):
        return _sigmoid(g_ref[...].astype(F32)) * jnp.dot(y_ref[...], w_ref[...], preferred_element_type=F32)

    o_ref[...] = (branch(ya_ref, ga_ref, wa_ref) + branch(yb_ref, gb_ref, wb_ref)
                  + branch(yc_ref, gc_ref, wc_ref)).astype(o_ref.dtype)


def merge_branches(ya, yb, yc, proj, wa, wb, wc, name):
    m, k = ya.shape
    n = wa.shape[1]
    tm = _tile(m, 512)
    tn = _tile(n, 1024)
    nj = n // tn
    gate_blk = OFF_GATE // tn

    def y_spec():
        return pl.BlockSpec((tm, k), lambda j, i: (i, 0))

    def g_spec(which):
        return pl.BlockSpec((tm, tn), lambda j, i, which=which: (i, gate_blk + which * nj + j))

    def w_spec():
        return pl.BlockSpec((k, tn), lambda j, i: (0, j))

    return pl.pallas_call(
        _merge_kernel,
        grid=(nj, m // tm),
        in_specs=[y_spec(), y_spec(), y_spec(), g_spec(0), g_spec(1), g_spec(2), w_spec(), w_spec(), w_spec()],
        out_specs=pl.BlockSpec((tm, tn), lambda j, i: (i, j)),
        out_shape=jax.ShapeDtypeStruct((m, n), BF16),
        compiler_params=_cparams(("parallel", "arbitrary")),
        name=name,
    )(ya, yb, yc, proj, proj, proj, wa, wb, wc)


def _ple_kernel(x_ref, xres_ref, p_ref, wg_ref, we_ref, o_ref, xb_ref, pb_ref):
    @pl.when(pl.program_id(1) == 0)
    def _():
        xb_ref[...] = x_ref[...].astype(BF16)
        pb_ref[...] = p_ref[...].astype(BF16)

    gate = _sigmoid(jnp.dot(xb_ref[...], wg_ref[...], preferred_element_type=F32))
    emb = jnp.dot(pb_ref[...], we_ref[...], preferred_element_type=F32)
    o_ref[...] = xres_ref[...] + gate * emb


def ple_update(x, p, wg, we, name):
    m, k = x.shape
    n = wg.shape[1]
    kp = p.shape[1]
    tm = _tile(m, 1024)
    tn = _tile(n, 512)
    return pl.pallas_call(
        _ple_kernel,
        grid=(m // tm, n // tn),
        in_specs=[
            pl.BlockSpec((tm, k), lambda i, j: (i, 0)),
            pl.BlockSpec((tm, tn), lambda i, j: (i, j)),
            pl.BlockSpec((tm, kp), lambda i, j: (i, 0)),
            pl.BlockSpec((k, tn), lambda i, j: (0, j)),
            pl.BlockSpec((kp, tn), lambda i, j: (0, j)),
        ],
        out_specs=pl.BlockSpec((tm, tn), lambda i, j: (i, j)),
        out_shape=jax.ShapeDtypeStruct((m, n), F32),
        scratch_shapes=[pltpu.VMEM((tm, k), BF16), pltpu.VMEM((tm, kp), BF16)],
        compiler_params=_cparams(("parallel", "arbitrary")),
        name=name,
    )(x, x, p, wg, we)


def _conv3(x, prev_row, next_row, w_ref, lanes):
    ts = x.shape[0]
    r = lax.broadcasted_iota(jnp.int32, (ts, 1), 0)
    xp = jnp.where(r == 0, prev_row, pltpu.roll(x, 1, 0))
    xn = jnp.where(r == ts - 1, next_row, pltpu.roll(x, ts - 1, 0))
    return xp * w_ref[0:1, lanes] + x * w_ref[1:2, lanes] + xn * w_ref[2:3, lanes]


def _halo_specs(batch, seq, ts, width, col_fn):
    n_s = seq // ts
    per_seq = seq // SUBLANES
    per_tile = ts // SUBLANES
    last = batch * per_seq - 1

    def prev_map(b, i, *rest):
        return (jnp.maximum(b * per_seq + i * per_tile - 1, 0), col_fn(*rest))

    def next_map(b, i, *rest):
        return (jnp.minimum(b * per_seq + (i + 1) * per_tile, last), col_fn(*rest))

    return pl.BlockSpec((SUBLANES, width), prev_map), pl.BlockSpec((SUBLANES, width), next_map), n_s


def _ffn_act_kernel(ug_ref, ugp_ref, ugn_ref, uv_ref, uvp_ref, uvn_ref, wg_ref, wv_ref, bg_ref, bv_ref, o_ref, *, n_s):
    i = pl.program_id(1)
    has_prev = (i > 0).astype(F32)
    has_next = (i < n_s - 1).astype(F32)
    all_lanes = slice(None)

    def conv(u_ref, p_ref, n_ref, w_ref, b_ref):
        x = u_ref[...].astype(F32)
        prev_row = p_ref[SUBLANES - 1:SUBLANES, :].astype(F32) * has_prev
        next_row = n_ref[0:1, :].astype(F32) * has_next
        return _conv3(x, prev_row, next_row, w_ref, all_lanes) + b_ref[...]

    gate = conv(ug_ref, ugp_ref, ugn_ref, wg_ref, bg_ref)
    val = conv(uv_ref, uvp_ref, uvn_ref, wv_ref, bv_ref)
    o_ref[...] = (gate * _sigmoid(gate) * val).astype(o_ref.dtype)


def ffn_act(u, conv_w, conv_b, batch, seq, name):
    t, two_f = u.shape
    f = two_f // 2
    ts = _tile(seq, 256)
    tc = 512
    nj = f // tc
    prev_g, next_g, n_s = _halo_specs(batch, seq, ts, tc, lambda j: j)
    prev_v, next_v, _ = _halo_specs(batch, seq, ts, tc, lambda j: nj + j)
    main_g = pl.BlockSpec((ts, tc), lambda b, i, j: (b * n_s + i, j))
    main_v = pl.BlockSpec((ts, tc), lambda b, i, j: (b * n_s + i, nj + j))
    w_g = pl.BlockSpec((3, tc), lambda b, i, j: (0, j))
    w_v = pl.BlockSpec((3, tc), lambda b, i, j: (0, nj + j))
    b_g = pl.BlockSpec((1, tc), lambda b, i, j: (0, j))
    b_v = pl.BlockSpec((1, tc), lambda b, i, j: (0, nj + j))
    cb = conv_b.reshape(1, two_f)
    return pl.pallas_call(
        functools.partial(_ffn_act_kernel, n_s=n_s),
        grid=(batch, n_s, nj),
        in_specs=[main_g, prev_g, next_g, main_v, prev_v, next_v, w_g, w_v, b_g, b_v],
        out_specs=pl.BlockSpec((ts, tc), lambda b, i, j: (b * n_s + i, j)),
        out_shape=jax.ShapeDtypeStruct((t, f), BF16),
        compiler_params=_cparams(("parallel", "parallel", "arbitrary")),
        name=name,
    )(u, u, u, u, u, u, conv_w, conv_w, cb, cb)


def _gdn_prep_kernel(x_ref, xp_ref, xn_ref, ab_ref, w_ref, alog_ref, dtb_ref, q_ref, k_ref, v_ref, gb_ref, *, n_s):
    i = pl.program_id(1)
    has_prev = (i > 0).astype(F32)
    has_next = (i < n_s - 1).astype(F32)
    for c in range(3 * HA):
        lanes = slice(c * DA, (c + 1) * DA)
        x = x_ref[:, lanes].astype(F32)
        prev_row = xp_ref[SUBLANES - 1:SUBLANES, lanes].astype(F32) * has_prev
        next_row = xn_ref[0:1, lanes].astype(F32) * has_next
        y = _conv3(x, prev_row, next_row, w_ref, lanes)
        y = y * _sigmoid(y)
        out_lanes = slice((c % HA) * DA, (c % HA + 1) * DA)
        if c < 2 * HA:
            y = y * lax.rsqrt(jnp.sum(y * y, axis=-1, keepdims=True) + NORM_EPS)
            (q_ref if c < HA else k_ref)[:, out_lanes] = y.astype(BF16)
        else:
            v_ref[:, out_lanes] = y.astype(BF16)
    ab = ab_ref[...].astype(F32)
    lane = lax.broadcasted_iota(jnp.int32, (1, AB_WIDTH), 1) % LANES
    g = -jnp.exp(alog_ref[...]) * _softplus(ab + dtb_ref[...])
    gb_ref[...] = jnp.where(lane < GDN_HEAD_GROUP, g, _sigmoid(ab))


def gdn_prep(proj, conv_w, alog_row, dtb_row, batch, seq, name):
    t = proj.shape[0]
    ts = _tile(seq, 256)
    w3 = 3 * A_WIDTH
    prev_s, next_s, n_s = _halo_specs(batch, seq, ts, w3, lambda: 0)
    row = lambda b, i: (b * n_s + i, 0)
    out_spec = pl.BlockSpec((ts, A_WIDTH), row)
    return pl.pallas_call(
        functools.partial(_gdn_prep_kernel, n_s=n_s),
        grid=(batch, n_s),
        in_specs=[
            pl.BlockSpec((ts, w3), row), prev_s, next_s,
            pl.BlockSpec((ts, AB_WIDTH), lambda b, i: (b * n_s + i, OFF_AB // AB_WIDTH)),
            pl.BlockSpec((3, w3), lambda b, i: (0, 0)),
            pl.BlockSpec((1, AB_WIDTH), lambda b, i: (0, 0)),
            pl.BlockSpec((1, AB_WIDTH), lambda b, i: (0, 0)),
        ],
        out_specs=[out_spec, out_spec, out_spec, pl.BlockSpec((ts, AB_WIDTH), row)],
        out_shape=[jax.ShapeDtypeStruct((t, A_WIDTH), BF16)] * 3 + [jax.ShapeDtypeStruct((t, AB_WIDTH), F32)],
        compiler_params=_cparams(("parallel", "parallel")),
        name=name,
    )(proj, proj, proj, proj, conv_w, alog_row, dtb_row)


def _gdn_scan_kernel(q_ref, k_ref, v_ref, gb_ref, o_ref, s_ref, *, n_chunks):
    c_len = DELTA_CHUNK
    d = pl.program_id(1)

    @pl.when(pl.program_id(3) == 0)
    def _():
        s_ref[...] = jnp.zeros_like(s_ref)

    sgn = 1 - 2 * d
    row = lax.broadcasted_iota(jnp.int32, (c_len, c_len), 0)
    col = lax.broadcasted_iota(jnp.int32, (c_len, c_len), 1)
    order = (row - col) * sgn
    incl = order >= 0
    strict = order > 0
    cum_mat = incl.astype(F32)
    eye = (row == col).astype(F32)
    sel = (lax.broadcasted_iota(jnp.int32, (SUBLANES, LANES), 0)
           == lax.broadcasted_iota(jnp.int32, (SUBLANES, LANES), 1)).astype(F32)
    scale = DA ** -0.5
    nt = (((1,), (1,)), ((), ()))
    tn = (((0,), (0,)), ((), ()))

    for j in range(n_chunks):
        c = j + d * (n_chunks - 1 - 2 * j)
        rows = pl.ds(pl.multiple_of(c * c_len, c_len), c_len)
        gblk = gb_ref[rows, :]
        gcol = jnp.dot(cum_mat, gblk, precision=HIGHEST, preferred_element_type=F32)
        grow = lax.dot_general(sel, gcol, nt, precision=HIGHEST, preferred_element_type=F32)
        gtot = jnp.sum(gblk, axis=0, keepdims=True)
        exp_g = jnp.exp(gcol)
        exp_rest = jnp.exp(gtot - gcol)
        exp_tot = jnp.exp(gtot)
        for h in range(GDN_HEAD_GROUP):
            lanes = slice(h * DA, (h + 1) * DA)
            qf = q_ref[rows, lanes].astype(F32)
            kf = k_ref[rows, lanes].astype(F32)
            vf = v_ref[rows, lanes].astype(F32)
            beta = gblk[:, GDN_HEAD_GROUP + h:GDN_HEAD_GROUP + h + 1]
            eg = exp_g[:, h:h + 1]
            diff = gcol[:, h:h + 1] - grow[h:h + 1, :]
            decay = jnp.where(incl, jnp.exp(jnp.where(incl, diff, 0.0)), 0.0)
            kb = kf * beta
            kbf = k_ref[rows, lanes]
            kk = lax.dot_general(kb.astype(BF16), kbf, nt, preferred_element_type=F32)
            low = jnp.where(strict, kk * decay, 0.0)
            qk = lax.dot_general((qf * scale).astype(BF16), kbf, nt, preferred_element_type=F32)
            intra = jnp.where(incl, qk * decay, 0.0)
            inv = eye - low
            power = low
            for _ in range(5):
                pb = power.astype(BF16)
                power = jnp.dot(pb, pb, preferred_element_type=F32)
                inv = inv + jnp.dot(inv.astype(BF16), power.astype(BF16), preferred_element_type=F32)
            rhs = jnp.concatenate([vf * beta, kb * eg], axis=1).astype(BF16)
            uw = jnp.dot(inv.astype(BF16), rhs, preferred_element_type=F32)
            u = uw[:, :DA]
            w = uw[:, DA:]
            state = s_ref[h]
            sb = state.astype(BF16)
            v_new = u - jnp.dot(w.astype(BF16), sb, preferred_element_type=F32)
            vnb = v_new.astype(BF16)
            q_dec = (qf * scale * eg).astype(BF16)
            out = (jnp.dot(q_dec, sb, preferred_element_type=F32)
                   + jnp.dot(intra.astype(BF16), vnb, preferred_element_type=F32))
            k_dec = (kf * exp_rest[:, h:h + 1]).astype(BF16)
            s_ref[h] = state * exp_tot[:, h:h + 1] + lax.dot_general(k_dec, vnb, tn, preferred_element_type=F32)
            o_ref[0, rows, lanes] = out.astype(o_ref.dtype)


def gdn_scan(q, k, v, gb, batch, seq, name):
    t = q.shape[0]
    blk = GDN_BLOCK
    n_blk = seq // blk
    width = GDN_HEAD_GROUP * DA
    n_hg = HA // GDN_HEAD_GROUP

    def rblk(b, d, i):
        return b * n_blk + i + d * (n_blk - 1 - 2 * i)

    qkv_spec = pl.BlockSpec((blk, width), lambda b, d, g, i: (rblk(b, d, i), g))
    return pl.pallas_call(
        functools.partial(_gdn_scan_kernel, n_chunks=blk // DELTA_CHUNK),
        grid=(batch, 2, n_hg, n_blk),
        in_specs=[qkv_spec, qkv_spec, qkv_spec,
                  pl.BlockSpec((blk, LANES), lambda b, d, g, i: (rblk(b, d, i), d * n_hg + g))],
        out_specs=pl.BlockSpec((1, blk, width), lambda b, d, g, i: (d, rblk(b, d, i), g)),
        out_shape=jax.ShapeDtypeStruct((2, t, A_WIDTH), BF16),
        scratch_shapes=[pltpu.VMEM((GDN_HEAD_GROUP, DA, DA), F32)],
        compiler_params=_cparams(("parallel", "parallel", "parallel", "arbitrary")),
        name=name,
    )(q, k, v, gb)


def _gdn_post_kernel(o_ref, z_ref, g_ref, y_ref):
    for h in range(HA):
        lanes = slice(h * DA, (h + 1) * DA)
        o = o_ref[0, :, lanes].astype(F32) + o_ref[1, :, lanes].astype(F32)
        z = z_ref[:, lanes].astype(F32)
        ms = jnp.mean(o * o, axis=-1, keepdims=True)
        y = o * lax.rsqrt(ms + NORM_EPS) * g_ref[...]
        y_ref[:, lanes] = (y * (z * _sigmoid(z))).astype(y_ref.dtype)


def gdn_post(o2, proj, norm_w, name):
    t = proj.shape[0]
    ts = _tile(t, 512)
    return pl.pallas_call(
        _gdn_post_kernel,
        grid=(t // ts,),
        in_specs=[
            pl.BlockSpec((2, ts, A_WIDTH), lambda i: (0, i, 0)),
            pl.BlockSpec((ts, A_WIDTH), lambda i: (i, OFF_Z // A_WIDTH)),
            pl.BlockSpec((1, DA), lambda i: (0, 0)),
        ],
        out_specs=pl.BlockSpec((ts, A_WIDTH), lambda i: (i, 0)),
        out_shape=jax.ShapeDtypeStruct((t, A_WIDTH), BF16),
        compiler_params=_cparams(("parallel",)),
        name=name,
    )(o2, proj, norm_w.reshape(1, DA))


def _qk_prep_kernel(x_ref, cos_ref, sa_ref, sb_ref, gq_ref, gk_ref, o_ref, *, n_q, seg, shift, q_scale, split):
    is_q = pl.program_id(1) < n_q
    x = x_ref[...].astype(F32)
    sq = x * x
    if seg == LANES:
        ms = jnp.mean(sq, axis=-1, keepdims=True)
    else:
        grp_r = lax.broadcasted_iota(jnp.int32, (LANES, LANES), 0) // seg
        grp_c = lax.broadcasted_iota(jnp.int32, (LANES, LANES), 1) // seg
        ms = jnp.dot(sq, (grp_r == grp_c).astype(F32), precision=HIGHEST, preferred_element_type=F32) * (1.0 / seg)
    gain = jnp.where(is_q, gq_ref[...], gk_ref[...])
    y = x * lax.rsqrt(ms + NORM_EPS) * gain
    y = y * cos_ref[...] + pltpu.roll(y, LANES - shift, 1) * sa_ref[...] + pltpu.roll(y, shift, 1) * sb_ref[...]
    y = y * jnp.where(is_q, q_scale, 1.0)
    if not split:
        o_ref[0, 0] = y.astype(o_ref.dtype)
    else:
        lane = lax.broadcasted_iota(jnp.int32, (1, LANES), 1)
        o_ref[0, 0] = jnp.where(jnp.logical_or(lane < seg, jnp.logical_not(is_q)), y, 0.0).astype(o_ref.dtype)
        o_ref[0, 1] = jnp.where(jnp.logical_and(lane >= seg, is_q), y, 0.0).astype(o_ref.dtype)


def qk_prep(proj, tables, gq, gk, batch, seq, *, col_off, n_q, n_k, seg, shift, q_scale, split, name):
    ts = _tile(seq, 512)
    n_s = seq // ts
    per = 2 if split else 1
    col0 = col_off // LANES
    tab = pl.BlockSpec((ts, LANES), lambda b, h, i: (i, 0))
    gain = pl.BlockSpec((1, LANES), lambda b, h, i: (0, 0))
    return pl.pallas_call(
        functools.partial(_qk_prep_kernel, n_q=n_q, seg=seg, shift=shift, q_scale=q_scale, split=split),
        grid=(batch, n_q + n_k, n_s),
        in_specs=[pl.BlockSpec((ts, LANES), lambda b, h, i: (b * n_s + i, col0 + h)), tab, tab, tab, gain, gain],
        out_specs=pl.BlockSpec((1, per, ts, LANES), lambda b, h, i: (b, h, i, 0)),
        out_shape=jax.ShapeDtypeStruct((batch, per * (n_q + n_k), seq, LANES), BF16),
        compiler_params=_cparams(("parallel", "parallel", "parallel")),
        name=name,
    )(proj, tables[0], tables[1], tables[2], gq.reshape(1, LANES), gk.reshape(1, LANES))


def _v_transpose_kernel(x_ref, o_ref):
    o_ref[0, 0, 0] = x_ref[...].astype(F32).T.astype(o_ref.dtype)


def v_transpose(proj, batch, seq, *, col_off, heads, tk, name):
    n_c = seq // tk
    col0 = col_off // LANES
    return pl.pallas_call(
        _v_transpose_kernel,
        grid=(batch, heads, n_c),
        in_specs=[pl.BlockSpec((tk, LANES), lambda b, h, i: (b * n_c + i, col0 + h))],
        out_specs=pl.BlockSpec((1, 1, 1, LANES, tk), lambda b, h, i: (b, h, i, 0, 0)),
        out_shape=jax.ShapeDtypeStruct((batch, heads, n_c, LANES, tk), BF16),
        compiler_params=_cparams(("parallel", "parallel", "parallel")),
        name=name,
    )(proj)


def _flash_kernel(*refs, groups, tq, tk, seq, diff, lambda_init):
    if diff:
        q_ref, k_ref, vt_ref, lam_ref, sub_ref, o_ref, qt_ref, sa_ref, sb_ref, ma_ref, mb_ref, m_ref, l_ref, acc_ref = refs
    else:
        q_ref, k_ref, vt_ref, o_ref, qt_ref, sa_ref, sb_ref, ma_ref, mb_ref, m_ref, l_ref, acc_ref = refs
    rows = groups * tq
    n_c = seq // tk
    qt_ref[...] = q_ref[0].reshape(rows, LANES).astype(F32).T.astype(BF16)
    m_ref[...] = jnp.full(m_ref.shape, -jnp.inf, F32)
    l_ref[...] = jnp.zeros(l_ref.shape, F32)
    acc_ref[...] = jnp.zeros(acc_ref.shape, F32)
    buf_a = (sa_ref, ma_ref)
    buf_b = (sb_ref, mb_ref)

    def scores_into(c, buf):
        ks = pl.ds(pl.multiple_of(c * tk, tk), tk)
        s_t = jnp.dot(k_ref[0, 0, ks, :], qt_ref[...], preferred_element_type=F32)
        buf[0][...] = s_t
        buf[1][...] = jnp.max(s_t, axis=0, keepdims=True)

    def step(c, cur, nxt):
        if nxt is not None:
            scores_into(c + 1, nxt)
        m_prev = m_ref[...]
        m_new = jnp.maximum(m_prev, cur[1][...])
        p_t = jnp.exp2(cur[0][...] - m_new)
        alpha = jnp.exp2(m_prev - m_new)
        l_ref[...] = alpha * l_ref[...] + jnp.sum(p_t, axis=0, keepdims=True)
        acc_ref[...] = alpha * acc_ref[...] + jnp.dot(vt_ref[0, 0, c], p_t.astype(BF16),
                                                      preferred_element_type=F32)
        m_ref[...] = m_new

    scores_into(0, buf_a)
    n_pairs = (n_c - 1) // 2

    def body(i, carry):
        step(2 * i, buf_a, buf_b)
        step(2 * i + 1, buf_b, buf_a)
        return carry

    lax.fori_loop(0, n_pairs, body, 0)
    if (n_c - 1) % 2:
        step(n_c - 2, buf_a, buf_b)
        step(n_c - 1, buf_b, None)
    else:
        step(n_c - 1, buf_a, None)
    o_t = acc_ref[...] / l_ref[...]
    if not diff:
        o = o_t.T
        for g in range(groups):
            o_ref[:, g * LANES:(g + 1) * LANES] = o[g * tq:(g + 1) * tq].astype(o_ref.dtype)
    else:
        lf = lam_ref[...]
        lam = (jnp.exp(jnp.sum(lf[0:1] * lf[1:2], axis=-1, keepdims=True))
               - jnp.exp(jnp.sum(lf[2:3] * lf[3:4], axis=-1, keepdims=True)) + lambda_init)
        d_t = o_t[:, :tq] - lam * o_t[:, tq:]
        ms = jnp.mean(d_t * d_t, axis=0, keepdims=True)
        y = (d_t * lax.rsqrt(ms + NORM_EPS)).T * sub_ref[...] * (1.0 - lambda_init)
        o_ref[...] = y.astype(o_ref.dtype)


def flash_attention(qk, vt, batch, seq, *, kv_heads, groups, k_slot0, k_stride, diff=False,
                    lambdas=None, subln=None, lambda_init=0.0, name):
    tk = vt.shape[-1]
    n_c = seq // tk
    tq = _tile(seq, 1024 // groups)
    n_q = seq // tq
    rows = groups * tq
    out_w = (1 if diff else groups) * LANES
    in_specs = [
        pl.BlockSpec((1, groups, tq, LANES), lambda b, h, i: (b, h, i, 0)),
        pl.BlockSpec((1, 1, seq, LANES), lambda b, h, i: (b, k_slot0 + k_stride * h, 0, 0)),
        pl.BlockSpec((1, 1, n_c, LANES, tk), lambda b, h, i: (b, h, 0, 0, 0)),
    ]
    args = [qk, qk, vt]
    if diff:
        in_specs += [pl.BlockSpec(lambdas.shape, lambda b, h, i: (0, 0)),
                     pl.BlockSpec((1, LANES), lambda b, h, i: (0, 0))]
        args += [lambdas, subln.reshape(1, LANES)]
    return pl.pallas_call(
        functools.partial(_flash_kernel, groups=groups, tq=tq, tk=tk, seq=seq, diff=diff, lambda_init=lambda_init),
        grid=(batch, kv_heads, n_q),
        in_specs=in_specs,
        out_specs=pl.BlockSpec((tq, out_w), lambda b, h, i: (b * n_q + i, h)),
        out_shape=jax.ShapeDtypeStruct((batch * seq, kv_heads * out_w), BF16),
        scratch_shapes=[pltpu.VMEM((LANES, rows), BF16), pltpu.VMEM((tk, rows), F32), pltpu.VMEM((tk, rows), F32),
                        pltpu.VMEM((1, rows), F32), pltpu.VMEM((1, rows), F32),
                        pltpu.VMEM((1, rows), F32), pltpu.VMEM((1, rows), F32), pltpu.VMEM((LANES, rows), F32)],
        compiler_params=_cparams(("parallel", "parallel", "arbitrary")),
        name=name,
    )(*args)


def _rope_angles(pos, dim, theta):
    inv = theta ** (-jnp.arange(0, dim, 2, dtype=F32) / dim)
    return pos[:, None] * inv[None, :]


def _axial_tables(seq):
    t = jnp.arange(seq)
    row = (t // GRID_W).astype(F32)
    col = (t % GRID_W).astype(F32)
    half = DB // 2
    ang_r = _rope_angles(row, half, AXIAL_THETA)
    ang_c = _rope_angles(col, half, AXIAL_THETA)
    ang = jnp.concatenate([ang_r, ang_r, ang_c, ang_c], axis=-1)
    cos, sin = jnp.cos(ang), jnp.sin(ang)
    first = (jnp.arange(LANES) % half) < (half // 2)
    return cos, jnp.where(first, -sin, 0.0), jnp.where(first, 0.0, sin)


def _partial_tables(seq):
    pos = jnp.arange(seq, dtype=F32)
    ang8 = _rope_angles(pos, ROT_C, ROPE_THETA)
    lane = jnp.arange(LANES) % DC
    ang = jnp.take(ang8, lane % (ROT_C // 2), axis=1)
    rot = lane < ROT_C
    first = lane < ROT_C // 2
    cos = jnp.where(rot, jnp.cos(ang), 1.0)
    sin = jnp.where(rot, jnp.sin(ang), 0.0)
    return cos, jnp.where(first, -sin, 0.0), jnp.where(first, 0.0, sin)


def _permute_w_in(w):
    k = w.shape[0]
    alpha0 = 4 * A_WIDTH
    beta0 = alpha0 + 2 * HA
    blocks = []
    for d in range(2):
        for g in range(HA // GDN_HEAD_GROUP):
            a0 = alpha0 + d * HA + g * GDN_HEAD_GROUP
            b0 = beta0 + d * HA + g * GDN_HEAD_GROUP
            blocks += [w[:, a0:a0 + GDN_HEAD_GROUP], w[:, b0:b0 + GDN_HEAD_GROUP],
                       jnp.zeros((k, LANES - 2 * GDN_HEAD_GROUP), w.dtype)]
    return jnp.concatenate([w[:, :alpha0]] + blocks + [w[:, beta0 + 2 * HA:]], axis=1)


def _gdn_param_row(p):
    blocks = []
    for d in range(2):
        for g in range(HA // GDN_HEAD_GROUP):
            blocks += [p[d, g * GDN_HEAD_GROUP:(g + 1) * GDN_HEAD_GROUP], jnp.zeros((LANES - GDN_HEAD_GROUP,), p.dtype)]
    return jnp.concatenate(blocks).reshape(1, AB_WIDTH)


def _encoder_layer(x, p_emb, w, li, batch, seq, tabs_b, tabs_c, tag):
    lambda_init = 0.8 - 0.6 * math.exp(-0.3 * li)
    proj = norm_matmul(x, w['norm_mix'], w['w_in'], f"in_proj_{tag}")

    q_a, k_a, v_a, gb = gdn_prep(proj, w['conv_a'], w['alog_row'], w['dtb_row'], batch, seq, f"gdn_prep_{tag}")
    o2 = gdn_scan(q_a, k_a, v_a, gb, batch, seq, f"gdn_scan_{tag}")
    y_a = gdn_post(o2, proj, w['norm_a'], f"gdn_post_{tag}")

    qk_b = qk_prep(proj, tabs_b, w['qn_b'], w['kn_b'], batch, seq, col_off=OFF_BQ, n_q=HB, n_k=KVB, seg=DB,
                   shift=DB // 4, q_scale=DB ** -0.5 * LOG2_E, split=False, name=f"prep_b_{tag}")
    vt_b = v_transpose(proj, batch, seq, col_off=OFF_BV, heads=KVB, tk=_tile(seq, ATTN_KV_CHUNK), name=f"vt_b_{tag}")
    y_b = flash_attention(qk_b, vt_b, batch, seq, kv_heads=KVB, groups=HB // KVB, k_slot0=HB, k_stride=1,
                          name=f"attn_b_{tag}")

    qk_c = qk_prep(proj, tabs_c, w['qn_c2'], w['kn_c2'], batch, seq, col_off=OFF_CQ, n_q=HC, n_k=HC, seg=DC,
                   shift=ROT_C // 2, q_scale=DC ** -0.5 * LOG2_E, split=True, name=f"prep_c_{tag}")
    vt_c = v_transpose(proj, batch, seq, col_off=OFF_CV, heads=HC, tk=_tile(seq, ATTN_KV_CHUNK), name=f"vt_c_{tag}")
    y_c = flash_attention(qk_c, vt_c, batch, seq, kv_heads=HC, groups=2, k_slot0=2 * HC, k_stride=2,
                          diff=True, lambdas=w['lambdas_c'], subln=w['subln_c'],
                          lambda_init=lambda_init, name=f"attn_c_{tag}")

    mixed = merge_branches(y_a, y_b, y_c, proj, w['w_o_a'], w['w_o_b'], w['w_o_c'], f"merge_{tag}")
    x = res_matmul(mixed, w['w_out'], x, f"out_proj_{tag}")

    u = norm_matmul(x, w['norm_ffn'], w['w_up'], f"ffn_up_{tag}")
    act = ffn_act(u, w['conv_ffn'], w['conv_ffn_b'], batch, seq, f"ffn_act_{tag}")
    x = res_matmul(act, w['w_down'], x, f"ffn_down_{tag}")
    return ple_update(x, p_emb, w['w_ple_gate'], w['w_ple'], f"ple_{tag}")


def kernel(x_prompt, x_sample, p_prompt, p_sample, norm_mix, w_in, conv_a, a_log, dt_bias, norm_a, qn_b, kn_b, qn_c, kn_c, lambdas_c, subln_c, w_o_a, w_o_b, w_o_c, w_out, norm_ffn, w_up, conv_ffn, conv_ffn_b, w_down, w_ple, w_ple_gate):
    depth = w_in.shape[0]
    layers = []
    for i in range(depth):
        layers.append(dict(
            norm_mix=norm_mix[i], w_in=_permute_w_in(w_in[i]).astype(BF16), conv_a=conv_a[i],
            alog_row=_gdn_param_row(a_log[i]), dtb_row=_gdn_param_row(dt_bias[i]), norm_a=norm_a[i],
            qn_b=qn_b[i], kn_b=kn_b[i], qn_c2=jnp.tile(qn_c[i], 2), kn_c2=jnp.tile(kn_c[i], 2),
            lambdas_c=lambdas_c[i], subln_c=subln_c[i],
            w_o_a=w_o_a[i].astype(BF16), w_o_b=w_o_b[i].astype(BF16), w_o_c=w_o_c[i].astype(BF16),
            w_out=w_out[i].astype(BF16), norm_ffn=norm_ffn[i], w_up=w_up[i].astype(BF16),
            conv_ffn=conv_ffn[i], conv_ffn_b=conv_ffn_b[i], w_down=w_down[i].astype(BF16),
            w_ple=w_ple[i].astype(BF16), w_ple_gate=w_ple_gate[i].astype(BF16)))

    outs = []
    for tag, x, p in (("p", x_prompt, p_prompt), ("s", x_sample, p_sample)):
        batch, seq, d = x.shape
        tabs_b = _axial_tables(seq)
        tabs_c = _partial_tables(seq)
        h = x.reshape(batch * seq, d)
        for i in range(depth):
            h = _encoder_layer(h, p[i].reshape(batch * seq, -1), layers[i], i, batch, seq, tabs_b, tabs_c, f"{tag}{i}")
        outs.append(h.reshape(batch, seq, d))
    return tuple(outs)
```

```python
import functools
import math

import jax
import jax.numpy as jnp
from jax import lax
from jax.experimental import pallas as pl
from jax.experimental.pallas import tpu as pltpu

F32 = jnp.float32
BF16 = jnp.bfloat16
HIGHEST = lax.Precision.HIGHEST

D_MODEL = 2048
DEPTH = 2
GRID_W = 64
PLE_DIM = 256
NORM_EPS = 1e-6
HA = 8
DA = 128
A_WIDTH = HA * DA
DELTA_CHUNK = 64
HB = 8
KVB = 2
DB = 128
AXIAL_THETA = 10000.0
HC = 8
DC = 64
DVC = 2 * DC
ROT_C = DC // 4
ROPE_THETA = 500000.0
D_FF = 5632

LANES = 128
SUBLANES = 8
VMEM_LIMIT_BYTES = 56 * 1024 * 1024

OFF_QKV = 0
OFF_Z = 3072
OFF_AB = 4096
AB_WIDTH = 512
OFF_BQ = 4608
OFF_BV = 5888
OFF_CQ = 6144
OFF_CV = 8192
OFF_GATE = 9216
N_PROJ = 15360
GDN_HEAD_GROUP = 4
GDN_BLOCK = 256
ATTN_KV_CHUNK = 1024
LOG2_E = math.log2(math.e)


def _cparams(semantics):
    return pltpu.CompilerParams(dimension_semantics=semantics, vmem_limit_bytes=VMEM_LIMIT_BYTES)


def _tile(n, pref):
    t = min(n, pref)
    while n % t:
        t //= 2
    return t


def _sigmoid(x):
    return 1.0 / (1.0 + jnp.exp(-x))


def _softplus(x):
    return jnp.maximum(x, 0.0) + jnp.log(1.0 + jnp.exp(-jnp.abs(x)))


def _norm_matmul_kernel(x_ref, g_ref, w_ref, o_ref, h_ref, *, rows):
    @pl.when(pl.program_id(1) == 0)
    def _():
        def chunk(r, c):
            rs = pl.ds(pl.multiple_of(r * rows, rows), rows)
            x = x_ref[rs, :]
            ms = jnp.mean(x * x, axis=-1, keepdims=True)
            h_ref[rs, :] = (x * lax.rsqrt(ms + NORM_EPS) * g_ref[...]).astype(BF16)
            return c
        lax.fori_loop(0, x_ref.shape[0] // rows, chunk, 0)

    o_ref[...] = jnp.dot(h_ref[...], w_ref[...], preferred_element_type=F32).astype(o_ref.dtype)


def norm_matmul(x, g, w, name):
    m, k = x.shape
    n = w.shape[1]
    tm = _tile(m, 1024)
    tn = 768 if n % 768 == 0 else _tile(n, 1024)
    rows = _tile(tm, 128)
    return pl.pallas_call(
        functools.partial(_norm_matmul_kernel, rows=rows),
        grid=(m // tm, n // tn),
        in_specs=[
            pl.BlockSpec((tm, k), lambda i, j: (i, 0)),
            pl.BlockSpec((1, k), lambda i, j: (0, 0)),
            pl.BlockSpec((k, tn), lambda i, j: (0, j)),
        ],
        out_specs=pl.BlockSpec((tm, tn), lambda i, j: (i, j)),
        out_shape=jax.ShapeDtypeStruct((m, n), BF16),
        scratch_shapes=[pltpu.VMEM((tm, k), BF16)],
        compiler_params=_cparams(("parallel", "arbitrary")),
        name=name,
    )(x, g.reshape(1, k), w)


def _res_matmul_kernel(a_ref, w_ref, r_ref, o_ref):
    o_ref[...] = r_ref[...] + jnp.dot(a_ref[...], w_ref[...], preferred_element_type=F32)


def res_matmul(a, w, res, name):
    m, k = a.shape
    n = w.shape[1]
    tm = _tile(m, 1024)
    tn = _tile(n, 512)
    return pl.pallas_call(
        _res_matmul_kernel,
        grid=(m // tm, n // tn),
        in_specs=[
            pl.BlockSpec((tm, k), lambda i, j: (i, 0)),
            pl.BlockSpec((k, tn), lambda i, j: (0, j)),
            pl.BlockSpec((tm, tn), lambda i, j: (i, j)),
        ],
        out_specs=pl.BlockSpec((tm, tn), lambda i, j: (i, j)),
        out_shape=jax.ShapeDtypeStruct((m, n), F32),
        compiler_params=_cparams(("parallel", "arbitrary")),
        name=name,
    )(a, w, res)


def _merge_kernel(ya_ref, yb_ref, yc_ref, ga_ref, gb_ref, gc_ref, wa_ref, wb_ref, wc_ref, o_ref):
    def branch(y_ref, g_ref, w_ref):
        return _sigmoid(g_ref[...].astype(F32)) * jnp.dot(y_ref[...], w_ref[...], preferred_element_type=F32)

    o_ref[...] = (branch(ya_ref, ga_ref, wa_ref) + branch(yb_ref, gb_ref, wb_ref)
                  + branch(yc_ref, gc_ref, wc_ref)).astype(o_ref.dtype)


def merge_branches(ya, yb, yc, proj, wa, wb, wc, name):
    m, k = ya.shape
    n = wa.shape[1]
    tm = _tile(m, 512)
    tn = _tile(n, 1024)
    nj = n // tn
    gate_blk = OFF_GATE // tn

    def y_spec():
        return pl.BlockSpec((tm, k), lambda j, i: (i, 0))

    def g_spec(which):
        return pl.BlockSpec((tm, tn), lambda j, i, which=which: (i, gate_blk + which * nj + j))

    def w_spec():
        return pl.BlockSpec((k, tn), lambda j, i: (0, j))

    return pl.pallas_call(
        _merge_kernel,
        grid=(nj, m // tm),
        in_specs=[y_spec(), y_spec(), y_spec(), g_spec(0), g_spec(1), g_spec(2), w_spec(), w_spec(), w_spec()],
        out_specs=pl.BlockSpec((tm, tn), lambda j, i: (i, j)),
        out_shape=jax.ShapeDtypeStruct((m, n), BF16),
        compiler_params=_cparams(("parallel", "arbitrary")),
        name=name,
    )(ya, yb, yc, proj, proj, proj, wa, wb, wc)


def _ple_kernel(x_ref, xres_ref, p_ref, wg_ref, we_ref, o_ref, xb_ref, pb_ref):
    @pl.when(pl.program_id(1) == 0)
    def _():
        xb_ref[...] = x_ref[...].astype(BF16)
        pb_ref[...] = p_ref[...].astype(BF16)

    gate = _sigmoid(jnp.dot(xb_ref[...], wg_ref[...], preferred_element_type=F32))
    emb = jnp.dot(pb_ref[...], we_ref[...], preferred_element_type=F32)
    o_ref[...] = xres_ref[...] + gate * emb


def ple_update(x, p, wg, we, name):
    m, k = x.shape
    n = wg.shape[1]
    kp = p.shape[1]
    tm = _tile(m, 1024)
    tn = _tile(n, 512)
    return pl.pallas_call(
        _ple_kernel,
        grid=(m // tm, n // tn),
        in_specs=[
            pl.BlockSpec((tm, k), lambda i, j: (i, 0)),
            pl.BlockSpec((tm, tn), lambda i, j: (i, j)),
            pl.BlockSpec((tm, kp), lambda i, j: (i, 0)),
            pl.BlockSpec((k, tn), lambda i, j: (0, j)),
            pl.BlockSpec((kp, tn), lambda i, j: (0, j)),
        ],
        out_specs=pl.BlockSpec((tm, tn), lambda i, j: (i, j)),
        out_shape=jax.ShapeDtypeStruct((m, n), F32),
        scratch_shapes=[pltpu.VMEM((tm, k), BF16), pltpu.VMEM((tm, kp), BF16)],
        compiler_params=_cparams(("parallel", "arbitrary")),
        name=name,
    )(x, x, p, wg, we)


def _conv3(x, prev_row, next_row, w_ref, lanes, pad_ref):
    ts = x.shape[0]
    lo = SUBLANES
    pad_ref[lo:lo + ts, lanes] = x
    pad_ref[lo - 1:lo, lanes] = prev_row
    pad_ref[lo + ts:lo + ts + 1, lanes] = next_row
    xp = pad_ref[lo - 1:lo + ts - 1, lanes]
    xn = pad_ref[lo + 1:lo + ts + 1, lanes]
    return xp * w_ref[0:1, lanes] + x * w_ref[1:2, lanes] + xn * w_ref[2:3, lanes]


def _halo_specs(batch, seq, ts, width, col_fn):
    n_s = seq // ts
    per_seq = seq // SUBLANES
    per_tile = ts // SUBLANES
    last = batch * per_seq - 1

    def prev_map(b, i, *rest):
        return (jnp.maximum(b * per_seq + i * per_tile - 1, 0), col_fn(*rest))

    def next_map(b, i, *rest):
        return (jnp.minimum(b * per_seq + (i + 1) * per_tile, last), col_fn(*rest))

    return pl.BlockSpec((SUBLANES, width), prev_map), pl.BlockSpec((SUBLANES, width), next_map), n_s


def _ffn_act_kernel(ug_ref, ugp_ref, ugn_ref, uv_ref, uvp_ref, uvn_ref, wg_ref, wv_ref, bg_ref, bv_ref, o_ref,
                    padg_ref, padv_ref, *, n_s):
    i = pl.program_id(1)
    has_prev = (i > 0).astype(F32)
    has_next = (i < n_s - 1).astype(F32)
    all_lanes = slice(None)

    def conv(u_ref, p_ref, n_ref, w_ref, b_ref, pad_ref):
        x = u_ref[...].astype(F32)
        prev_row = p_ref[SUBLANES - 1:SUBLANES, :].astype(F32) * has_prev
        next_row = n_ref[0:1, :].astype(F32) * has_next
        return _conv3(x, prev_row, next_row, w_ref, all_lanes, pad_ref) + b_ref[...]

    gate = conv(ug_ref, ugp_ref, ugn_ref, wg_ref, bg_ref, padg_ref)
    val = conv(uv_ref, uvp_ref, uvn_ref, wv_ref, bv_ref, padv_ref)
    o_ref[...] = (gate * _sigmoid(gate) * val).astype(o_ref.dtype)


def ffn_act(u, conv_w, conv_b, batch, seq, name):
    t, two_f = u.shape
    f = two_f // 2
    ts = _tile(seq, 256)
    tc = 512
    nj = f // tc
    prev_g, next_g, n_s = _halo_specs(batch, seq, ts, tc, lambda j: j)
    prev_v, next_v, _ = _halo_specs(batch, seq, ts, tc, lambda j: nj + j)
    main_g = pl.BlockSpec((ts, tc), lambda b, i, j: (b * n_s + i, j))
    main_v = pl.BlockSpec((ts, tc), lambda b, i, j: (b * n_s + i, nj + j))
    w_g = pl.BlockSpec((3, tc), lambda b, i, j: (0, j))
    w_v = pl.BlockSpec((3, tc), lambda b, i, j: (0, nj + j))
    b_g = pl.BlockSpec((1, tc), lambda b, i, j: (0, j))
    b_v = pl.BlockSpec((1, tc), lambda b, i, j: (0, nj + j))
    cb = conv_b.reshape(1, two_f)
    return pl.pallas_call(
        functools.partial(_ffn_act_kernel, n_s=n_s),
        grid=(batch, n_s, nj),
        in_specs=[main_g, prev_g, next_g, main_v, prev_v, next_v, w_g, w_v, b_g, b_v],
        out_specs=pl.BlockSpec((ts, tc), lambda b, i, j: (b * n_s + i, j)),
        out_shape=jax.ShapeDtypeStruct((t, f), BF16),
        scratch_shapes=[pltpu.VMEM((ts + 2 * SUBLANES, tc), F32)] * 2,
        compiler_params=_cparams(("parallel", "parallel", "arbitrary")),
        name=name,
    )(u, u, u, u, u, u, conv_w, conv_w, cb, cb)


def _gdn_prep_kernel(x_ref, xp_ref, xn_ref, ab_ref, w_ref, alog_ref, dtb_ref, q_ref, k_ref, v_ref, gb_ref, pad_ref,
                     *, n_s):
    i = pl.program_id(1)
    has_prev = (i > 0).astype(F32)
    has_next = (i < n_s - 1).astype(F32)
    for c in range(3 * HA):
        lanes = slice(c * DA, (c + 1) * DA)
        x = x_ref[:, lanes].astype(F32)
        prev_row = xp_ref[SUBLANES - 1:SUBLANES, lanes].astype(F32) * has_prev
        next_row = xn_ref[0:1, lanes].astype(F32) * has_next
        y = _conv3(x, prev_row, next_row, w_ref, lanes, pad_ref)
        y = y * _sigmoid(y)
        out_lanes = slice((c % HA) * DA, (c % HA + 1) * DA)
        if c < 2 * HA:
            y = y * lax.rsqrt(jnp.sum(y * y, axis=-1, keepdims=True) + NORM_EPS)
            (q_ref if c < HA else k_ref)[:, out_lanes] = y.astype(BF16)
        else:
            v_ref[:, out_lanes] = y.astype(BF16)
    ab = ab_ref[...].astype(F32)
    lane = lax.broadcasted_iota(jnp.int32, (1, AB_WIDTH), 1) % LANES
    g = -jnp.exp(alog_ref[...]) * _softplus(ab + dtb_ref[...])
    gb_ref[...] = jnp.where(lane < GDN_HEAD_GROUP, g, _sigmoid(ab))


def gdn_prep(proj, conv_w, alog_row, dtb_row, batch, seq, name):
    t = proj.shape[0]
    ts = _tile(seq, 256)
    w3 = 3 * A_WIDTH
    prev_s, next_s, n_s = _halo_specs(batch, seq, ts, w3, lambda: 0)
    row = lambda b, i: (b * n_s + i, 0)
    out_spec = pl.BlockSpec((ts, A_WIDTH), row)
    return pl.pallas_call(
        functools.partial(_gdn_prep_kernel, n_s=n_s),
        grid=(batch, n_s),
        in_specs=[
            pl.BlockSpec((ts, w3), row), prev_s, next_s,
            pl.BlockSpec((ts, AB_WIDTH), lambda b, i: (b * n_s + i, OFF_AB // AB_WIDTH)),
            pl.BlockSpec((3, w3), lambda b, i: (0, 0)),
            pl.BlockSpec((1, AB_WIDTH), lambda b, i: (0, 0)),
            pl.BlockSpec((1, AB_WIDTH), lambda b, i: (0, 0)),
        ],
        out_specs=[out_spec, out_spec, out_spec, pl.BlockSpec((ts, AB_WIDTH), row)],
        out_shape=[jax.ShapeDtypeStruct((t, A_WIDTH), BF16)] * 3 + [jax.ShapeDtypeStruct((t, AB_WIDTH), F32)],
        scratch_shapes=[pltpu.VMEM((ts + 2 * SUBLANES, w3), F32)],
        compiler_params=_cparams(("parallel", "parallel")),
        name=name,
    )(proj, proj, proj, proj, conv_w, alog_row, dtb_row)


def _gdn_scan_kernel(q_ref, k_ref, v_ref, gb_ref, o_ref, s_ref, *, n_chunks):
    c_len = DELTA_CHUNK
    d = pl.program_id(1)

    @pl.when(pl.program_id(3) == 0)
    def _():
        s_ref[...] = jnp.zeros_like(s_ref)

    sgn = 1 - 2 * d
    row = lax.broadcasted_iota(jnp.int32, (c_len, c_len), 0)
    col = lax.broadcasted_iota(jnp.int32, (c_len, c_len), 1)
    order = (row - col) * sgn
    incl = order >= 0
    strict = order > 0
    cum_mat = incl.astype(F32)
    eye = (row == col).astype(F32)
    sel = (lax.broadcasted_iota(jnp.int32, (SUBLANES, LANES), 0)
           == lax.broadcasted_iota(jnp.int32, (SUBLANES, LANES), 1)).astype(F32)
    scale = DA ** -0.5
    nt = (((1,), (1,)), ((), ()))
    tn = (((0,), (0,)), ((), ()))

    heads = range(GDN_HEAD_GROUP)
    chunk_rows, gcols, grows, exp_gs, exp_rests, exp_tots, gblks = [], [], [], [], [], [], []
    for j in range(n_chunks):
        c = j + d * (n_chunks - 1 - 2 * j)
        rows = pl.ds(pl.multiple_of(c * c_len, c_len), c_len)
        gblk = gb_ref[rows, :]
        gcol = jnp.dot(cum_mat, gblk, precision=HIGHEST, preferred_element_type=F32)
        chunk_rows.append(rows)
        gblks.append(gblk)
        gcols.append(gcol)
    for j in range(n_chunks):
        grows.append(lax.dot_general(sel, gcols[j], nt, precision=HIGHEST, preferred_element_type=F32))
        gtot = jnp.sum(gblks[j], axis=0, keepdims=True)
        exp_gs.append(jnp.exp(gcols[j]))
        exp_rests.append(jnp.exp(gtot - gcols[j]))
        exp_tots.append(jnp.exp(gtot))

    units = [(j, h) for j in range(n_chunks) for h in heads]
    low, intra, rhs = {}, {}, {}
    for (j, h) in units:
        rows = chunk_rows[j]
        lanes = slice(h * DA, (h + 1) * DA)
        kbf = k_ref[rows, lanes]
        kf = kbf.astype(F32)
        beta = gblks[j][:, GDN_HEAD_GROUP + h:GDN_HEAD_GROUP + h + 1]
        diff = gcols[j][:, h:h + 1] - grows[j][h:h + 1, :]
        decay = jnp.where(incl, jnp.exp(jnp.where(incl, diff, 0.0)), 0.0)
        kb = kf * beta
        kk = lax.dot_general(kb.astype(BF16), kbf, nt, preferred_element_type=F32)
        qs = (q_ref[rows, lanes].astype(F32) * scale).astype(BF16)
        qk = lax.dot_general(qs, kbf, nt, preferred_element_type=F32)
        low[j, h] = jnp.where(strict, kk * decay, 0.0)
        intra[j, h] = jnp.where(incl, qk * decay, 0.0).astype(BF16)
        rhs[j, h] = jnp.concatenate([v_ref[rows, lanes].astype(F32) * beta, kb * exp_gs[j][:, h:h + 1]],
                                    axis=1).astype(BF16)

    def square(p):
        return {u: jnp.dot(p[u].astype(BF16), p[u].astype(BF16), preferred_element_type=F32) for u in units}

    def pair(first, p_second):
        return {u: first[u] + jnp.dot(first[u].astype(BF16), p_second[u].astype(BF16), preferred_element_type=F32)
                for u in units}

    p2 = square(low)
    p4 = square(p2)
    fac_a = pair({u: eye - low[u] for u in units}, p2)
    p8 = square(p4)
    p16 = square(p8)
    fac_b = pair({u: eye + p4[u] for u in units}, p8)
    p32 = square(p16)
    fac_ab = {u: jnp.dot(fac_a[u].astype(BF16), fac_b[u].astype(BF16), preferred_element_type=F32) for u in units}
    fac_c = pair({u: eye + p16[u] for u in units}, p32)
    inv = {u: jnp.dot(fac_ab[u].astype(BF16), fac_c[u].astype(BF16), preferred_element_type=F32) for u in units}
    uw = {u: jnp.dot(inv[u].astype(BF16), rhs[u], preferred_element_type=F32) for u in units}

    for j in range(n_chunks):
        rows = chunk_rows[j]
        state = [s_ref[h] for h in heads]
        sb = [st.astype(BF16) for st in state]
        vnb = [(uw[j, h][:, :DA] - jnp.dot(uw[j, h][:, DA:].astype(BF16), sb[h], preferred_element_type=F32)
                ).astype(BF16) for h in heads]
        for h in heads:
            lanes = slice(h * DA, (h + 1) * DA)
            kf = k_ref[rows, lanes].astype(F32)
            q_dec = (q_ref[rows, lanes].astype(F32) * scale * exp_gs[j][:, h:h + 1]).astype(BF16)
            out = (jnp.dot(q_dec, sb[h], preferred_element_type=F32)
                   + jnp.dot(intra[j, h], vnb[h], preferred_element_type=F32))
            k_dec = (kf * exp_rests[j][:, h:h + 1]).astype(BF16)
            s_ref[h] = (state[h] * exp_tots[j][:, h:h + 1]
                        + lax.dot_general(k_dec, vnb[h], tn, preferred_element_type=F32))
            o_ref[0, rows, lanes] = out.astype(o_ref.dtype)


def gdn_scan(q, k, v, gb, batch, seq, name):
    t = q.shape[0]
    blk = GDN_BLOCK
    n_blk = seq // blk
    width = GDN_HEAD_GROUP * DA
    n_hg = HA // GDN_HEAD_GROUP

    def rblk(b, d, i):
        return b * n_blk + i + d * (n_blk - 1 - 2 * i)

    qkv_spec = pl.BlockSpec((blk, width), lambda b, d, g, i: (rblk(b, d, i), g))
    return pl.pallas_call(
        functools.partial(_gdn_scan_kernel, n_chunks=blk // DELTA_CHUNK),
        grid=(batch, 2, n_hg, n_blk),
        in_specs=[qkv_spec, qkv_spec, qkv_spec,
                  pl.BlockSpec((blk, LANES), lambda b, d, g, i: (rblk(b, d, i), d * n_hg + g))],
        out_specs=pl.BlockSpec((1, blk, width), lambda b, d, g, i: (d, rblk(b, d, i), g)),
        out_shape=jax.ShapeDtypeStruct((2, t, A_WIDTH), BF16),
        scratch_shapes=[pltpu.VMEM((GDN_HEAD_GROUP, DA, DA), F32)],
        compiler_params=_cparams(("parallel", "parallel", "parallel", "arbitrary")),
        name=name,
    )(q, k, v, gb)


def _gdn_post_kernel(o_ref, z_ref, g_ref, y_ref):
    for h in range(HA):
        lanes = slice(h * DA, (h + 1) * DA)
        o = o_ref[0, :, lanes].astype(F32) + o_ref[1, :, lanes].astype(F32)
        z = z_ref[:, lanes].astype(F32)
        ms = jnp.mean(o * o, axis=-1, keepdims=True)
        y = o * lax.rsqrt(ms + NORM_EPS) * g_ref[...]
        y_ref[:, lanes] = (y * (z * _sigmoid(z))).astype(y_ref.dtype)


def gdn_post(o2, proj, norm_w, name):
    t = proj.shape[0]
    ts = _tile(t, 512)
    return pl.pallas_call(
        _gdn_post_kernel,
        grid=(t // ts,),
        in_specs=[
            pl.BlockSpec((2, ts, A_WIDTH), lambda i: (0, i, 0)),
            pl.BlockSpec((ts, A_WIDTH), lambda i: (i, OFF_Z // A_WIDTH)),
            pl.BlockSpec((1, DA), lambda i: (0, 0)),
        ],
        out_specs=pl.BlockSpec((ts, A_WIDTH), lambda i: (i, 0)),
        out_shape=jax.ShapeDtypeStruct((t, A_WIDTH), BF16),
        compiler_params=_cparams(("parallel",)),
        name=name,
    )(o2, proj, norm_w.reshape(1, DA))


def _qkv_prep_kernel(x_ref, cos_ref, sa_ref, sb_ref, gq_ref, gk_ref, qk_ref, vt_ref, *, n_q, n_k, n_v, seg, shift,
                     q_scale, split):
    grp_r = lax.broadcasted_iota(jnp.int32, (LANES, LANES), 0) // seg
    grp_c = lax.broadcasted_iota(jnp.int32, (LANES, LANES), 1) // seg
    seg_ones = (grp_r == grp_c).astype(BF16)
    lane = lax.broadcasted_iota(jnp.int32, (1, LANES), 1)
    cos, sin_a, sin_b = cos_ref[...], sa_ref[...], sb_ref[...]
    per = 2 if split else 1
    for h in range(n_q + n_k):
        is_q = h < n_q
        x = x_ref[:, h * LANES:(h + 1) * LANES].astype(F32)
        sq = x * x
        hi = sq.astype(BF16)
        lo = (sq - hi.astype(F32)).astype(BF16)
        ssum = (jnp.dot(hi, seg_ones, preferred_element_type=F32) + jnp.dot(lo, seg_ones, preferred_element_type=F32))
        gain = (gq_ref if is_q else gk_ref)[...]
        y = x * lax.rsqrt(ssum * (1.0 / seg) + NORM_EPS) * gain
        y = y * cos + pltpu.roll(y, LANES - shift, 1) * sin_a + pltpu.roll(y, shift, 1) * sin_b
        if is_q:
            y = y * q_scale
        if is_q and split:
            qk_ref[0, per * h] = jnp.where(lane < seg, y, 0.0).astype(qk_ref.dtype)
            qk_ref[0, per * h + 1] = jnp.where(lane >= seg, y, 0.0).astype(qk_ref.dtype)
        else:
            slot = per * h if is_q else per * n_q + (h - n_q)
            qk_ref[0, slot] = y.astype(qk_ref.dtype)
    for h in range(n_v):
        c0 = (n_q + n_k + h) * LANES
        vt_ref[0, h, 0] = x_ref[:, c0:c0 + LANES].astype(F32).T.astype(vt_ref.dtype)


def qkv_prep(proj, tables, gq, gk, batch, seq, *, col_off, n_q, n_k, n_v, seg, shift, q_scale, split, tk, name):
    ts = tk
    n_s = seq // ts
    width = (n_q + n_k + n_v) * LANES
    slots = (2 if split else 1) * n_q + n_k
    tab = pl.BlockSpec((ts, LANES), lambda b, i: (i, 0))
    gain = pl.BlockSpec((1, LANES), lambda b, i: (0, 0))
    return pl.pallas_call(
        functools.partial(_qkv_prep_kernel, n_q=n_q, n_k=n_k, n_v=n_v, seg=seg, shift=shift, q_scale=q_scale,
                          split=split),
        grid=(batch, n_s),
        in_specs=[pl.BlockSpec((ts, width), lambda b, i: (b * n_s + i, col_off // width)), tab, tab, tab, gain, gain],
        out_specs=[pl.BlockSpec((1, slots, ts, LANES), lambda b, i: (b, 0, i, 0)),
                   pl.BlockSpec((1, n_v, 1, LANES, tk), lambda b, i: (b, 0, i, 0, 0))],
        out_shape=[jax.ShapeDtypeStruct((batch, slots, seq, LANES), BF16),
                   jax.ShapeDtypeStruct((batch, n_v, n_s, LANES, tk), BF16)],
        compiler_params=_cparams(("parallel", "parallel")),
        name=name,
    )(proj, tables[0], tables[1], tables[2], gq.reshape(1, LANES), gk.reshape(1, LANES))


def _flash_kernel(*refs, groups, tq, tk, seq, diff, lambda_init):
    if diff:
        q_ref, k_ref, vt_ref, lam_ref, sub_ref, o_ref, qt_ref, sa_ref, sb_ref, ma_ref, mb_ref, m_ref, l_ref, acc_ref = refs
    else:
        q_ref, k_ref, vt_ref, o_ref, qt_ref, sa_ref, sb_ref, ma_ref, mb_ref, m_ref, l_ref, acc_ref = refs
    rows = groups * tq
    n_c = seq // tk
    qt_ref[...] = q_ref[0].reshape(rows, LANES).astype(F32).T.astype(BF16)
    m_ref[...] = jnp.full(m_ref.shape, -jnp.inf, F32)
    l_ref[...] = jnp.zeros(l_ref.shape, F32)
    acc_ref[...] = jnp.zeros(acc_ref.shape, F32)
    buf_a = (sa_ref, ma_ref)
    buf_b = (sb_ref, mb_ref)

    def scores_into(c, buf):
        ks = pl.ds(pl.multiple_of(c * tk, tk), tk)
        s_t = jnp.dot(k_ref[0, 0, ks, :], qt_ref[...], preferred_element_type=F32)
        buf[0][...] = s_t
        buf[1][...] = jnp.max(s_t, axis=0, keepdims=True)

    def step(c, cur, nxt):
        if nxt is not None:
            scores_into(c + 1, nxt)
        m_prev = m_ref[...]
        m_new = jnp.maximum(m_prev, cur[1][...])
        p_t = jnp.exp2(cur[0][...] - m_new)
        alpha = jnp.exp2(m_prev - m_new)
        l_ref[...] = alpha * l_ref[...] + jnp.sum(p_t, axis=0, keepdims=True)
        acc_ref[...] = alpha * acc_ref[...] + jnp.dot(vt_ref[0, 0, c], p_t.astype(BF16),
                                                      preferred_element_type=F32)
        m_ref[...] = m_new

    scores_into(0, buf_a)
    n_pairs = (n_c - 1) // 2

    def body(i, carry):
        step(2 * i, buf_a, buf_b)
        step(2 * i + 1, buf_b, buf_a)
        return carry

    lax.fori_loop(0, n_pairs, body, 0)
    if (n_c - 1) % 2:
        step(n_c - 2, buf_a, buf_b)
        step(n_c - 1, buf_b, None)
    else:
        step(n_c - 1, buf_a, None)
    o_t = acc_ref[...] / l_ref[...]
    if not diff:
        o = o_t.T
        for g in range(groups):
            o_ref[:, g * LANES:(g + 1) * LANES] = o[g * tq:(g + 1) * tq].astype(o_ref.dtype)
    else:
        lf = lam_ref[...]
        lam = (jnp.exp(jnp.sum(lf[0:1] * lf[1:2], axis=-1, keepdims=True))
               - jnp.exp(jnp.sum(lf[2:3] * lf[3:4], axis=-1, keepdims=True)) + lambda_init)
        d_t = o_t[:, :tq] - lam * o_t[:, tq:]
        ms = jnp.mean(d_t * d_t, axis=0, keepdims=True)
        y = (d_t * lax.rsqrt(ms + NORM_EPS)).T * sub_ref[...] * (1.0 - lambda_init)
        o_ref[...] = y.astype(o_ref.dtype)


def flash_attention(qk, vt, batch, seq, *, kv_heads, groups, k_slot0, k_stride, diff=False,
                    lambdas=None, subln=None, lambda_init=0.0, name):
    tk = vt.shape[-1]
    n_c = seq // tk
    tq = _tile(seq, 1024 // groups)
    n_q = seq // tq
    rows = groups * tq
    out_w = (1 if diff else groups) * LANES
    in_specs = [
        pl.BlockSpec((1, groups, tq, LANES), lambda b, h, i: (b, h, i, 0)),
        pl.BlockSpec((1, 1, seq, LANES), lambda b, h, i: (b, k_slot0 + k_stride * h, 0, 0)),
        pl.BlockSpec((1, 1, n_c, LANES, tk), lambda b, h, i: (b, h, 0, 0, 0)),
    ]
    args = [qk, qk, vt]
    if diff:
        in_specs += [pl.BlockSpec(lambdas.shape, lambda b, h, i: (0, 0)),
                     pl.BlockSpec((1, LANES), lambda b, h, i: (0, 0))]
        args += [lambdas, subln.reshape(1, LANES)]
    return pl.pallas_call(
        functools.partial(_flash_kernel, groups=groups, tq=tq, tk=tk, seq=seq, diff=diff, lambda_init=lambda_init),
        grid=(batch, kv_heads, n_q),
        in_specs=in_specs,
        out_specs=pl.BlockSpec((tq, out_w), lambda b, h, i: (b * n_q + i, h)),
        out_shape=jax.ShapeDtypeStruct((batch * seq, kv_heads * out_w), BF16),
        scratch_shapes=[pltpu.VMEM((LANES, rows), BF16), pltpu.VMEM((tk, rows), F32), pltpu.VMEM((tk, rows), F32),
                        pltpu.VMEM((1, rows), F32), pltpu.VMEM((1, rows), F32),
                        pltpu.VMEM((1, rows), F32), pltpu.VMEM((1, rows), F32), pltpu.VMEM((LANES, rows), F32)],
        compiler_params=_cparams(("parallel", "parallel", "arbitrary")),
        name=name,
    )(*args)


def _rope_angles(pos, dim, theta):
    inv = theta ** (-jnp.arange(0, dim, 2, dtype=F32) / dim)
    return pos[:, None] * inv[None, :]


def _axial_tables(seq):
    t = jnp.arange(seq)
    row = (t // GRID_W).astype(F32)
    col = (t % GRID_W).astype(F32)
    half = DB // 2
    ang_r = _rope_angles(row, half, AXIAL_THETA)
    ang_c = _rope_angles(col, half, AXIAL_THETA)
    ang = jnp.concatenate([ang_r, ang_r, ang_c, ang_c], axis=-1)
    cos, sin = jnp.cos(ang), jnp.sin(ang)
    first = (jnp.arange(LANES) % half) < (half // 2)
    return cos, jnp.where(first, -sin, 0.0), jnp.where(first, 0.0, sin)


def _partial_tables(seq):
    pos = jnp.arange(seq, dtype=F32)
    ang8 = _rope_angles(pos, ROT_C, ROPE_THETA)
    lane = jnp.arange(LANES) % DC
    ang = jnp.take(ang8, lane % (ROT_C // 2), axis=1)
    rot = lane < ROT_C
    first = lane < ROT_C // 2
    cos = jnp.where(rot, jnp.cos(ang), 1.0)
    sin = jnp.where(rot, jnp.sin(ang), 0.0)
    return cos, jnp.where(first, -sin, 0.0), jnp.where(first, 0.0, sin)


def _permute_w_in(w):
    k = w.shape[0]
    alpha0 = 4 * A_WIDTH
    beta0 = alpha0 + 2 * HA
    blocks = []
    for d in range(2):
        for g in range(HA // GDN_HEAD_GROUP):
            a0 = alpha0 + d * HA + g * GDN_HEAD_GROUP
            b0 = beta0 + d * HA + g * GDN_HEAD_GROUP
            blocks += [w[:, a0:a0 + GDN_HEAD_GROUP], w[:, b0:b0 + GDN_HEAD_GROUP],
                       jnp.zeros((k, LANES - 2 * GDN_HEAD_GROUP), w.dtype)]
    return jnp.concatenate([w[:, :alpha0]] + blocks + [w[:, beta0 + 2 * HA:]], axis=1)


def _gdn_param_row(p):
    blocks = []
    for d in range(2):
        for g in range(HA // GDN_HEAD_GROUP):
            blocks += [p[d, g * GDN_HEAD_GROUP:(g + 1) * GDN_HEAD_GROUP], jnp.zeros((LANES - GDN_HEAD_GROUP,), p.dtype)]
    return jnp.concatenate(blocks).reshape(1, AB_WIDTH)


def _encoder_layer(x, p_emb, w, li, batch, seq, tabs_b, tabs_c, tag):
    lambda_init = 0.8 - 0.6 * math.exp(-0.3 * li)
    proj = norm_matmul(x, w['norm_mix'], w['w_in'], f"in_proj_{tag}")

    q_a, k_a, v_a, gb = gdn_prep(proj, w['conv_a'], w['alog_row'], w['dtb_row'], batch, seq, f"gdn_prep_{tag}")
    o2 = gdn_scan(q_a, k_a, v_a, gb, batch, seq, f"gdn_scan_{tag}")
    y_a = gdn_post(o2, proj, w['norm_a'], f"gdn_post_{tag}")

    tk = _tile(seq, ATTN_KV_CHUNK)
    qk_b, vt_b = qkv_prep(proj, tabs_b, w['qn_b'], w['kn_b'], batch, seq, col_off=OFF_BQ, n_q=HB, n_k=KVB, n_v=KVB,
                          seg=DB, shift=DB // 4, q_scale=DB ** -0.5 * LOG2_E, split=False, tk=tk,
                          name=f"prep_b_{tag}")
    y_b = flash_attention(qk_b, vt_b, batch, seq, kv_heads=KVB, groups=HB // KVB, k_slot0=HB, k_stride=1,
                          name=f"attn_b_{tag}")

    qk_c, vt_c = qkv_prep(proj, tabs_c, w['qn_c2'], w['kn_c2'], batch, seq, col_off=OFF_CQ, n_q=HC, n_k=HC, n_v=HC,
                          seg=DC, shift=ROT_C // 2, q_scale=DC ** -0.5 * LOG2_E, split=True, tk=tk,
                          name=f"prep_c_{tag}")
    y_c = flash_attention(qk_c, vt_c, batch, seq, kv_heads=HC, groups=2, k_slot0=2 * HC, k_stride=1,
                          diff=True, lambdas=w['lambdas_c'], subln=w['subln_c'],
                          lambda_init=lambda_init, name=f"attn_c_{tag}")

    mixed = merge_branches(y_a, y_b, y_c, proj, w['w_o_a'], w['w_o_b'], w['w_o_c'], f"merge_{tag}")
    x = res_matmul(mixed, w['w_out'], x, f"out_proj_{tag}")

    u = norm_matmul(x, w['norm_ffn'], w['w_up'], f"ffn_up_{tag}")
    act = ffn_act(u, w['conv_ffn'], w['conv_ffn_b'], batch, seq, f"ffn_act_{tag}")
    x = res_matmul(act, w['w_down'], x, f"ffn_down_{tag}")
    return ple_update(x, p_emb, w['w_ple_gate'], w['w_ple'], f"ple_{tag}")


def kernel(x_prompt, x_sample, p_prompt, p_sample, norm_mix, w_in, conv_a, a_log, dt_bias, norm_a, qn_b, kn_b, qn_c, kn_c, lambdas_c, subln_c, w_o_a, w_o_b, w_o_c, w_out, norm_ffn, w_up, conv_ffn, conv_ffn_b, w_down, w_ple, w_ple_gate):
    depth = w_in.shape[0]
    layers = []
    for i in range(depth):
        layers.append(dict(
            norm_mix=norm_mix[i], w_in=_permute_w_in(w_in[i]).astype(BF16), conv_a=conv_a[i],
            alog_row=_gdn_param_row(a_log[i]), dtb_row=_gdn_param_row(dt_bias[i]), norm_a=norm_a[i],
            qn_b=qn_b[i], kn_b=kn_b[i], qn_c2=jnp.tile(qn_c[i], 2), kn_c2=jnp.tile(kn_c[i], 2),
            lambdas_c=lambdas_c[i], subln_c=subln_c[i],
            w_o_a=w_o_a[i].astype(BF16), w_o_b=w_o_b[i].astype(BF16), w_o_c=w_o_c[i].astype(BF16),
            w_out=w_out[i].astype(BF16), norm_ffn=norm_ffn[i], w_up=w_up[i].astype(BF16),
            conv_ffn=conv_ffn[i], conv_ffn_b=conv_ffn_b[i], w_down=w_down[i].astype(BF16),
            w_ple=w_ple[i].astype(BF16), w_ple_gate=w_ple_gate[i].astype(BF16)))

    outs = []
    for tag, x, p in (("p", x_prompt, p_prompt), ("s", x_sample, p_sample)):
        batch, seq, d = x.shape
        tabs_b = _axial_tables(seq)
        tabs_c = _partial_tables(seq)
        h = x.reshape(batch * seq, d)
        for i in range(depth):
            h = _encoder_layer(h, p[i].reshape(batch * seq, -1), layers[i], i, batch, seq, tabs_b, tabs_c, f"{tag}{i}")
        outs.append(h.reshape(batch, seq, d))
    return tuple(outs)
```

```python
import functools
import math

import jax
import jax.numpy as jnp
from jax import lax
from jax.experimental import pallas as pl
from jax.experimental.pallas import tpu as pltpu

F32 = jnp.float32
BF16 = jnp.bfloat16
HIGHEST = lax.Precision.HIGHEST

D_MODEL = 2048
DEPTH = 2
GRID_W = 64
PLE_DIM = 256
NORM_EPS = 1e-6
HA = 8
DA = 128
A_WIDTH = HA * DA
DELTA_CHUNK = 64
HB = 8
KVB = 2
DB = 128
AXIAL_THETA = 10000.0
HC = 8
DC = 64
DVC = 2 * DC
ROT_C = DC // 4
ROPE_THETA = 500000.0
D_FF = 5632

LANES = 128
SUBLANES = 8
VMEM_LIMIT_BYTES = 56 * 1024 * 1024

OFF_QKV = 0
OFF_Z = 3072
OFF_AB = 4096
AB_WIDTH = 512
OFF_BQ = 4608
OFF_BV = 5888
OFF_CQ = 6144
OFF_CV = 8192
OFF_GATE = 9216
N_PROJ = 15360
GDN_HEAD_GROUP = 4
GDN_BLOCK = 256
ATTN_KV_CHUNK = 1024
V_ROWS = LANES + 16
LOG2_E = math.log2(math.e)


def _cparams(semantics):
    return pltpu.CompilerParams(dimension_semantics=semantics, vmem_limit_bytes=VMEM_LIMIT_BYTES)


def _tile(n, pref):
    t = min(n, pref)
    while n % t:
        t //= 2
    return t


def _sigmoid(x):
    return 1.0 / (1.0 + jnp.exp(-x))


def _softplus(x):
    return jnp.maximum(x, 0.0) + jnp.log(1.0 + jnp.exp(-jnp.abs(x)))


def _norm_matmul_kernel(x_ref, g_ref, w_ref, o_ref, h_ref, *, rows):
    @pl.when(pl.program_id(1) == 0)
    def _():
        def chunk(r, c):
            rs = pl.ds(pl.multiple_of(r * rows, rows), rows)
            x = x_ref[rs, :]
            ms = jnp.mean(x * x, axis=-1, keepdims=True)
            h_ref[rs, :] = (x * lax.rsqrt(ms + NORM_EPS) * g_ref[...]).astype(BF16)
            return c
        lax.fori_loop(0, x_ref.shape[0] // rows, chunk, 0)

    o_ref[...] = jnp.dot(h_ref[...], w_ref[...], preferred_element_type=F32).astype(o_ref.dtype)


def norm_matmul(x, g, w, name):
    m, k = x.shape
    n = w.shape[1]
    tm = _tile(m, 1024)
    tn = 768 if n % 768 == 0 else _tile(n, 1024)
    rows = _tile(tm, 128)
    return pl.pallas_call(
        functools.partial(_norm_matmul_kernel, rows=rows),
        grid=(m // tm, n // tn),
        in_specs=[
            pl.BlockSpec((tm, k), lambda i, j: (i, 0)),
            pl.BlockSpec((1, k), lambda i, j: (0, 0)),
            pl.BlockSpec((k, tn), lambda i, j: (0, j)),
        ],
        out_specs=pl.BlockSpec((tm, tn), lambda i, j: (i, j)),
        out_shape=jax.ShapeDtypeStruct((m, n), BF16),
        scratch_shapes=[pltpu.VMEM((tm, k), BF16)],
        compiler_params=_cparams(("parallel", "arbitrary")),
        name=name,
    )(x, g.reshape(1, k), w)


def _res_matmul_kernel(a_ref, w_ref, r_ref, o_ref):
    o_ref[...] = r_ref[...] + jnp.dot(a_ref[...], w_ref[...], preferred_element_type=F32)


def res_matmul(a, w, res, name):
    m, k = a.shape
    n = w.shape[1]
    tm = _tile(m, 1024)
    tn = _tile(n, 512)
    return pl.pallas_call(
        _res_matmul_kernel,
        grid=(m // tm, n // tn),
        in_specs=[
            pl.BlockSpec((tm, k), lambda i, j: (i, 0)),
            pl.BlockSpec((k, tn), lambda i, j: (0, j)),
            pl.BlockSpec((tm, tn), lambda i, j: (i, j)),
        ],
        out_specs=pl.BlockSpec((tm, tn), lambda i, j: (i, j)),
        out_shape=jax.ShapeDtypeStruct((m, n), F32),
        compiler_params=_cparams(("parallel", "arbitrary")),
        name=name,
    )(a, w, res)


def _merge_kernel(ya_ref, yb_ref, yc_ref, ga_ref, gb_ref, gc_ref, wa_ref, wb_ref, wc_ref, o_ref):
    def branch(y_ref, g_ref, w_ref):
        return _sigmoid(g_ref[...].astype(F32)) * jnp.dot(y_ref[...], w_ref[...], preferred_element_type=F32)

    o_ref[...] = (branch(ya_ref, ga_ref, wa_ref) + branch(yb_ref, gb_ref, wb_ref)
                  + branch(yc_ref, gc_ref, wc_ref)).astype(o_ref.dtype)


def merge_branches(ya, yb, yc, proj, wa, wb, wc, name):
    m, k = ya.shape
    n = wa.shape[1]
    tm = _tile(m, 512)
    tn = _tile(n, 1024)
    nj = n // tn
    gate_blk = OFF_GATE // tn

    def y_spec():
        return pl.BlockSpec((tm, k), lambda j, i: (i, 0))

    def g_spec(which):
        return pl.BlockSpec((tm, tn), lambda j, i, which=which: (i, gate_blk + which * nj + j))

    def w_spec():
        return pl.BlockSpec((k, tn), lambda j, i: (0, j))

    return pl.pallas_call(
        _merge_kernel,
        grid=(nj, m // tm),
        in_specs=[y_spec(), y_spec(), y_spec(), g_spec(0), g_spec(1), g_spec(2), w_spec(), w_spec(), w_spec()],
        out_specs=pl.BlockSpec((tm, tn), lambda j, i: (i, j)),
        out_shape=jax.ShapeDtypeStruct((m, n), BF16),
        compiler_params=_cparams(("parallel", "arbitrary")),
        name=name,
    )(ya, yb, yc, proj, proj, proj, wa, wb, wc)


def _ple_kernel(x_ref, xres_ref, p_ref, wg_ref, we_ref, o_ref, xb_ref, pb_ref):
    @pl.when(pl.program_id(1) == 0)
    def _():
        xb_ref[...] = x_ref[...].astype(BF16)
        pb_ref[...] = p_ref[...].astype(BF16)

    gate = _sigmoid(jnp.dot(xb_ref[...], wg_ref[...], preferred_element_type=F32))
    emb = jnp.dot(pb_ref[...], we_ref[...], preferred_element_type=F32)
    o_ref[...] = xres_ref[...] + gate * emb


def ple_update(x, p, wg, we, name):
    m, k = x.shape
    n = wg.shape[1]
    kp = p.shape[1]
    tm = _tile(m, 1024)
    tn = _tile(n, 512)
    return pl.pallas_call(
        _ple_kernel,
        grid=(m // tm, n // tn),
        in_specs=[
            pl.BlockSpec((tm, k), lambda i, j: (i, 0)),
            pl.BlockSpec((tm, tn), lambda i, j: (i, j)),
            pl.BlockSpec((tm, kp), lambda i, j: (i, 0)),
            pl.BlockSpec((k, tn), lambda i, j: (0, j)),
            pl.BlockSpec((kp, tn), lambda i, j: (0, j)),
        ],
        out_specs=pl.BlockSpec((tm, tn), lambda i, j: (i, j)),
        out_shape=jax.ShapeDtypeStruct((m, n), F32),
        scratch_shapes=[pltpu.VMEM((tm, k), BF16), pltpu.VMEM((tm, kp), BF16)],
        compiler_params=_cparams(("parallel", "arbitrary")),
        name=name,
    )(x, x, p, wg, we)


def _shift_mats(ts):
    r = lax.broadcasted_iota(jnp.int32, (ts, ts), 0)
    c = lax.broadcasted_iota(jnp.int32, (ts, ts), 1)
    return (c == r - 1).astype(BF16), (c == r + 1).astype(BF16)


def _conv3(xb, prev_row, next_row, w_ref, lanes, shifts):
    ts = xb.shape[0]
    w0, w1, w2 = w_ref[0:1, lanes], w_ref[1:2, lanes], w_ref[2:3, lanes]
    xp = jnp.dot(shifts[0], xb, preferred_element_type=F32)
    xn = jnp.dot(shifts[1], xb, preferred_element_type=F32)
    y = xp * w0 + xb.astype(F32) * w1 + xn * w2
    r8 = lax.broadcasted_iota(jnp.int32, (SUBLANES, 1), 0)
    first = y[0:SUBLANES] + jnp.where(r8 == 0, prev_row, 0.0) * w0
    last = y[ts - SUBLANES:] + jnp.where(r8 == SUBLANES - 1, next_row, 0.0) * w2
    return jnp.concatenate([first, y[SUBLANES:ts - SUBLANES], last], axis=0)


def _halo_specs(batch, seq, ts, width, col_fn):
    n_s = seq // ts
    per_seq = seq // SUBLANES
    per_tile = ts // SUBLANES
    last = batch * per_seq - 1

    def prev_map(b, i, *rest):
        return (jnp.maximum(b * per_seq + i * per_tile - 1, 0), col_fn(*rest))

    def next_map(b, i, *rest):
        return (jnp.minimum(b * per_seq + (i + 1) * per_tile, last), col_fn(*rest))

    return pl.BlockSpec((SUBLANES, width), prev_map), pl.BlockSpec((SUBLANES, width), next_map), n_s


def _ffn_act_kernel(ug_ref, ugp_ref, ugn_ref, uv_ref, uvp_ref, uvn_ref, wg_ref, wv_ref, bg_ref, bv_ref, o_ref, *, n_s):
    i = pl.program_id(1)
    has_prev = (i > 0).astype(F32)
    has_next = (i < n_s - 1).astype(F32)
    all_lanes = slice(None)
    shifts = _shift_mats(ug_ref.shape[0])

    def conv(u_ref, p_ref, n_ref, w_ref, b_ref):
        prev_row = p_ref[SUBLANES - 1:SUBLANES, :].astype(F32) * has_prev
        next_row = n_ref[0:1, :].astype(F32) * has_next
        return _conv3(u_ref[...], prev_row, next_row, w_ref, all_lanes, shifts) + b_ref[...]

    gate = conv(ug_ref, ugp_ref, ugn_ref, wg_ref, bg_ref)
    val = conv(uv_ref, uvp_ref, uvn_ref, wv_ref, bv_ref)
    o_ref[...] = (gate * _sigmoid(gate) * val).astype(o_ref.dtype)


def ffn_act(u, conv_w, conv_b, batch, seq, name):
    t, two_f = u.shape
    f = two_f // 2
    ts = _tile(seq, 256)
    tc = f // 4 if f % (4 * LANES) == 0 else 512
    nj = f // tc
    prev_g, next_g, n_s = _halo_specs(batch, seq, ts, tc, lambda j: j)
    prev_v, next_v, _ = _halo_specs(batch, seq, ts, tc, lambda j: nj + j)
    main_g = pl.BlockSpec((ts, tc), lambda b, i, j: (b * n_s + i, j))
    main_v = pl.BlockSpec((ts, tc), lambda b, i, j: (b * n_s + i, nj + j))
    w_g = pl.BlockSpec((3, tc), lambda b, i, j: (0, j))
    w_v = pl.BlockSpec((3, tc), lambda b, i, j: (0, nj + j))
    b_g = pl.BlockSpec((1, tc), lambda b, i, j: (0, j))
    b_v = pl.BlockSpec((1, tc), lambda b, i, j: (0, nj + j))
    cb = conv_b.reshape(1, two_f)
    return pl.pallas_call(
        functools.partial(_ffn_act_kernel, n_s=n_s),
        grid=(batch, n_s, nj),
        in_specs=[main_g, prev_g, next_g, main_v, prev_v, next_v, w_g, w_v, b_g, b_v],
        out_specs=pl.BlockSpec((ts, tc), lambda b, i, j: (b * n_s + i, j)),
        out_shape=jax.ShapeDtypeStruct((t, f), BF16),
        compiler_params=_cparams(("parallel", "parallel", "arbitrary")),
        name=name,
    )(u, u, u, u, u, u, conv_w, conv_w, cb, cb)


def _gdn_prep_kernel(x_ref, xp_ref, xn_ref, ab_ref, w_ref, alog_ref, dtb_ref, q_ref, k_ref, v_ref, gb_ref, *, n_s):
    i = pl.program_id(1)
    has_prev = (i > 0).astype(F32)
    has_next = (i < n_s - 1).astype(F32)
    shifts = _shift_mats(x_ref.shape[0])
    for c in range(3 * HA):
        lanes = slice(c * DA, (c + 1) * DA)
        prev_row = xp_ref[SUBLANES - 1:SUBLANES, lanes].astype(F32) * has_prev
        next_row = xn_ref[0:1, lanes].astype(F32) * has_next
        y = _conv3(x_ref[:, lanes], prev_row, next_row, w_ref, lanes, shifts)
        y = y * _sigmoid(y)
        out_lanes = slice((c % HA) * DA, (c % HA + 1) * DA)
        if c < 2 * HA:
            y = y * lax.rsqrt(jnp.sum(y * y, axis=-1, keepdims=True) + NORM_EPS)
            (q_ref if c < HA else k_ref)[:, out_lanes] = y.astype(BF16)
        else:
            v_ref[:, out_lanes] = y.astype(BF16)
    ab = ab_ref[...].astype(F32)
    lane = lax.broadcasted_iota(jnp.int32, (1, AB_WIDTH), 1) % LANES
    g = -jnp.exp(alog_ref[...]) * _softplus(ab + dtb_ref[...])
    gb_ref[...] = jnp.where(lane < GDN_HEAD_GROUP, g, _sigmoid(ab))


def gdn_prep(proj, conv_w, alog_row, dtb_row, batch, seq, name):
    t = proj.shape[0]
    ts = _tile(seq, 256)
    w3 = 3 * A_WIDTH
    prev_s, next_s, n_s = _halo_specs(batch, seq, ts, w3, lambda: 0)
    row = lambda b, i: (b * n_s + i, 0)
    out_spec = pl.BlockSpec((ts, A_WIDTH), row)
    return pl.pallas_call(
        functools.partial(_gdn_prep_kernel, n_s=n_s),
        grid=(batch, n_s),
        in_specs=[
            pl.BlockSpec((ts, w3), row), prev_s, next_s,
            pl.BlockSpec((ts, AB_WIDTH), lambda b, i: (b * n_s + i, OFF_AB // AB_WIDTH)),
            pl.BlockSpec((3, w3), lambda b, i: (0, 0)),
            pl.BlockSpec((1, AB_WIDTH), lambda b, i: (0, 0)),
            pl.BlockSpec((1, AB_WIDTH), lambda b, i: (0, 0)),
        ],
        out_specs=[out_spec, out_spec, out_spec, pl.BlockSpec((ts, AB_WIDTH), row)],
        out_shape=[jax.ShapeDtypeStruct((t, A_WIDTH), BF16)] * 3 + [jax.ShapeDtypeStruct((t, AB_WIDTH), F32)],
        compiler_params=_cparams(("parallel", "parallel")),
        name=name,
    )(proj, proj, proj, proj, conv_w, alog_row, dtb_row)


def _gdn_scan_kernel(q_ref, k_ref, v_ref, gb_ref, o_ref, s_ref, *, n_chunks):
    c_len = DELTA_CHUNK
    d = pl.program_id(1)

    @pl.when(pl.program_id(3) == 0)
    def _():
        s_ref[...] = jnp.zeros_like(s_ref)

    sgn = 1 - 2 * d
    row = lax.broadcasted_iota(jnp.int32, (c_len, c_len), 0)
    col = lax.broadcasted_iota(jnp.int32, (c_len, c_len), 1)
    order = (row - col) * sgn
    incl = order >= 0
    strict = order > 0
    cum_mat = incl.astype(F32)
    eye = (row == col).astype(F32)
    sel = (lax.broadcasted_iota(jnp.int32, (SUBLANES, LANES), 0)
           == lax.broadcasted_iota(jnp.int32, (SUBLANES, LANES), 1)).astype(F32)
    scale = DA ** -0.5
    nt = (((1,), (1,)), ((), ()))
    tn = (((0,), (0,)), ((), ()))

    heads = range(GDN_HEAD_GROUP)
    chunk_rows, gcols, grows, exp_gs, exp_rests, exp_tots, gblks = [], [], [], [], [], [], []
    for j in range(n_chunks):
        c = j + d * (n_chunks - 1 - 2 * j)
        rows = pl.ds(pl.multiple_of(c * c_len, c_len), c_len)
        gblk = gb_ref[rows, :]
        gcol = jnp.dot(cum_mat, gblk, precision=HIGHEST, preferred_element_type=F32)
        chunk_rows.append(rows)
        gblks.append(gblk)
        gcols.append(gcol)
    for j in range(n_chunks):
        grows.append(lax.dot_general(sel, gcols[j], nt, precision=HIGHEST, preferred_element_type=F32))
        gtot = jnp.sum(gblks[j], axis=0, keepdims=True)
        exp_gs.append(jnp.exp(gcols[j]))
        exp_rests.append(jnp.exp(gtot - gcols[j]))
        exp_tots.append(jnp.exp(gtot))

    units = [(j, h) for j in range(n_chunks) for h in heads]
    low, intra, rhs = {}, {}, {}
    for (j, h) in units:
        rows = chunk_rows[j]
        lanes = slice(h * DA, (h + 1) * DA)
        kbf = k_ref[rows, lanes]
        kf = kbf.astype(F32)
        beta = gblks[j][:, GDN_HEAD_GROUP + h:GDN_HEAD_GROUP + h + 1]
        diff = gcols[j][:, h:h + 1] - grows[j][h:h + 1, :]
        decay = jnp.where(incl, jnp.exp(jnp.where(incl, diff, 0.0)), 0.0)
        kb = kf * beta
        kk = lax.dot_general(kb.astype(BF16), kbf, nt, preferred_element_type=F32)
        qs = (q_ref[rows, lanes].astype(F32) * scale).astype(BF16)
        qk = lax.dot_general(qs, kbf, nt, preferred_element_type=F32)
        low[j, h] = jnp.where(strict, kk * decay, 0.0)
        intra[j, h] = jnp.where(incl, qk * decay, 0.0).astype(BF16)
        rhs[j, h] = jnp.concatenate([v_ref[rows, lanes].astype(F32) * beta, kb * exp_gs[j][:, h:h + 1]],
                                    axis=1).astype(BF16)

    def square(p):
        return {u: jnp.dot(p[u].astype(BF16), p[u].astype(BF16), preferred_element_type=F32) for u in units}

    def pair(first, p_second):
        return {u: first[u] + jnp.dot(first[u].astype(BF16), p_second[u].astype(BF16), preferred_element_type=F32)
                for u in units}

    p2 = square(low)
    p4 = square(p2)
    fac_a = pair({u: eye - low[u] for u in units}, p2)
    p8 = square(p4)
    p16 = square(p8)
    fac_b = pair({u: eye + p4[u] for u in units}, p8)
    p32 = square(p16)
    fac_ab = {u: jnp.dot(fac_a[u].astype(BF16), fac_b[u].astype(BF16), preferred_element_type=F32) for u in units}
    fac_c = pair({u: eye + p16[u] for u in units}, p32)
    inv = {u: jnp.dot(fac_ab[u].astype(BF16), fac_c[u].astype(BF16), preferred_element_type=F32) for u in units}
    uw = {u: jnp.dot(inv[u].astype(BF16), rhs[u], preferred_element_type=F32).astype(BF16) for u in units}

    chain_lhs, out_local, state_add = {}, {}, {}
    for (j, h) in units:
        rows = chunk_rows[j]
        lanes = slice(h * DA, (h + 1) * DA)
        k_dec = (k_ref[rows, lanes].astype(F32) * exp_rests[j][:, h:h + 1]).astype(BF16)
        k_uw = lax.dot_general(k_dec, uw[j, h], tn, preferred_element_type=F32)
        i_uw = jnp.dot(intra[j, h], uw[j, h], preferred_element_type=F32)
        q_dec = q_ref[rows, lanes].astype(F32) * scale * exp_gs[j][:, h:h + 1]
        chain_lhs[j, h] = jnp.concatenate([q_dec - i_uw[:, DA:], k_uw[:, DA:]], axis=0).astype(BF16)
        out_local[j, h] = i_uw[:, :DA]
        state_add[j, h] = k_uw[:, :DA]

    for j in range(n_chunks):
        rows = chunk_rows[j]
        for h in heads:
            lanes = slice(h * DA, (h + 1) * DA)
            state = s_ref[h]
            prod = jnp.dot(chain_lhs[j, h], state.astype(BF16), preferred_element_type=F32)
            s_ref[h] = state * exp_tots[j][:, h:h + 1] - prod[c_len:] + state_add[j, h]
            o_ref[0, rows, lanes] = (prod[:c_len] + out_local[j, h]).astype(o_ref.dtype)


def gdn_scan(q, k, v, gb, batch, seq, name):
    t = q.shape[0]
    blk = GDN_BLOCK
    n_blk = seq // blk
    width = GDN_HEAD_GROUP * DA
    n_hg = HA // GDN_HEAD_GROUP

    def rblk(b, d, i):
        return b * n_blk + i + d * (n_blk - 1 - 2 * i)

    qkv_spec = pl.BlockSpec((blk, width), lambda b, d, g, i: (rblk(b, d, i), g))
    return pl.pallas_call(
        functools.partial(_gdn_scan_kernel, n_chunks=blk // DELTA_CHUNK),
        grid=(batch, 2, n_hg, n_blk),
        in_specs=[qkv_spec, qkv_spec, qkv_spec,
                  pl.BlockSpec((blk, LANES), lambda b, d, g, i: (rblk(b, d, i), d * n_hg + g))],
        out_specs=pl.BlockSpec((1, blk, width), lambda b, d, g, i: (d, rblk(b, d, i), g)),
        out_shape=jax.ShapeDtypeStruct((2, t, A_WIDTH), BF16),
        scratch_shapes=[pltpu.VMEM((GDN_HEAD_GROUP, DA, DA), F32)],
        compiler_params=_cparams(("parallel", "parallel", "parallel", "arbitrary")),
        name=name,
    )(q, k, v, gb)


def _gdn_post_kernel(o_ref, z_ref, g_ref, y_ref):
    for h in range(HA):
        lanes = slice(h * DA, (h + 1) * DA)
        o = o_ref[0, :, lanes].astype(F32) + o_ref[1, :, lanes].astype(F32)
        z = z_ref[:, lanes].astype(F32)
        ms = jnp.mean(o * o, axis=-1, keepdims=True)
        y = o * lax.rsqrt(ms + NORM_EPS) * g_ref[...]
        y_ref[:, lanes] = (y * (z * _sigmoid(z))).astype(y_ref.dtype)


def gdn_post(o2, proj, norm_w, name):
    t = proj.shape[0]
    ts = _tile(t, 512)
    return pl.pallas_call(
        _gdn_post_kernel,
        grid=(t // ts,),
        in_specs=[
            pl.BlockSpec((2, ts, A_WIDTH), lambda i: (0, i, 0)),
            pl.BlockSpec((ts, A_WIDTH), lambda i: (i, OFF_Z // A_WIDTH)),
            pl.BlockSpec((1, DA), lambda i: (0, 0)),
        ],
        out_specs=pl.BlockSpec((ts, A_WIDTH), lambda i: (i, 0)),
        out_shape=jax.ShapeDtypeStruct((t, A_WIDTH), BF16),
        compiler_params=_cparams(("parallel",)),
        name=name,
    )(o2, proj, norm_w.reshape(1, DA))


def _qkv_prep_kernel(x_ref, cos_ref, sa_ref, sb_ref, gq_ref, gk_ref, qk_ref, vt_ref, *, n_q, n_k, n_v, seg, shift,
                     q_scale, split):
    grp_r = lax.broadcasted_iota(jnp.int32, (LANES, LANES), 0) // seg
    grp_c = lax.broadcasted_iota(jnp.int32, (LANES, LANES), 1) // seg
    seg_ones = (grp_r == grp_c).astype(BF16)
    lane = lax.broadcasted_iota(jnp.int32, (1, LANES), 1)
    cos, sin_a, sin_b = cos_ref[...], sa_ref[...], sb_ref[...]
    per = 2 if split else 1
    for h in range(n_q + n_k):
        is_q = h < n_q
        x = x_ref[:, h * LANES:(h + 1) * LANES].astype(F32)
        sq = x * x
        hi = sq.astype(BF16)
        lo = (sq - hi.astype(F32)).astype(BF16)
        ssum = (jnp.dot(hi, seg_ones, preferred_element_type=F32) + jnp.dot(lo, seg_ones, preferred_element_type=F32))
        gain = (gq_ref if is_q else gk_ref)[...]
        y = x * lax.rsqrt(ssum * (1.0 / seg) + NORM_EPS) * gain
        y = y * cos + pltpu.roll(y, LANES - shift, 1) * sin_a + pltpu.roll(y, shift, 1) * sin_b
        if is_q:
            y = y * q_scale
        if is_q and split:
            qk_ref[0, per * h] = jnp.where(lane < seg, y, 0.0).astype(qk_ref.dtype)
            qk_ref[0, per * h + 1] = jnp.where(lane >= seg, y, 0.0).astype(qk_ref.dtype)
        else:
            slot = per * h if is_q else per * n_q + (h - n_q)
            qk_ref[0, slot] = y.astype(qk_ref.dtype)
    for h in range(n_v):
        c0 = (n_q + n_k + h) * LANES
        vt_ref[0, h, 0, 0:LANES, :] = x_ref[:, c0:c0 + LANES].astype(F32).T.astype(vt_ref.dtype)
        vt_ref[0, h, 0, LANES:V_ROWS, :] = jnp.ones((V_ROWS - LANES, vt_ref.shape[-1]), vt_ref.dtype)


def qkv_prep(proj, tables, gq, gk, batch, seq, *, col_off, n_q, n_k, n_v, seg, shift, q_scale, split, tk, name):
    ts = tk
    n_s = seq // ts
    width = (n_q + n_k + n_v) * LANES
    slots = (2 if split else 1) * n_q + n_k
    tab = pl.BlockSpec((ts, LANES), lambda b, i: (i, 0))
    gain = pl.BlockSpec((1, LANES), lambda b, i: (0, 0))
    return pl.pallas_call(
        functools.partial(_qkv_prep_kernel, n_q=n_q, n_k=n_k, n_v=n_v, seg=seg, shift=shift, q_scale=q_scale,
                          split=split),
        grid=(batch, n_s),
        in_specs=[pl.BlockSpec((ts, width), lambda b, i: (b * n_s + i, col_off // width)), tab, tab, tab, gain, gain],
        out_specs=[pl.BlockSpec((1, slots, ts, LANES), lambda b, i: (b, 0, i, 0)),
                   pl.BlockSpec((1, n_v, 1, V_ROWS, tk), lambda b, i: (b, 0, i, 0, 0))],
        out_shape=[jax.ShapeDtypeStruct((batch, slots, seq, LANES), BF16),
                   jax.ShapeDtypeStruct((batch, n_v, n_s, V_ROWS, tk), BF16)],
        compiler_params=_cparams(("parallel", "parallel")),
        name=name,
    )(proj, tables[0], tables[1], tables[2], gq.reshape(1, LANES), gk.reshape(1, LANES))


def _flash_kernel(*refs, groups, tq, tk, seq, diff, lambda_init):
    if diff:
        q_ref, k_ref, vt_ref, lam_ref, sub_ref, o_ref, qt_ref, sa_ref, sb_ref, ma_ref, mb_ref, m_ref, acc_ref = refs
    else:
        q_ref, k_ref, vt_ref, o_ref, qt_ref, sa_ref, sb_ref, ma_ref, mb_ref, m_ref, acc_ref = refs
    rows = groups * tq
    n_c = seq // tk
    qt_ref[...] = q_ref[0].reshape(rows, LANES).astype(F32).T.astype(BF16)
    m_ref[...] = jnp.full(m_ref.shape, -jnp.inf, F32)
    acc_ref[...] = jnp.zeros(acc_ref.shape, F32)
    buf_a = (sa_ref, ma_ref)
    buf_b = (sb_ref, mb_ref)

    def scores_into(c, buf):
        ks = pl.ds(pl.multiple_of(c * tk, tk), tk)
        s_t = jnp.dot(k_ref[0, 0, ks, :], qt_ref[...], preferred_element_type=F32)
        buf[0][...] = s_t
        buf[1][...] = jnp.max(s_t, axis=0, keepdims=True)

    def step(c, cur, nxt):
        if nxt is not None:
            scores_into(c + 1, nxt)
        m_prev = m_ref[...]
        m_new = jnp.maximum(m_prev, cur[1][...])
        p_t = jnp.exp2(cur[0][...] - m_new)
        alpha = jnp.exp2(m_prev - m_new)
        acc_ref[...] = alpha * acc_ref[...] + jnp.dot(vt_ref[0, 0, c], p_t.astype(BF16),
                                                      preferred_element_type=F32)
        m_ref[...] = m_new

    scores_into(0, buf_a)
    n_pairs = (n_c - 1) // 2

    def body(i, carry):
        step(2 * i, buf_a, buf_b)
        step(2 * i + 1, buf_b, buf_a)
        return carry

    lax.fori_loop(0, n_pairs, body, 0)
    if (n_c - 1) % 2:
        step(n_c - 2, buf_a, buf_b)
        step(n_c - 1, buf_b, None)
    else:
        step(n_c - 1, buf_a, None)
    o_t = acc_ref[0:LANES, :] / acc_ref[LANES:LANES + 1, :]
    if not diff:
        o = o_t.T
        for g in range(groups):
            o_ref[:, g * LANES:(g + 1) * LANES] = o[g * tq:(g + 1) * tq].astype(o_ref.dtype)
    else:
        lf = lam_ref[...]
        lam = (jnp.exp(jnp.sum(lf[0:1] * lf[1:2], axis=-1, keepdims=True))
               - jnp.exp(jnp.sum(lf[2:3] * lf[3:4], axis=-1, keepdims=True)) + lambda_init)
        d_t = o_t[:, :tq] - lam * o_t[:, tq:]
        ms = jnp.mean(d_t * d_t, axis=0, keepdims=True)
        y = (d_t * lax.rsqrt(ms + NORM_EPS)).T * sub_ref[...] * (1.0 - lambda_init)
        o_ref[...] = y.astype(o_ref.dtype)


def flash_attention(qk, vt, batch, seq, *, kv_heads, groups, k_slot0, k_stride, diff=False,
                    lambdas=None, subln=None, lambda_init=0.0, name):
    tk = vt.shape[-1]
    n_c = seq // tk
    tq = _tile(seq, 1024 // groups)
    n_q = seq // tq
    rows = groups * tq
    out_w = (1 if diff else groups) * LANES
    in_specs = [
        pl.BlockSpec((1, groups, tq, LANES), lambda b, h, i: (b, h, i, 0)),
        pl.BlockSpec((1, 1, seq, LANES), lambda b, h, i: (b, k_slot0 + k_stride * h, 0, 0)),
        pl.BlockSpec((1, 1, n_c, V_ROWS, tk), lambda b, h, i: (b, h, 0, 0, 0)),
    ]
    args = [qk, qk, vt]
    if diff:
        in_specs += [pl.BlockSpec(lambdas.shape, lambda b, h, i: (0, 0)),
                     pl.BlockSpec((1, LANES), lambda b, h, i: (0, 0))]
        args += [lambdas, subln.reshape(1, LANES)]
    return pl.pallas_call(
        functools.partial(_flash_kernel, groups=groups, tq=tq, tk=tk, seq=seq, diff=diff, lambda_init=lambda_init),
        grid=(batch, kv_heads, n_q),
        in_specs=in_specs,
        out_specs=pl.BlockSpec((tq, out_w), lambda b, h, i: (b * n_q + i, h)),
        out_shape=jax.ShapeDtypeStruct((batch * seq, kv_heads * out_w), BF16),
        scratch_shapes=[pltpu.VMEM((LANES, rows), BF16), pltpu.VMEM((tk, rows), F32), pltpu.VMEM((tk, rows), F32),
                        pltpu.VMEM((1, rows), F32), pltpu.VMEM((1, rows), F32),
                        pltpu.VMEM((1, rows), F32), pltpu.VMEM((V_ROWS, rows), F32)],
        compiler_params=_cparams(("parallel", "parallel", "arbitrary")),
        name=name,
    )(*args)


def _rope_angles(pos, dim, theta):
    inv = theta ** (-jnp.arange(0, dim, 2, dtype=F32) / dim)
    return pos[:, None] * inv[None, :]


def _axial_tables(seq):
    t = jnp.arange(seq)
    row = (t // GRID_W).astype(F32)
    col = (t % GRID_W).astype(F32)
    half = DB // 2
    ang_r = _rope_angles(row, half, AXIAL_THETA)
    ang_c = _rope_angles(col, half, AXIAL_THETA)
    ang = jnp.concatenate([ang_r, ang_r, ang_c, ang_c], axis=-1)
    cos, sin = jnp.cos(ang), jnp.sin(ang)
    first = (jnp.arange(LANES) % half) < (half // 2)
    return cos, jnp.where(first, -sin, 0.0), jnp.where(first, 0.0, sin)


def _partial_tables(seq):
    pos = jnp.arange(seq, dtype=F32)
    ang8 = _rope_angles(pos, ROT_C, ROPE_THETA)
    lane = jnp.arange(LANES) % DC
    ang = jnp.take(ang8, lane % (ROT_C // 2), axis=1)
    rot = lane < ROT_C
    first = lane < ROT_C // 2
    cos = jnp.where(rot, jnp.cos(ang), 1.0)
    sin = jnp.where(rot, jnp.sin(ang), 0.0)
    return cos, jnp.where(first, -sin, 0.0), jnp.where(first, 0.0, sin)


def _permute_w_in(w):
    k = w.shape[0]
    alpha0 = 4 * A_WIDTH
    beta0 = alpha0 + 2 * HA
    blocks = []
    for d in range(2):
        for g in range(HA // GDN_HEAD_GROUP):
            a0 = alpha0 + d * HA + g * GDN_HEAD_GROUP
            b0 = beta0 + d * HA + g * GDN_HEAD_GROUP
            blocks += [w[:, a0:a0 + GDN_HEAD_GROUP], w[:, b0:b0 + GDN_HEAD_GROUP],
                       jnp.zeros((k, LANES - 2 * GDN_HEAD_GROUP), w.dtype)]
    return jnp.concatenate([w[:, :alpha0]] + blocks + [w[:, beta0 + 2 * HA:]], axis=1)


def _gdn_param_row(p):
    blocks = []
    for d in range(2):
        for g in range(HA // GDN_HEAD_GROUP):
            blocks += [p[d, g * GDN_HEAD_GROUP:(g + 1) * GDN_HEAD_GROUP], jnp.zeros((LANES - GDN_HEAD_GROUP,), p.dtype)]
    return jnp.concatenate(blocks).reshape(1, AB_WIDTH)


def _encoder_layer(x, p_emb, w, li, batch, seq, tabs_b, tabs_c, tag):
    lambda_init = 0.8 - 0.6 * math.exp(-0.3 * li)
    proj = norm_matmul(x, w['norm_mix'], w['w_in'], f"in_proj_{tag}")

    q_a, k_a, v_a, gb = gdn_prep(proj, w['conv_a'], w['alog_row'], w['dtb_row'], batch, seq, f"gdn_prep_{tag}")
    o2 = gdn_scan(q_a, k_a, v_a, gb, batch, seq, f"gdn_scan_{tag}")
    y_a = gdn_post(o2, proj, w['norm_a'], f"gdn_post_{tag}")

    tk = _tile(seq, ATTN_KV_CHUNK)
    qk_b, vt_b = qkv_prep(proj, tabs_b, w['qn_b'], w['kn_b'], batch, seq, col_off=OFF_BQ, n_q=HB, n_k=KVB, n_v=KVB,
                          seg=DB, shift=DB // 4, q_scale=DB ** -0.5 * LOG2_E, split=False, tk=tk,
                          name=f"prep_b_{tag}")
    y_b = flash_attention(qk_b, vt_b, batch, seq, kv_heads=KVB, groups=HB // KVB, k_slot0=HB, k_stride=1,
                          name=f"attn_b_{tag}")

    qk_c, vt_c = qkv_prep(proj, tabs_c, w['qn_c2'], w['kn_c2'], batch, seq, col_off=OFF_CQ, n_q=HC, n_k=HC, n_v=HC,
                          seg=DC, shift=ROT_C // 2, q_scale=DC ** -0.5 * LOG2_E, split=True, tk=tk,
                          name=f"prep_c_{tag}")
    y_c = flash_attention(qk_c, vt_c, batch, seq, kv_heads=HC, groups=2, k_slot0=2 * HC, k_stride=1,
                          diff=True, lambdas=w['lambdas_c'], subln=w['subln_c'],
                          lambda_init=lambda_init, name=f"attn_c_{tag}")

    mixed = merge_branches(y_a, y_b, y_c, proj, w['w_o_a'], w['w_o_b'], w['w_o_c'], f"merge_{tag}")
    x = res_matmul(mixed, w['w_out'], x, f"out_proj_{tag}")

    u = norm_matmul(x, w['norm_ffn'], w['w_up'], f"ffn_up_{tag}")
    act = ffn_act(u, w['conv_ffn'], w['conv_ffn_b'], batch, seq, f"ffn_act_{tag}")
    x = res_matmul(act, w['w_down'], x, f"ffn_down_{tag}")
    return ple_update(x, p_emb, w['w_ple_gate'], w['w_ple'], f"ple_{tag}")


def kernel(x_prompt, x_sample, p_prompt, p_sample, norm_mix, w_in, conv_a, a_log, dt_bias, norm_a, qn_b, kn_b, qn_c, kn_c, lambdas_c, subln_c, w_o_a, w_o_b, w_o_c, w_out, norm_ffn, w_up, conv_ffn, conv_ffn_b, w_down, w_ple, w_ple_gate):
    depth = w_in.shape[0]
    layers = []
    for i in range(depth):
        layers.append(dict(
            norm_mix=norm_mix[i], w_in=_permute_w_in(w_in[i]).astype(BF16), conv_a=conv_a[i],
            alog_row=_gdn_param_row(a_log[i]), dtb_row=_gdn_param_row(dt_bias[i]), norm_a=norm_a[i],
            qn_b=qn_b[i], kn_b=kn_b[i], qn_c2=jnp.tile(qn_c[i], 2), kn_c2=jnp.tile(kn_c[i], 2),
            lambdas_c=lambdas_c[i], subln_c=subln_c[i],
            w_o_a=w_o_a[i].astype(BF16), w_o_b=w_o_b[i].astype(BF16), w_o_c=w_o_c[i].astype(BF16),
            w_out=w_out[i].astype(BF16), norm_ffn=norm_ffn[i], w_up=w_up[i].astype(BF16),
            conv_ffn=conv_ffn[i], conv_ffn_b=conv_ffn_b[i], w_down=w_down[i].astype(BF16),
            w_ple=w_ple[i].astype(BF16), w_ple_gate=w_ple_gate[i].astype(BF16)))

    outs = []
    for tag, x, p in (("p", x_prompt, p_prompt), ("s", x_sample, p_sample)):
        batch, seq, d = x.shape
        tabs_b = _axial_tables(seq)
        tabs_c = _partial_tables(seq)
        h = x.reshape(batch * seq, d)
        for i in range(depth):
            h = _encoder_layer(h, p[i].reshape(batch * seq, -1), layers[i], i, batch, seq, tabs_b, tabs_c, f"{tag}{i}")
        outs.append(h.reshape(batch, seq, d))
    return tuple(outs)
```

```python
import functools
import math

import jax
import jax.numpy as jnp
from jax import lax
from jax.experimental import pallas as pl
from jax.experimental.pallas import tpu as pltpu

F32 = jnp.float32
BF16 = jnp.bfloat16
HIGHEST = lax.Precision.HIGHEST

D_MODEL = 2048
DEPTH = 2
GRID_W = 64
PLE_DIM = 256
NORM_EPS = 1e-6
HA = 8
DA = 128
A_WIDTH = HA * DA
DELTA_CHUNK = 64
HB = 8
KVB = 2
DB = 128
AXIAL_THETA = 10000.0
HC = 8
DC = 64
DVC = 2 * DC
ROT_C = DC // 4
ROPE_THETA = 500000.0
D_FF = 5632

LANES = 128
SUBLANES = 8
VMEM_LIMIT_BYTES = 56 * 1024 * 1024

OFF_QKV = 0
OFF_Z = 3072
OFF_AB = 4096
AB_WIDTH = 512
OFF_BQ = 4608
OFF_BV = 5888
OFF_CQ = 6144
OFF_CV = 8192
OFF_GATE = 9216
N_PROJ = 15360
GDN_HEAD_GROUP = 4
GDN_BLOCK = 512
ATTN_KV_CHUNK = 1024
FLASH_UNROLL = 4
V_ROWS = LANES + 16
LOG2_E = math.log2(math.e)


def _cparams(semantics):
    return pltpu.CompilerParams(dimension_semantics=semantics, vmem_limit_bytes=VMEM_LIMIT_BYTES)


def _tile(n, pref):
    t = min(n, pref)
    while n % t:
        t //= 2
    return t


def _sigmoid(x):
    return 1.0 / (1.0 + jnp.exp(-x))


def _softplus(x):
    return jnp.maximum(x, 0.0) + jnp.log(1.0 + jnp.exp(-jnp.abs(x)))


def _norm_matmul_kernel(x_ref, g_ref, w_ref, o_ref, h_ref, *, rows):
    @pl.when(pl.program_id(1) == 0)
    def _():
        def chunk(r, c):
            rs = pl.ds(pl.multiple_of(r * rows, rows), rows)
            x = x_ref[rs, :]
            ms = jnp.mean(x * x, axis=-1, keepdims=True)
            h_ref[rs, :] = (x * lax.rsqrt(ms + NORM_EPS) * g_ref[...]).astype(BF16)
            return c
        lax.fori_loop(0, x_ref.shape[0] // rows, chunk, 0)

    o_ref[...] = jnp.dot(h_ref[...], w_ref[...], preferred_element_type=F32).astype(o_ref.dtype)


def norm_matmul(x, g, w, name):
    m, k = x.shape
    n = w.shape[1]
    tm = _tile(m, 1024)
    tn = next((t for t in (1536, 1408) if n % t == 0), _tile(n, 1024))
    rows = _tile(tm, 128)
    return pl.pallas_call(
        functools.partial(_norm_matmul_kernel, rows=rows),
        grid=(m // tm, n // tn),
        in_specs=[
            pl.BlockSpec((tm, k), lambda i, j: (i, 0)),
            pl.BlockSpec((1, k), lambda i, j: (0, 0)),
            pl.BlockSpec((k, tn), lambda i, j: (0, j)),
        ],
        out_specs=pl.BlockSpec((tm, tn), lambda i, j: (i, j)),
        out_shape=jax.ShapeDtypeStruct((m, n), BF16),
        scratch_shapes=[pltpu.VMEM((tm, k), BF16)],
        compiler_params=_cparams(("parallel", "arbitrary")),
        name=name,
    )(x, g.reshape(1, k), w)


def _resident(shape):
    return pl.BlockSpec(shape, lambda *_: (0,) * len(shape), pipeline_mode=pl.Buffered(1))


def _merge_out_kernel(*refs, n_branch, n_chunk, tc):
    y_refs = refs[:n_branch]
    g_refs = refs[n_branch:n_branch + n_branch * n_chunk]
    w_refs = refs[n_branch + n_branch * n_chunk:2 * n_branch + n_branch * n_chunk]
    wo_ref, x_ref, o_ref, m_ref = refs[2 * n_branch + n_branch * n_chunk:]
    for c in range(n_chunk):
        cols = slice(c * tc, (c + 1) * tc)
        acc = None
        for i in range(n_branch):
            gate = _sigmoid(g_refs[i * n_chunk + c][...].astype(F32))
            term = gate * jnp.dot(y_refs[i][...], w_refs[i][:, cols], preferred_element_type=F32)
            acc = term if acc is None else acc + term
        m_ref[:, cols] = acc.astype(BF16)
    o_ref[...] = x_ref[...] + jnp.dot(m_ref[...], wo_ref[...], preferred_element_type=F32)


def merge_out(ys, proj, w_os, w_out, x, name):
    m, k = ys[0].shape
    n = w_out.shape[1]
    tm = _tile(m, 256)
    tc = _tile(n, 1024)
    n_chunk = n // tc
    n_branch = len(ys)
    gate_blk = OFF_GATE // tc
    row = lambda i: (i, 0)
    g_specs = [pl.BlockSpec((tm, tc), lambda i, b=b: (i, gate_blk + b)) for b in range(n_branch * n_chunk)]
    return pl.pallas_call(
        functools.partial(_merge_out_kernel, n_branch=n_branch, n_chunk=n_chunk, tc=tc),
        grid=(m // tm,),
        in_specs=([pl.BlockSpec((tm, k), row)] * n_branch + g_specs + [_resident((k, n))] * n_branch
                  + [_resident((n, n)), pl.BlockSpec((tm, n), row)]),
        out_specs=pl.BlockSpec((tm, n), row),
        out_shape=jax.ShapeDtypeStruct((m, n), F32),
        scratch_shapes=[pltpu.VMEM((tm, n), BF16)],
        compiler_params=_cparams(("parallel",)),
        name=name,
    )(*ys, *([proj] * (n_branch * n_chunk)), *w_os, w_out, x)


def _down_ple_kernel(a_ref, wd_ref, x_ref, p_ref, wg_ref, we_ref, o_ref):
    x2 = x_ref[...] + jnp.dot(a_ref[...], wd_ref[...], preferred_element_type=F32)
    gate = _sigmoid(jnp.dot(x2.astype(BF16), wg_ref[...], preferred_element_type=F32))
    emb = jnp.dot(p_ref[...].astype(BF16), we_ref[...], preferred_element_type=F32)
    o_ref[...] = x2 + gate * emb


def ffn_down_ple(act, w_down, x, p, wg, we, name):
    m, k = act.shape
    n = w_down.shape[1]
    kp = p.shape[1]
    tm = _tile(m, 256)
    row = lambda i: (i, 0)
    return pl.pallas_call(
        _down_ple_kernel,
        grid=(m // tm,),
        in_specs=[pl.BlockSpec((tm, k), row), _resident((k, n)), pl.BlockSpec((tm, n), row),
                  pl.BlockSpec((tm, kp), row), _resident((n, n)), _resident((kp, n))],
        out_specs=pl.BlockSpec((tm, n), row),
        out_shape=jax.ShapeDtypeStruct((m, n), F32),
        compiler_params=_cparams(("parallel",)),
        name=name,
    )(act, w_down, x, p, wg, we)


def _shift_mats(ts):
    r = lax.broadcasted_iota(jnp.int32, (ts, ts), 0)
    c = lax.broadcasted_iota(jnp.int32, (ts, ts), 1)
    return (c == r - 1).astype(BF16), (c == r + 1).astype(BF16)


def _conv3(xb, prev_row, next_row, w_ref, lanes, shifts):
    ts = xb.shape[0]
    w0, w1, w2 = w_ref[0:1, lanes], w_ref[1:2, lanes], w_ref[2:3, lanes]
    xp = jnp.dot(shifts[0], xb, preferred_element_type=F32)
    xn = jnp.dot(shifts[1], xb, preferred_element_type=F32)
    y = xp * w0 + xb.astype(F32) * w1 + xn * w2
    r8 = lax.broadcasted_iota(jnp.int32, (SUBLANES, 1), 0)
    first = y[0:SUBLANES] + jnp.where(r8 == 0, prev_row, 0.0) * w0
    last = y[ts - SUBLANES:] + jnp.where(r8 == SUBLANES - 1, next_row, 0.0) * w2
    return jnp.concatenate([first, y[SUBLANES:ts - SUBLANES], last], axis=0)


def _halo_specs(batch, seq, ts, width, col_fn):
    n_s = seq // ts
    per_seq = seq // SUBLANES
    per_tile = ts // SUBLANES
    last = batch * per_seq - 1

    def prev_map(b, i, *rest):
        return (jnp.maximum(b * per_seq + i * per_tile - 1, 0), col_fn(*rest))

    def next_map(b, i, *rest):
        return (jnp.minimum(b * per_seq + (i + 1) * per_tile, last), col_fn(*rest))

    return pl.BlockSpec((SUBLANES, width), prev_map), pl.BlockSpec((SUBLANES, width), next_map), n_s


def _ffn_act_kernel(ug_ref, ugp_ref, ugn_ref, uv_ref, uvp_ref, uvn_ref, wg_ref, wv_ref, bg_ref, bv_ref, o_ref, *, n_s):
    i = pl.program_id(1)
    has_prev = (i > 0).astype(F32)
    has_next = (i < n_s - 1).astype(F32)
    all_lanes = slice(None)
    shifts = _shift_mats(ug_ref.shape[0])

    def conv(u_ref, p_ref, n_ref, w_ref, b_ref):
        prev_row = p_ref[SUBLANES - 1:SUBLANES, :].astype(F32) * has_prev
        next_row = n_ref[0:1, :].astype(F32) * has_next
        return _conv3(u_ref[...], prev_row, next_row, w_ref, all_lanes, shifts) + b_ref[...]

    gate = conv(ug_ref, ugp_ref, ugn_ref, wg_ref, bg_ref)
    val = conv(uv_ref, uvp_ref, uvn_ref, wv_ref, bv_ref)
    o_ref[...] = (gate * _sigmoid(gate) * val).astype(o_ref.dtype)


def ffn_act(u, conv_w, conv_b, batch, seq, name):
    t, two_f = u.shape
    f = two_f // 2
    ts = _tile(seq, 256)
    tc = f // 4 if f % (4 * LANES) == 0 else 512
    nj = f // tc
    prev_g, next_g, n_s = _halo_specs(batch, seq, ts, tc, lambda j: j)
    prev_v, next_v, _ = _halo_specs(batch, seq, ts, tc, lambda j: nj + j)
    main_g = pl.BlockSpec((ts, tc), lambda b, i, j: (b * n_s + i, j))
    main_v = pl.BlockSpec((ts, tc), lambda b, i, j: (b * n_s + i, nj + j))
    w_g = pl.BlockSpec((3, tc), lambda b, i, j: (0, j))
    w_v = pl.BlockSpec((3, tc), lambda b, i, j: (0, nj + j))
    b_g = pl.BlockSpec((1, tc), lambda b, i, j: (0, j))
    b_v = pl.BlockSpec((1, tc), lambda b, i, j: (0, nj + j))
    cb = conv_b.reshape(1, two_f)
    return pl.pallas_call(
        functools.partial(_ffn_act_kernel, n_s=n_s),
        grid=(batch, n_s, nj),
        in_specs=[main_g, prev_g, next_g, main_v, prev_v, next_v, w_g, w_v, b_g, b_v],
        out_specs=pl.BlockSpec((ts, tc), lambda b, i, j: (b * n_s + i, j)),
        out_shape=jax.ShapeDtypeStruct((t, f), BF16),
        compiler_params=_cparams(("parallel", "parallel", "arbitrary")),
        name=name,
    )(u, u, u, u, u, u, conv_w, conv_w, cb, cb)


def _gdn_prep_kernel(x_ref, xp_ref, xn_ref, ab_ref, w_ref, alog_ref, dtb_ref, q_ref, k_ref, v_ref, gb_ref, *, n_s):
    i = pl.program_id(1)
    has_prev = (i > 0).astype(F32)
    has_next = (i < n_s - 1).astype(F32)
    shifts = _shift_mats(x_ref.shape[0])
    for c in range(3 * HA):
        lanes = slice(c * DA, (c + 1) * DA)
        prev_row = xp_ref[SUBLANES - 1:SUBLANES, lanes].astype(F32) * has_prev
        next_row = xn_ref[0:1, lanes].astype(F32) * has_next
        y = _conv3(x_ref[:, lanes], prev_row, next_row, w_ref, lanes, shifts)
        y = y * _sigmoid(y)
        out_lanes = slice((c % HA) * DA, (c % HA + 1) * DA)
        if c < 2 * HA:
            y = y * lax.rsqrt(jnp.sum(y * y, axis=-1, keepdims=True) + NORM_EPS)
            (q_ref if c < HA else k_ref)[:, out_lanes] = y.astype(BF16)
        else:
            v_ref[:, out_lanes] = y.astype(BF16)
    ab = ab_ref[...].astype(F32)
    lane = lax.broadcasted_iota(jnp.int32, (1, AB_WIDTH), 1) % LANES
    g = -jnp.exp(alog_ref[...]) * _softplus(ab + dtb_ref[...])
    gb_ref[...] = jnp.where(lane < GDN_HEAD_GROUP, g, _sigmoid(ab))


def gdn_prep(proj, conv_w, alog_row, dtb_row, batch, seq, name):
    t = proj.shape[0]
    ts = _tile(seq, 256)
    w3 = 3 * A_WIDTH
    prev_s, next_s, n_s = _halo_specs(batch, seq, ts, w3, lambda: 0)
    row = lambda b, i: (b * n_s + i, 0)
    out_spec = pl.BlockSpec((ts, A_WIDTH), row)
    return pl.pallas_call(
        functools.partial(_gdn_prep_kernel, n_s=n_s),
        grid=(batch, n_s),
        in_specs=[
            pl.BlockSpec((ts, w3), row), prev_s, next_s,
            pl.BlockSpec((ts, AB_WIDTH), lambda b, i: (b * n_s + i, OFF_AB // AB_WIDTH)),
            pl.BlockSpec((3, w3), lambda b, i: (0, 0)),
            pl.BlockSpec((1, AB_WIDTH), lambda b, i: (0, 0)),
            pl.BlockSpec((1, AB_WIDTH), lambda b, i: (0, 0)),
        ],
        out_specs=[out_spec, out_spec, out_spec, pl.BlockSpec((ts, AB_WIDTH), row)],
        out_shape=[jax.ShapeDtypeStruct((t, A_WIDTH), BF16)] * 3 + [jax.ShapeDtypeStruct((t, AB_WIDTH), F32)],
        compiler_params=_cparams(("parallel", "parallel")),
        name=name,
    )(proj, proj, proj, proj, conv_w, alog_row, dtb_row)


def _gdn_scan_kernel(q_ref, k_ref, v_ref, gb_ref, o_ref, s_ref, *, n_chunks):
    c_len = DELTA_CHUNK
    d = pl.program_id(1)

    @pl.when(pl.program_id(3) == 0)
    def _():
        s_ref[...] = jnp.zeros_like(s_ref)

    sgn = 1 - 2 * d
    row = lax.broadcasted_iota(jnp.int32, (c_len, c_len), 0)
    col = lax.broadcasted_iota(jnp.int32, (c_len, c_len), 1)
    order = (row - col) * sgn
    incl = order >= 0
    strict = order > 0
    cum_mat = incl.astype(F32)
    eye = (row == col).astype(F32)
    sel = (lax.broadcasted_iota(jnp.int32, (SUBLANES, LANES), 0)
           == lax.broadcasted_iota(jnp.int32, (SUBLANES, LANES), 1)).astype(F32)
    scale = DA ** -0.5
    nt = (((1,), (1,)), ((), ()))
    tn = (((0,), (0,)), ((), ()))

    heads = range(GDN_HEAD_GROUP)
    chunk_rows, gcols, grows, exp_gs, exp_rests, exp_tots, gblks = [], [], [], [], [], [], []
    for j in range(n_chunks):
        c = j + d * (n_chunks - 1 - 2 * j)
        rows = pl.ds(pl.multiple_of(c * c_len, c_len), c_len)
        gblk = gb_ref[rows, :]
        gcol = jnp.dot(cum_mat, gblk, precision=HIGHEST, preferred_element_type=F32)
        chunk_rows.append(rows)
        gblks.append(gblk)
        gcols.append(gcol)
    for j in range(n_chunks):
        grows.append(lax.dot_general(sel, gcols[j], nt, precision=HIGHEST, preferred_element_type=F32))
        gtot = jnp.sum(gblks[j], axis=0, keepdims=True)
        exp_gs.append(jnp.exp(gcols[j]))
        exp_rests.append(jnp.exp(gtot - gcols[j]))
        exp_tots.append(jnp.exp(gtot))

    units = [(j, h) for j in range(n_chunks) for h in heads]
    low, intra, rhs = {}, {}, {}
    for (j, h) in units:
        rows = chunk_rows[j]
        lanes = slice(h * DA, (h + 1) * DA)
        kbf = k_ref[rows, lanes]
        kf = kbf.astype(F32)
        beta = gblks[j][:, GDN_HEAD_GROUP + h:GDN_HEAD_GROUP + h + 1]
        diff = gcols[j][:, h:h + 1] - grows[j][h:h + 1, :]
        decay = jnp.where(incl, jnp.exp(jnp.where(incl, diff, 0.0)), 0.0)
        kb = kf * beta
        kk = lax.dot_general(kb.astype(BF16), kbf, nt, preferred_element_type=F32)
        qs = (q_ref[rows, lanes].astype(F32) * scale).astype(BF16)
        qk = lax.dot_general(qs, kbf, nt, preferred_element_type=F32)
        low[j, h] = jnp.where(strict, kk * decay, 0.0)
        intra[j, h] = jnp.where(incl, qk * decay, 0.0).astype(BF16)
        rhs[j, h] = jnp.concatenate([v_ref[rows, lanes].astype(F32) * beta, kb * exp_gs[j][:, h:h + 1]],
                                    axis=1).astype(BF16)

    def square(p):
        return {u: jnp.dot(p[u].astype(BF16), p[u].astype(BF16), preferred_element_type=F32) for u in units}

    def pair(first, p_second):
        return {u: first[u] + jnp.dot(first[u].astype(BF16), p_second[u].astype(BF16), preferred_element_type=F32)
                for u in units}

    p2 = square(low)
    p4 = square(p2)
    fac_a = pair({u: eye - low[u] for u in units}, p2)
    p8 = square(p4)
    p16 = square(p8)
    fac_b = pair({u: eye + p4[u] for u in units}, p8)
    p32 = square(p16)
    fac_ab = {u: jnp.dot(fac_a[u].astype(BF16), fac_b[u].astype(BF16), preferred_element_type=F32) for u in units}
    fac_c = pair({u: eye + p16[u] for u in units}, p32)
    inv = {u: jnp.dot(fac_ab[u].astype(BF16), fac_c[u].astype(BF16), preferred_element_type=F32) for u in units}
    uw = {u: jnp.dot(inv[u].astype(BF16), rhs[u], preferred_element_type=F32).astype(BF16) for u in units}

    chain_lhs, out_local, state_add = {}, {}, {}
    for (j, h) in units:
        rows = chunk_rows[j]
        lanes = slice(h * DA, (h + 1) * DA)
        k_dec = (k_ref[rows, lanes].astype(F32) * exp_rests[j][:, h:h + 1]).astype(BF16)
        k_uw = lax.dot_general(k_dec, uw[j, h], tn, preferred_element_type=F32)
        i_uw = jnp.dot(intra[j, h], uw[j, h], preferred_element_type=F32)
        q_dec = q_ref[rows, lanes].astype(F32) * scale * exp_gs[j][:, h:h + 1]
        chain_lhs[j, h] = jnp.concatenate([q_dec - i_uw[:, DA:], k_uw[:, DA:]], axis=0).astype(BF16)
        out_local[j, h] = i_uw[:, :DA]
        state_add[j, h] = k_uw[:, :DA]

    for j in range(n_chunks):
        rows = chunk_rows[j]
        for h in heads:
            lanes = slice(h * DA, (h + 1) * DA)
            state = s_ref[h]
            prod = jnp.dot(chain_lhs[j, h], state.astype(BF16), preferred_element_type=F32)
            s_ref[h] = state * exp_tots[j][:, h:h + 1] - prod[c_len:] + state_add[j, h]
            o_ref[0, rows, lanes] = (prod[:c_len] + out_local[j, h]).astype(o_ref.dtype)


def gdn_scan(q, k, v, gb, batch, seq, name):
    t = q.shape[0]
    blk = _tile(seq, GDN_BLOCK)
    n_blk = seq // blk
    width = GDN_HEAD_GROUP * DA
    n_hg = HA // GDN_HEAD_GROUP

    def rblk(b, d, i):
        return b * n_blk + i + d * (n_blk - 1 - 2 * i)

    qkv_spec = pl.BlockSpec((blk, width), lambda b, d, g, i: (rblk(b, d, i), g))
    return pl.pallas_call(
        functools.partial(_gdn_scan_kernel, n_chunks=blk // DELTA_CHUNK),
        grid=(batch, 2, n_hg, n_blk),
        in_specs=[qkv_spec, qkv_spec, qkv_spec,
                  pl.BlockSpec((blk, LANES), lambda b, d, g, i: (rblk(b, d, i), d * n_hg + g))],
        out_specs=pl.BlockSpec((1, blk, width), lambda b, d, g, i: (d, rblk(b, d, i), g)),
        out_shape=jax.ShapeDtypeStruct((2, t, A_WIDTH), BF16),
        scratch_shapes=[pltpu.VMEM((GDN_HEAD_GROUP, DA, DA), F32)],
        compiler_params=_cparams(("parallel", "parallel", "parallel", "arbitrary")),
        name=name,
    )(q, k, v, gb)


def _gdn_post_kernel(o_ref, z_ref, g_ref, y_ref):
    for h in range(HA):
        lanes = slice(h * DA, (h + 1) * DA)
        o = o_ref[0, :, lanes].astype(F32) + o_ref[1, :, lanes].astype(F32)
        z = z_ref[:, lanes].astype(F32)
        ms = jnp.mean(o * o, axis=-1, keepdims=True)
        y = o * lax.rsqrt(ms + NORM_EPS) * g_ref[...]
        y_ref[:, lanes] = (y * (z * _sigmoid(z))).astype(y_ref.dtype)


def gdn_post(o2, proj, norm_w, name):
    t = proj.shape[0]
    ts = _tile(t, 512)
    return pl.pallas_call(
        _gdn_post_kernel,
        grid=(t // ts,),
        in_specs=[
            pl.BlockSpec((2, ts, A_WIDTH), lambda i: (0, i, 0)),
            pl.BlockSpec((ts, A_WIDTH), lambda i: (i, OFF_Z // A_WIDTH)),
            pl.BlockSpec((1, DA), lambda i: (0, 0)),
        ],
        out_specs=pl.BlockSpec((ts, A_WIDTH), lambda i: (i, 0)),
        out_shape=jax.ShapeDtypeStruct((t, A_WIDTH), BF16),
        compiler_params=_cparams(("parallel",)),
        name=name,
    )(o2, proj, norm_w.reshape(1, DA))


def _qkv_prep_kernel(x_ref, cos_ref, sa_ref, sb_ref, gq_ref, gk_ref, qk_ref, vt_ref, *, n_q, n_k, n_v, seg, shift,
                     q_scale, split):
    grp_r = lax.broadcasted_iota(jnp.int32, (LANES, LANES), 0) // seg
    grp_c = lax.broadcasted_iota(jnp.int32, (LANES, LANES), 1) // seg
    seg_ones = (grp_r == grp_c).astype(BF16)
    lane = lax.broadcasted_iota(jnp.int32, (1, LANES), 1)
    cos, sin_a, sin_b = cos_ref[...], sa_ref[...], sb_ref[...]
    per = 2 if split else 1
    for h in range(n_q + n_k):
        is_q = h < n_q
        x = x_ref[:, h * LANES:(h + 1) * LANES].astype(F32)
        sq = x * x
        hi = sq.astype(BF16)
        lo = (sq - hi.astype(F32)).astype(BF16)
        ssum = (jnp.dot(hi, seg_ones, preferred_element_type=F32) + jnp.dot(lo, seg_ones, preferred_element_type=F32))
        gain = (gq_ref if is_q else gk_ref)[...]
        y = x * lax.rsqrt(ssum * (1.0 / seg) + NORM_EPS) * gain
        y = y * cos + pltpu.roll(y, LANES - shift, 1) * sin_a + pltpu.roll(y, shift, 1) * sin_b
        if is_q:
            y = y * q_scale
        if is_q and split:
            qk_ref[0, per * h] = jnp.where(lane < seg, y, 0.0).astype(qk_ref.dtype)
            qk_ref[0, per * h + 1] = jnp.where(lane >= seg, y, 0.0).astype(qk_ref.dtype)
        else:
            slot = per * h if is_q else per * n_q + (h - n_q)
            qk_ref[0, slot] = y.astype(qk_ref.dtype)
    for h in range(n_v):
        c0 = (n_q + n_k + h) * LANES
        vt_ref[0, h, 0, 0:LANES, :] = x_ref[:, c0:c0 + LANES].astype(F32).T.astype(vt_ref.dtype)
        vt_ref[0, h, 0, LANES:V_ROWS, :] = jnp.ones((V_ROWS - LANES, vt_ref.shape[-1]), vt_ref.dtype)


def qkv_prep(proj, tables, gq, gk, batch, seq, *, col_off, n_q, n_k, n_v, seg, shift, q_scale, split, tk, name):
    ts = tk
    n_s = seq // ts
    width = (n_q + n_k + n_v) * LANES
    slots = (2 if split else 1) * n_q + n_k
    tab = pl.BlockSpec((ts, LANES), lambda b, i: (i, 0))
    gain = pl.BlockSpec((1, LANES), lambda b, i: (0, 0))
    return pl.pallas_call(
        functools.partial(_qkv_prep_kernel, n_q=n_q, n_k=n_k, n_v=n_v, seg=seg, shift=shift, q_scale=q_scale,
                          split=split),
        grid=(batch, n_s),
        in_specs=[pl.BlockSpec((ts, width), lambda b, i: (b * n_s + i, col_off // width)), tab, tab, tab, gain, gain],
        out_specs=[pl.BlockSpec((1, slots, ts, LANES), lambda b, i: (b, 0, i, 0)),
                   pl.BlockSpec((1, n_v, 1, V_ROWS, tk), lambda b, i: (b, 0, i, 0, 0))],
        out_shape=[jax.ShapeDtypeStruct((batch, slots, seq, LANES), BF16),
                   jax.ShapeDtypeStruct((batch, n_v, n_s, V_ROWS, tk), BF16)],
        compiler_params=_cparams(("parallel", "parallel")),
        name=name,
    )(proj, tables[0], tables[1], tables[2], gq.reshape(1, LANES), gk.reshape(1, LANES))


def _flash_kernel(*refs, groups, tq, tk, seq, diff, lambda_init):
    if diff:
        q_ref, k_ref, vt_ref, lam_ref, sub_ref, o_ref, qt_ref, sa_ref, sb_ref, ma_ref, mb_ref, m_ref, acc_ref = refs
    else:
        q_ref, k_ref, vt_ref, o_ref, qt_ref, sa_ref, sb_ref, ma_ref, mb_ref, m_ref, acc_ref = refs
    rows = groups * tq
    n_c = seq // tk
    qt_ref[...] = q_ref[0].reshape(rows, LANES).astype(F32).T.astype(BF16)
    m_ref[...] = jnp.full(m_ref.shape, -jnp.inf, F32)
    acc_ref[...] = jnp.zeros(acc_ref.shape, F32)
    buf_a = (sa_ref, ma_ref)
    buf_b = (sb_ref, mb_ref)

    def scores_into(c, buf):
        ks = pl.ds(pl.multiple_of(c * tk, tk), tk)
        s_t = jnp.dot(k_ref[0, 0, ks, :], qt_ref[...], preferred_element_type=F32)
        buf[0][...] = s_t
        buf[1][...] = jnp.max(s_t, axis=0, keepdims=True)

    def step(c, cur, nxt):
        if nxt is not None:
            scores_into(c + 1, nxt)
        m_prev = m_ref[...]
        m_new = jnp.maximum(m_prev, cur[1][...])
        p_t = jnp.exp2(cur[0][...] - m_new)
        alpha = jnp.exp2(m_prev - m_new)
        acc_ref[...] = alpha * acc_ref[...] + jnp.dot(vt_ref[0, 0, c], p_t.astype(BF16),
                                                      preferred_element_type=F32)
        m_ref[...] = m_new

    scores_into(0, buf_a)
    bufs = (buf_a, buf_b)
    n_loop = (n_c - 1) // FLASH_UNROLL

    def body(i, carry):
        for u in range(FLASH_UNROLL):
            step(FLASH_UNROLL * i + u, bufs[u % 2], bufs[(u + 1) % 2])
        return carry

    lax.fori_loop(0, n_loop, body, 0)
    for c in range(FLASH_UNROLL * n_loop, n_c):
        step(c, bufs[c % 2], bufs[(c + 1) % 2] if c + 1 < n_c else None)
    o_t = acc_ref[0:LANES, :] / acc_ref[LANES:LANES + 1, :]
    if not diff:
        o = o_t.T
        for g in range(groups):
            o_ref[:, g * LANES:(g + 1) * LANES] = o[g * tq:(g + 1) * tq].astype(o_ref.dtype)
    else:
        lf = lam_ref[...]
        lam = (jnp.exp(jnp.sum(lf[0:1] * lf[1:2], axis=-1, keepdims=True))
               - jnp.exp(jnp.sum(lf[2:3] * lf[3:4], axis=-1, keepdims=True)) + lambda_init)
        d_t = o_t[:, :tq] - lam * o_t[:, tq:]
        ms = jnp.mean(d_t * d_t, axis=0, keepdims=True)
        y = (d_t * lax.rsqrt(ms + NORM_EPS)).T * sub_ref[...] * (1.0 - lambda_init)
        o_ref[...] = y.astype(o_ref.dtype)


def flash_attention(qk, vt, batch, seq, *, kv_heads, groups, k_slot0, k_stride, diff=False,
                    lambdas=None, subln=None, lambda_init=0.0, name):
    tk = vt.shape[-1]
    n_c = seq // tk
    tq = _tile(seq, 1024 // groups)
    n_q = seq // tq
    rows = groups * tq
    out_w = (1 if diff else groups) * LANES
    in_specs = [
        pl.BlockSpec((1, groups, tq, LANES), lambda b, h, i: (b, h, i, 0)),
        pl.BlockSpec((1, 1, seq, LANES), lambda b, h, i: (b, k_slot0 + k_stride * h, 0, 0)),
        pl.BlockSpec((1, 1, n_c, V_ROWS, tk), lambda b, h, i: (b, h, 0, 0, 0)),
    ]
    args = [qk, qk, vt]
    if diff:
        in_specs += [pl.BlockSpec(lambdas.shape, lambda b, h, i: (0, 0)),
                     pl.BlockSpec((1, LANES), lambda b, h, i: (0, 0))]
        args += [lambdas, subln.reshape(1, LANES)]
    return pl.pallas_call(
        functools.partial(_flash_kernel, groups=groups, tq=tq, tk=tk, seq=seq, diff=diff, lambda_init=lambda_init),
        grid=(batch, kv_heads, n_q),
        in_specs=in_specs,
        out_specs=pl.BlockSpec((tq, out_w), lambda b, h, i: (b * n_q + i, h)),
        out_shape=jax.ShapeDtypeStruct((batch * seq, kv_heads * out_w), BF16),
        scratch_shapes=[pltpu.VMEM((LANES, rows), BF16), pltpu.VMEM((tk, rows), F32), pltpu.VMEM((tk, rows), F32),
                        pltpu.VMEM((1, rows), F32), pltpu.VMEM((1, rows), F32),
                        pltpu.VMEM((1, rows), F32), pltpu.VMEM((V_ROWS, rows), F32)],
        compiler_params=_cparams(("parallel", "parallel", "arbitrary")),
        name=name,
    )(*args)


def _rope_angles(pos, dim, theta):
    inv = theta ** (-jnp.arange(0, dim, 2, dtype=F32) / dim)
    return pos[:, None] * inv[None, :]


def _axial_tables(seq):
    t = jnp.arange(seq)
    row = (t // GRID_W).astype(F32)
    col = (t % GRID_W).astype(F32)
    half = DB // 2
    ang_r = _rope_angles(row, half, AXIAL_THETA)
    ang_c = _rope_angles(col, half, AXIAL_THETA)
    ang = jnp.concatenate([ang_r, ang_r, ang_c, ang_c], axis=-1)
    cos, sin = jnp.cos(ang), jnp.sin(ang)
    first = (jnp.arange(LANES) % half) < (half // 2)
    return cos, jnp.where(first, -sin, 0.0), jnp.where(first, 0.0, sin)


def _partial_tables(seq):
    pos = jnp.arange(seq, dtype=F32)
    ang8 = _rope_angles(pos, ROT_C, ROPE_THETA)
    lane = jnp.arange(LANES) % DC
    ang = jnp.take(ang8, lane % (ROT_C // 2), axis=1)
    rot = lane < ROT_C
    first = lane < ROT_C // 2
    cos = jnp.where(rot, jnp.cos(ang), 1.0)
    sin = jnp.where(rot, jnp.sin(ang), 0.0)
    return cos, jnp.where(first, -sin, 0.0), jnp.where(first, 0.0, sin)


def _permute_w_in(w):
    k = w.shape[0]
    alpha0 = 4 * A_WIDTH
    beta0 = alpha0 + 2 * HA
    blocks = []
    for d in range(2):
        for g in range(HA // GDN_HEAD_GROUP):
            a0 = alpha0 + d * HA + g * GDN_HEAD_GROUP
            b0 = beta0 + d * HA + g * GDN_HEAD_GROUP
            blocks += [w[:, a0:a0 + GDN_HEAD_GROUP], w[:, b0:b0 + GDN_HEAD_GROUP],
                       jnp.zeros((k, LANES - 2 * GDN_HEAD_GROUP), w.dtype)]
    return jnp.concatenate([w[:, :alpha0]] + blocks + [w[:, beta0 + 2 * HA:]], axis=1)


def _gdn_param_row(p):
    blocks = []
    for d in range(2):
        for g in range(HA // GDN_HEAD_GROUP):
            blocks += [p[d, g * GDN_HEAD_GROUP:(g + 1) * GDN_HEAD_GROUP], jnp.zeros((LANES - GDN_HEAD_GROUP,), p.dtype)]
    return jnp.concatenate(blocks).reshape(1, AB_WIDTH)


def _encoder_layer(x, p_emb, w, li, batch, seq, tabs_b, tabs_c, tag):
    lambda_init = 0.8 - 0.6 * math.exp(-0.3 * li)
    proj = norm_matmul(x, w['norm_mix'], w['w_in'], f"in_proj_{tag}")

    q_a, k_a, v_a, gb = gdn_prep(proj, w['conv_a'], w['alog_row'], w['dtb_row'], batch, seq, f"gdn_prep_{tag}")
    o2 = gdn_scan(q_a, k_a, v_a, gb, batch, seq, f"gdn_scan_{tag}")
    y_a = gdn_post(o2, proj, w['norm_a'], f"gdn_post_{tag}")

    tk = _tile(seq, ATTN_KV_CHUNK)
    qk_b, vt_b = qkv_prep(proj, tabs_b, w['qn_b'], w['kn_b'], batch, seq, col_off=OFF_BQ, n_q=HB, n_k=KVB, n_v=KVB,
                          seg=DB, shift=DB // 4, q_scale=DB ** -0.5 * LOG2_E, split=False, tk=tk,
                          name=f"prep_b_{tag}")
    y_b = flash_attention(qk_b, vt_b, batch, seq, kv_heads=KVB, groups=HB // KVB, k_slot0=HB, k_stride=1,
                          name=f"attn_b_{tag}")

    qk_c, vt_c = qkv_prep(proj, tabs_c, w['qn_c2'], w['kn_c2'], batch, seq, col_off=OFF_CQ, n_q=HC, n_k=HC, n_v=HC,
                          seg=DC, shift=ROT_C // 2, q_scale=DC ** -0.5 * LOG2_E, split=True, tk=tk,
                          name=f"prep_c_{tag}")
    y_c = flash_attention(qk_c, vt_c, batch, seq, kv_heads=HC, groups=2, k_slot0=2 * HC, k_stride=1,
                          diff=True, lambdas=w['lambdas_c'], subln=w['subln_c'],
                          lambda_init=lambda_init, name=f"attn_c_{tag}")

    x = merge_out((y_a, y_b, y_c), proj, (w['w_o_a'], w['w_o_b'], w['w_o_c']), w['w_out'], x, f"merge_out_{tag}")

    u = norm_matmul(x, w['norm_ffn'], w['w_up'], f"ffn_up_{tag}")
    act = ffn_act(u, w['conv_ffn'], w['conv_ffn_b'], batch, seq, f"ffn_act_{tag}")
    return ffn_down_ple(act, w['w_down'], x, p_emb, w['w_ple_gate'], w['w_ple'], f"ffn_down_ple_{tag}")


def kernel(x_prompt, x_sample, p_prompt, p_sample, norm_mix, w_in, conv_a, a_log, dt_bias, norm_a, qn_b, kn_b, qn_c, kn_c, lambdas_c, subln_c, w_o_a, w_o_b, w_o_c, w_out, norm_ffn, w_up, conv_ffn, conv_ffn_b, w_down, w_ple, w_ple_gate):
    depth = w_in.shape[0]
    layers = []
    for i in range(depth):
        layers.append(dict(
            norm_mix=norm_mix[i], w_in=_permute_w_in(w_in[i]).astype(BF16), conv_a=conv_a[i],
            alog_row=_gdn_param_row(a_log[i]), dtb_row=_gdn_param_row(dt_bias[i]), norm_a=norm_a[i],
            qn_b=qn_b[i], kn_b=kn_b[i], qn_c2=jnp.tile(qn_c[i], 2), kn_c2=jnp.tile(kn_c[i], 2),
            lambdas_c=lambdas_c[i], subln_c=subln_c[i],
            w_o_a=w_o_a[i].astype(BF16), w_o_b=w_o_b[i].astype(BF16), w_o_c=w_o_c[i].astype(BF16),
            w_out=w_out[i].astype(BF16), norm_ffn=norm_ffn[i], w_up=w_up[i].astype(BF16),
            conv_ffn=conv_ffn[i], conv_ffn_b=conv_ffn_b[i], w_down=w_down[i].astype(BF16),
            w_ple=w_ple[i].astype(BF16), w_ple_gate=w_ple_gate[i].astype(BF16)))

    outs = []
    for tag, x, p in (("p", x_prompt, p_prompt), ("s", x_sample, p_sample)):
        batch, seq, d = x.shape
        tabs_b = _axial_tables(seq)
        tabs_c = _partial_tables(seq)
        h = x.reshape(batch * seq, d)
        for i in range(depth):
            h = _encoder_layer(h, p[i].reshape(batch * seq, -1), layers[i], i, batch, seq, tabs_b, tabs_c, f"{tag}{i}")
        outs.append(h.reshape(batch, seq, d))
    return tuple(outs)
```

```python
import functools
import math

import jax
import jax.numpy as jnp
from jax import lax
from jax.experimental import pallas as pl
from jax.experimental.pallas import tpu as pltpu

F32 = jnp.float32
BF16 = jnp.bfloat16
HIGHEST = lax.Precision.HIGHEST

D_MODEL = 2048
DEPTH = 2
GRID_W = 64
PLE_DIM = 256
NORM_EPS = 1e-6
HA = 8
DA = 128
A_WIDTH = HA * DA
DELTA_CHUNK = 64
HB = 8
KVB = 2
DB = 128
AXIAL_THETA = 10000.0
HC = 8
DC = 64
DVC = 2 * DC
ROT_C = DC // 4
ROPE_THETA = 500000.0
D_FF = 5632

LANES = 128
SUBLANES = 8
VMEM_LIMIT_BYTES = 56 * 1024 * 1024

OFF_QKV = 0
OFF_Z = 3072
OFF_AB = 4096
AB_WIDTH = 512
OFF_BQ = 4608
OFF_BV = 5888
OFF_CQ = 6144
OFF_CV = 8192
OFF_GATE = 9216
N_PROJ = 15360
GDN_HEAD_GROUP = 4
GDN_BLOCK = 512
ATTN_KV_CHUNK = 1024
ATTN_Q_ROWS = 2048
FLASH_UNROLL = 2
V_ROWS = LANES + 16
LOG2_E = math.log2(math.e)


def _cparams(semantics):
    return pltpu.CompilerParams(dimension_semantics=semantics, vmem_limit_bytes=VMEM_LIMIT_BYTES)


def _tile(n, pref):
    t = min(n, pref)
    while n % t:
        t //= 2
    return t


def _sigmoid(x):
    return 1.0 / (1.0 + jnp.exp(-x))


def _softplus(x):
    return jnp.maximum(x, 0.0) + jnp.log(1.0 + jnp.exp(-jnp.abs(x)))


def _norm_matmul_kernel(x_ref, g_ref, w_ref, o_ref, h_ref, *, rows):
    @pl.when(pl.program_id(1) == 0)
    def _():
        def chunk(r, c):
            rs = pl.ds(pl.multiple_of(r * rows, rows), rows)
            x = x_ref[rs, :]
            ms = jnp.mean(x * x, axis=-1, keepdims=True)
            h_ref[rs, :] = (x * lax.rsqrt(ms + NORM_EPS) * g_ref[...]).astype(BF16)
            return c
        lax.fori_loop(0, x_ref.shape[0] // rows, chunk, 0)

    o_ref[...] = jnp.dot(h_ref[...], w_ref[...], preferred_element_type=F32).astype(o_ref.dtype)


def norm_matmul(x, g, w, name):
    m, k = x.shape
    n = w.shape[1]
    tm = _tile(m, 1024)
    tn = 1536 if n % 1536 == 0 else _tile(n, 1024)
    rows = _tile(tm, 128)
    return pl.pallas_call(
        functools.partial(_norm_matmul_kernel, rows=rows),
        grid=(m // tm, n // tn),
        in_specs=[
            pl.BlockSpec((tm, k), lambda i, j: (i, 0)),
            pl.BlockSpec((1, k), lambda i, j: (0, 0)),
            pl.BlockSpec((k, tn), lambda i, j: (0, j)),
        ],
        out_specs=pl.BlockSpec((tm, tn), lambda i, j: (i, j)),
        out_shape=jax.ShapeDtypeStruct((m, n), BF16),
        scratch_shapes=[pltpu.VMEM((tm, k), BF16)],
        compiler_params=_cparams(("parallel", "arbitrary")),
        name=name,
    )(x, g.reshape(1, k), w)


def _resident(shape):
    return pl.BlockSpec(shape, lambda *_: (0,) * len(shape), pipeline_mode=pl.Buffered(1))


def _merge_out_kernel(*refs, n_branch, n_chunk, tc):
    y_refs = refs[:n_branch]
    g_refs = refs[n_branch:n_branch + n_branch * n_chunk]
    w_refs = refs[n_branch + n_branch * n_chunk:2 * n_branch + n_branch * n_chunk]
    wo_ref, x_ref, o_ref, m_ref = refs[2 * n_branch + n_branch * n_chunk:]
    for c in range(n_chunk):
        cols = slice(c * tc, (c + 1) * tc)
        acc = None
        for i in range(n_branch):
            gate = _sigmoid(g_refs[i * n_chunk + c][...].astype(F32))
            term = gate * jnp.dot(y_refs[i][...], w_refs[i][:, cols], preferred_element_type=F32)
            acc = term if acc is None else acc + term
        m_ref[:, cols] = acc.astype(BF16)
    o_ref[...] = x_ref[...] + jnp.dot(m_ref[...], wo_ref[...], preferred_element_type=F32)


def merge_out(ys, proj, w_os, w_out, x, name):
    m, k = ys[0].shape
    n = w_out.shape[1]
    tm = _tile(m, 256)
    tc = _tile(n, 1024)
    n_chunk = n // tc
    n_branch = len(ys)
    gate_blk = OFF_GATE // tc
    row = lambda i: (i, 0)
    g_specs = [pl.BlockSpec((tm, tc), lambda i, b=b: (i, gate_blk + b)) for b in range(n_branch * n_chunk)]
    return pl.pallas_call(
        functools.partial(_merge_out_kernel, n_branch=n_branch, n_chunk=n_chunk, tc=tc),
        grid=(m // tm,),
        in_specs=([pl.BlockSpec((tm, k), row)] * n_branch + g_specs + [_resident((k, n))] * n_branch
                  + [_resident((n, n)), pl.BlockSpec((tm, n), row)]),
        out_specs=pl.BlockSpec((tm, n), row),
        out_shape=jax.ShapeDtypeStruct((m, n), F32),
        scratch_shapes=[pltpu.VMEM((tm, n), BF16)],
        compiler_params=_cparams(("parallel",)),
        name=name,
    )(*ys, *([proj] * (n_branch * n_chunk)), *w_os, w_out, x)


def _down_ple_kernel(a_ref, wd_ref, x_ref, p_ref, wg_ref, we_ref, o_ref):
    x2 = x_ref[...] + jnp.dot(a_ref[...], wd_ref[...], preferred_element_type=F32)
    gate = _sigmoid(jnp.dot(x2.astype(BF16), wg_ref[...], preferred_element_type=F32))
    emb = jnp.dot(p_ref[...].astype(BF16), we_ref[...], preferred_element_type=F32)
    o_ref[...] = x2 + gate * emb


def ffn_down_ple(act, w_down, x, p, wg, we, name):
    m, k = act.shape
    n = w_down.shape[1]
    kp = p.shape[1]
    tm = _tile(m, 256)
    row = lambda i: (i, 0)
    return pl.pallas_call(
        _down_ple_kernel,
        grid=(m // tm,),
        in_specs=[pl.BlockSpec((tm, k), row), _resident((k, n)), pl.BlockSpec((tm, n), row),
                  pl.BlockSpec((tm, kp), row), _resident((n, n)), _resident((kp, n))],
        out_specs=pl.BlockSpec((tm, n), row),
        out_shape=jax.ShapeDtypeStruct((m, n), F32),
        compiler_params=_cparams(("parallel",)),
        name=name,
    )(act, w_down, x, p, wg, we)


def _shift_mats(ts):
    r = lax.broadcasted_iota(jnp.int32, (ts, ts), 0)
    c = lax.broadcasted_iota(jnp.int32, (ts, ts), 1)
    return (c == r - 1).astype(BF16), (c == r + 1).astype(BF16)


def _conv3(xb, prev_row, next_row, w_ref, lanes, shifts):
    ts = xb.shape[0]
    w0, w1, w2 = w_ref[0:1, lanes], w_ref[1:2, lanes], w_ref[2:3, lanes]
    xp = jnp.dot(shifts[0], xb, preferred_element_type=F32)
    xn = jnp.dot(shifts[1], xb, preferred_element_type=F32)
    y = xp * w0 + xb.astype(F32) * w1 + xn * w2
    r8 = lax.broadcasted_iota(jnp.int32, (SUBLANES, 1), 0)
    first = y[0:SUBLANES] + jnp.where(r8 == 0, prev_row, 0.0) * w0
    last = y[ts - SUBLANES:] + jnp.where(r8 == SUBLANES - 1, next_row, 0.0) * w2
    return jnp.concatenate([first, y[SUBLANES:ts - SUBLANES], last], axis=0)


def _halo_specs(batch, seq, ts, width, col_fn):
    n_s = seq // ts
    per_seq = seq // SUBLANES
    per_tile = ts // SUBLANES
    last = batch * per_seq - 1

    def prev_map(b, i, *rest):
        return (jnp.maximum(b * per_seq + i * per_tile - 1, 0), col_fn(*rest))

    def next_map(b, i, *rest):
        return (jnp.minimum(b * per_seq + (i + 1) * per_tile, last), col_fn(*rest))

    return pl.BlockSpec((SUBLANES, width), prev_map), pl.BlockSpec((SUBLANES, width), next_map), n_s


def _ffn_act_kernel(ug_ref, ugp_ref, ugn_ref, uv_ref, uvp_ref, uvn_ref, wg_ref, wv_ref, bg_ref, bv_ref, o_ref, *, n_s):
    i = pl.program_id(1)
    has_prev = (i > 0).astype(F32)
    has_next = (i < n_s - 1).astype(F32)
    all_lanes = slice(None)
    shifts = _shift_mats(ug_ref.shape[0])

    def conv(u_ref, p_ref, n_ref, w_ref, b_ref):
        prev_row = p_ref[SUBLANES - 1:SUBLANES, :].astype(F32) * has_prev
        next_row = n_ref[0:1, :].astype(F32) * has_next
        return _conv3(u_ref[...], prev_row, next_row, w_ref, all_lanes, shifts) + b_ref[...]

    gate = conv(ug_ref, ugp_ref, ugn_ref, wg_ref, bg_ref)
    val = conv(uv_ref, uvp_ref, uvn_ref, wv_ref, bv_ref)
    o_ref[...] = (gate * _sigmoid(gate) * val).astype(o_ref.dtype)


def ffn_act(u, conv_w, conv_b, batch, seq, name):
    t, two_f = u.shape
    f = two_f // 2
    ts = _tile(seq, 256)
    tc = f // 4 if f % (4 * LANES) == 0 else 512
    nj = f // tc
    prev_g, next_g, n_s = _halo_specs(batch, seq, ts, tc, lambda j: j)
    prev_v, next_v, _ = _halo_specs(batch, seq, ts, tc, lambda j: nj + j)
    main_g = pl.BlockSpec((ts, tc), lambda b, i, j: (b * n_s + i, j))
    main_v = pl.BlockSpec((ts, tc), lambda b, i, j: (b * n_s + i, nj + j))
    w_g = pl.BlockSpec((3, tc), lambda b, i, j: (0, j))
    w_v = pl.BlockSpec((3, tc), lambda b, i, j: (0, nj + j))
    b_g = pl.BlockSpec((1, tc), lambda b, i, j: (0, j))
    b_v = pl.BlockSpec((1, tc), lambda b, i, j: (0, nj + j))
    cb = conv_b.reshape(1, two_f)
    return pl.pallas_call(
        functools.partial(_ffn_act_kernel, n_s=n_s),
        grid=(batch, n_s, nj),
        in_specs=[main_g, prev_g, next_g, main_v, prev_v, next_v, w_g, w_v, b_g, b_v],
        out_specs=pl.BlockSpec((ts, tc), lambda b, i, j: (b * n_s + i, j)),
        out_shape=jax.ShapeDtypeStruct((t, f), BF16),
        compiler_params=_cparams(("parallel", "parallel", "arbitrary")),
        name=name,
    )(u, u, u, u, u, u, conv_w, conv_w, cb, cb)


def _gdn_prep_kernel(x_ref, xp_ref, xn_ref, ab_ref, w_ref, alog_ref, dtb_ref, q_ref, k_ref, v_ref, gb_ref, *, n_s):
    i = pl.program_id(1)
    has_prev = (i > 0).astype(F32)
    has_next = (i < n_s - 1).astype(F32)
    shifts = _shift_mats(x_ref.shape[0])
    for c in range(3 * HA):
        lanes = slice(c * DA, (c + 1) * DA)
        prev_row = xp_ref[SUBLANES - 1:SUBLANES, lanes].astype(F32) * has_prev
        next_row = xn_ref[0:1, lanes].astype(F32) * has_next
        y = _conv3(x_ref[:, lanes], prev_row, next_row, w_ref, lanes, shifts)
        y = y * _sigmoid(y)
        out_lanes = slice((c % HA) * DA, (c % HA + 1) * DA)
        if c < 2 * HA:
            y = y * lax.rsqrt(jnp.sum(y * y, axis=-1, keepdims=True) + NORM_EPS)
            (q_ref if c < HA else k_ref)[:, out_lanes] = y.astype(BF16)
        else:
            v_ref[:, out_lanes] = y.astype(BF16)
    ab = ab_ref[...].astype(F32)
    lane = lax.broadcasted_iota(jnp.int32, (1, AB_WIDTH), 1) % LANES
    g = -jnp.exp(alog_ref[...]) * _softplus(ab + dtb_ref[...])
    gb_ref[...] = jnp.where(lane < GDN_HEAD_GROUP, g, _sigmoid(ab))


def gdn_prep(proj, conv_w, alog_row, dtb_row, batch, seq, name):
    t = proj.shape[0]
    ts = _tile(seq, 256)
    w3 = 3 * A_WIDTH
    prev_s, next_s, n_s = _halo_specs(batch, seq, ts, w3, lambda: 0)
    row = lambda b, i: (b * n_s + i, 0)
    out_spec = pl.BlockSpec((ts, A_WIDTH), row)
    return pl.pallas_call(
        functools.partial(_gdn_prep_kernel, n_s=n_s),
        grid=(batch, n_s),
        in_specs=[
            pl.BlockSpec((ts, w3), row), prev_s, next_s,
            pl.BlockSpec((ts, AB_WIDTH), lambda b, i: (b * n_s + i, OFF_AB // AB_WIDTH)),
            pl.BlockSpec((3, w3), lambda b, i: (0, 0)),
            pl.BlockSpec((1, AB_WIDTH), lambda b, i: (0, 0)),
            pl.BlockSpec((1, AB_WIDTH), lambda b, i: (0, 0)),
        ],
        out_specs=[out_spec, out_spec, out_spec, pl.BlockSpec((ts, AB_WIDTH), row)],
        out_shape=[jax.ShapeDtypeStruct((t, A_WIDTH), BF16)] * 3 + [jax.ShapeDtypeStruct((t, AB_WIDTH), F32)],
        compiler_params=_cparams(("parallel", "parallel")),
        name=name,
    )(proj, proj, proj, proj, conv_w, alog_row, dtb_row)


def _gdn_scan_kernel(q_ref, k_ref, v_ref, gb_ref, o_ref, s_ref, *, n_chunks):
    c_len = DELTA_CHUNK
    d = pl.program_id(1)

    @pl.when(pl.program_id(3) == 0)
    def _():
        s_ref[...] = jnp.zeros_like(s_ref)

    sgn = 1 - 2 * d
    row = lax.broadcasted_iota(jnp.int32, (c_len, c_len), 0)
    col = lax.broadcasted_iota(jnp.int32, (c_len, c_len), 1)
    order = (row - col) * sgn
    incl = order >= 0
    strict = order > 0
    cum_mat = incl.astype(F32)
    eye = (row == col).astype(F32)
    sel = (lax.broadcasted_iota(jnp.int32, (SUBLANES, LANES), 0)
           == lax.broadcasted_iota(jnp.int32, (SUBLANES, LANES), 1)).astype(F32)
    scale = DA ** -0.5
    nt = (((1,), (1,)), ((), ()))
    tn = (((0,), (0,)), ((), ()))

    heads = range(GDN_HEAD_GROUP)
    chunk_rows, gcols, grows, exp_gs, exp_rests, exp_tots, gblks = [], [], [], [], [], [], []
    for j in range(n_chunks):
        c = j + d * (n_chunks - 1 - 2 * j)
        rows = pl.ds(pl.multiple_of(c * c_len, c_len), c_len)
        gblk = gb_ref[rows, :]
        gcol = jnp.dot(cum_mat, gblk, precision=HIGHEST, preferred_element_type=F32)
        chunk_rows.append(rows)
        gblks.append(gblk)
        gcols.append(gcol)
    for j in range(n_chunks):
        grows.append(lax.dot_general(sel, gcols[j], nt, precision=HIGHEST, preferred_element_type=F32))
        gtot = jnp.sum(gblks[j], axis=0, keepdims=True)
        exp_gs.append(jnp.exp(gcols[j]))
        exp_rests.append(jnp.exp(gtot - gcols[j]))
        exp_tots.append(jnp.exp(gtot))

    units = [(j, h) for j in range(n_chunks) for h in heads]
    low, intra, rhs = {}, {}, {}
    for (j, h) in units:
        rows = chunk_rows[j]
        lanes = slice(h * DA, (h + 1) * DA)
        kbf = k_ref[rows, lanes]
        kf = kbf.astype(F32)
        beta = gblks[j][:, GDN_HEAD_GROUP + h:GDN_HEAD_GROUP + h + 1]
        diff = gcols[j][:, h:h + 1] - grows[j][h:h + 1, :]
        decay = jnp.where(incl, jnp.exp(jnp.where(incl, diff, 0.0)), 0.0)
        kb = kf * beta
        kk = lax.dot_general(kb.astype(BF16), kbf, nt, preferred_element_type=F32)
        qs = (q_ref[rows, lanes].astype(F32) * scale).astype(BF16)
        qk = lax.dot_general(qs, kbf, nt, preferred_element_type=F32)
        low[j, h] = jnp.where(strict, kk * decay, 0.0)
        intra[j, h] = jnp.where(incl, qk * decay, 0.0).astype(BF16)
        rhs[j, h] = jnp.concatenate([v_ref[rows, lanes].astype(F32) * beta, kb * exp_gs[j][:, h:h + 1]],
                                    axis=1).astype(BF16)

    def square(p):
        return {u: jnp.dot(p[u].astype(BF16), p[u].astype(BF16), preferred_element_type=F32) for u in units}

    def pair(first, p_second):
        return {u: first[u] + jnp.dot(first[u].astype(BF16), p_second[u].astype(BF16), preferred_element_type=F32)
                for u in units}

    p2 = square(low)
    p4 = square(p2)
    fac_a = pair({u: eye - low[u] for u in units}, p2)
    p8 = square(p4)
    p16 = square(p8)
    fac_b = pair({u: eye + p4[u] for u in units}, p8)
    p32 = square(p16)
    fac_ab = {u: jnp.dot(fac_a[u].astype(BF16), fac_b[u].astype(BF16), preferred_element_type=F32) for u in units}
    fac_c = pair({u: eye + p16[u] for u in units}, p32)
    inv = {u: jnp.dot(fac_ab[u].astype(BF16), fac_c[u].astype(BF16), preferred_element_type=F32) for u in units}
    uw = {u: jnp.dot(inv[u].astype(BF16), rhs[u], preferred_element_type=F32).astype(BF16) for u in units}

    chain_lhs, out_local, state_add = {}, {}, {}
    for (j, h) in units:
        rows = chunk_rows[j]
        lanes = slice(h * DA, (h + 1) * DA)
        k_dec = (k_ref[rows, lanes].astype(F32) * exp_rests[j][:, h:h + 1]).astype(BF16)
        k_uw = lax.dot_general(k_dec, uw[j, h], tn, preferred_element_type=F32)
        i_uw = jnp.dot(intra[j, h], uw[j, h], preferred_element_type=F32)
        q_dec = q_ref[rows, lanes].astype(F32) * scale * exp_gs[j][:, h:h + 1]
        chain_lhs[j, h] = jnp.concatenate([q_dec - i_uw[:, DA:], k_uw[:, DA:]], axis=0).astype(BF16)
        out_local[j, h] = i_uw[:, :DA]
        state_add[j, h] = k_uw[:, :DA]

    for j in range(n_chunks):
        rows = chunk_rows[j]
        for h in heads:
            lanes = slice(h * DA, (h + 1) * DA)
            state = s_ref[h]
            prod = jnp.dot(chain_lhs[j, h], state.astype(BF16), preferred_element_type=F32)
            s_ref[h] = state * exp_tots[j][:, h:h + 1] - prod[c_len:] + state_add[j, h]
            o_ref[0, rows, lanes] = (prod[:c_len] + out_local[j, h]).astype(o_ref.dtype)


def gdn_scan(q, k, v, gb, batch, seq, name):
    t = q.shape[0]
    blk = _tile(seq, GDN_BLOCK)
    n_blk = seq // blk
    width = GDN_HEAD_GROUP * DA
    n_hg = HA // GDN_HEAD_GROUP

    def rblk(b, d, i):
        return b * n_blk + i + d * (n_blk - 1 - 2 * i)

    qkv_spec = pl.BlockSpec((blk, width), lambda b, d, g, i: (rblk(b, d, i), g))
    return pl.pallas_call(
        functools.partial(_gdn_scan_kernel, n_chunks=blk // DELTA_CHUNK),
        grid=(batch, 2, n_hg, n_blk),
        in_specs=[qkv_spec, qkv_spec, qkv_spec,
                  pl.BlockSpec((blk, LANES), lambda b, d, g, i: (rblk(b, d, i), d * n_hg + g))],
        out_specs=pl.BlockSpec((1, blk, width), lambda b, d, g, i: (d, rblk(b, d, i), g)),
        out_shape=jax.ShapeDtypeStruct((2, t, A_WIDTH), BF16),
        scratch_shapes=[pltpu.VMEM((GDN_HEAD_GROUP, DA, DA), F32)],
        compiler_params=_cparams(("parallel", "parallel", "parallel", "arbitrary")),
        name=name,
    )(q, k, v, gb)


def _gdn_post_kernel(o_ref, z_ref, g_ref, y_ref):
    for h in range(HA):
        lanes = slice(h * DA, (h + 1) * DA)
        o = o_ref[0, :, lanes].astype(F32) + o_ref[1, :, lanes].astype(F32)
        z = z_ref[:, lanes].astype(F32)
        ms = jnp.mean(o * o, axis=-1, keepdims=True)
        y = o * lax.rsqrt(ms + NORM_EPS) * g_ref[...]
        y_ref[:, lanes] = (y * (z * _sigmoid(z))).astype(y_ref.dtype)


def gdn_post(o2, proj, norm_w, name):
    t = proj.shape[0]
    ts = _tile(t, 512)
    return pl.pallas_call(
        _gdn_post_kernel,
        grid=(t // ts,),
        in_specs=[
            pl.BlockSpec((2, ts, A_WIDTH), lambda i: (0, i, 0)),
            pl.BlockSpec((ts, A_WIDTH), lambda i: (i, OFF_Z // A_WIDTH)),
            pl.BlockSpec((1, DA), lambda i: (0, 0)),
        ],
        out_specs=pl.BlockSpec((ts, A_WIDTH), lambda i: (i, 0)),
        out_shape=jax.ShapeDtypeStruct((t, A_WIDTH), BF16),
        compiler_params=_cparams(("parallel",)),
        name=name,
    )(o2, proj, norm_w.reshape(1, DA))


def _qkv_prep_kernel(x_ref, cos_ref, sa_ref, sb_ref, gq_ref, gk_ref, qk_ref, vt_ref, *, n_q, n_k, n_v, seg, shift,
                     q_scale, split):
    grp_r = lax.broadcasted_iota(jnp.int32, (LANES, LANES), 0) // seg
    grp_c = lax.broadcasted_iota(jnp.int32, (LANES, LANES), 1) // seg
    seg_ones = (grp_r == grp_c).astype(BF16)
    lane = lax.broadcasted_iota(jnp.int32, (1, LANES), 1)
    cos, sin_a, sin_b = cos_ref[...], sa_ref[...], sb_ref[...]
    per = 2 if split else 1
    for h in range(n_q + n_k):
        is_q = h < n_q
        x = x_ref[:, h * LANES:(h + 1) * LANES].astype(F32)
        sq = x * x
        hi = sq.astype(BF16)
        lo = (sq - hi.astype(F32)).astype(BF16)
        ssum = (jnp.dot(hi, seg_ones, preferred_element_type=F32) + jnp.dot(lo, seg_ones, preferred_element_type=F32))
        gain = (gq_ref if is_q else gk_ref)[...]
        y = x * lax.rsqrt(ssum * (1.0 / seg) + NORM_EPS) * gain
        y = y * cos + pltpu.roll(y, LANES - shift, 1) * sin_a + pltpu.roll(y, shift, 1) * sin_b
        if is_q:
            y = y * q_scale
        if is_q and split:
            qk_ref[0, per * h] = jnp.where(lane < seg, y, 0.0).astype(qk_ref.dtype)
            qk_ref[0, per * h + 1] = jnp.where(lane >= seg, y, 0.0).astype(qk_ref.dtype)
        else:
            slot = per * h if is_q else per * n_q + (h - n_q)
            qk_ref[0, slot] = y.astype(qk_ref.dtype)
    for h in range(n_v):
        c0 = (n_q + n_k + h) * LANES
        vt_ref[0, h, 0, 0:LANES, :] = x_ref[:, c0:c0 + LANES].astype(F32).T.astype(vt_ref.dtype)
        vt_ref[0, h, 0, LANES:V_ROWS, :] = jnp.ones((V_ROWS - LANES, vt_ref.shape[-1]), vt_ref.dtype)


def qkv_prep(proj, tables, gq, gk, batch, seq, *, col_off, n_q, n_k, n_v, seg, shift, q_scale, split, tk, name):
    ts = tk
    n_s = seq // ts
    width = (n_q + n_k + n_v) * LANES
    slots = (2 if split else 1) * n_q + n_k
    tab = pl.BlockSpec((ts, LANES), lambda b, i: (i, 0))
    gain = pl.BlockSpec((1, LANES), lambda b, i: (0, 0))
    return pl.pallas_call(
        functools.partial(_qkv_prep_kernel, n_q=n_q, n_k=n_k, n_v=n_v, seg=seg, shift=shift, q_scale=q_scale,
                          split=split),
        grid=(batch, n_s),
        in_specs=[pl.BlockSpec((ts, width), lambda b, i: (b * n_s + i, col_off // width)), tab, tab, tab, gain, gain],
        out_specs=[pl.BlockSpec((1, slots, ts, LANES), lambda b, i: (b, 0, i, 0)),
                   pl.BlockSpec((1, n_v, 1, V_ROWS, tk), lambda b, i: (b, 0, i, 0, 0))],
        out_shape=[jax.ShapeDtypeStruct((batch, slots, seq, LANES), BF16),
                   jax.ShapeDtypeStruct((batch, n_v, n_s, V_ROWS, tk), BF16)],
        compiler_params=_cparams(("parallel", "parallel")),
        name=name,
    )(proj, tables[0], tables[1], tables[2], gq.reshape(1, LANES), gk.reshape(1, LANES))


def _flash_kernel(*refs, groups, tq, tk, seq, diff, lambda_init):
    if diff:
        (q_ref, qn_ref, k_ref, vt_ref, lam_ref, sub_ref, o_ref,
         qt_ref, sa_ref, sb_ref, ma_ref, mb_ref, m_ref, acc_ref) = refs
    else:
        q_ref, qn_ref, k_ref, vt_ref, o_ref, qt_ref, sa_ref, sb_ref, ma_ref, mb_ref, m_ref, acc_ref = refs
    rows = groups * tq
    n_c = seq // tk
    carry_over = n_c % 2 == 0

    def load_q(ref):
        qt_ref[...] = ref[0].reshape(rows, LANES).astype(F32).T.astype(BF16)

    m_ref[...] = jnp.full(m_ref.shape, -jnp.inf, F32)
    acc_ref[...] = jnp.zeros(acc_ref.shape, F32)
    buf_a = (sa_ref, ma_ref)
    buf_b = (sb_ref, mb_ref)

    def scores_into(c, buf):
        ks = pl.ds(pl.multiple_of(c * tk, tk), tk)
        s_t = jnp.dot(k_ref[0, 0, ks, :], qt_ref[...], preferred_element_type=F32)
        buf[0][...] = s_t
        buf[1][...] = jnp.max(s_t, axis=0, keepdims=True)

    def step(c, cur, nxt, wrap=False):
        if wrap:
            load_q(qn_ref)
            scores_into(0, nxt)
        elif nxt is not None:
            scores_into(c + 1, nxt)
        m_prev = m_ref[...]
        m_new = jnp.maximum(m_prev, cur[1][...])
        p_t = jnp.exp2(cur[0][...] - m_new)
        alpha = jnp.exp2(m_prev - m_new)
        acc_ref[...] = alpha * acc_ref[...] + jnp.dot(vt_ref[0, 0, c], p_t.astype(BF16),
                                                      preferred_element_type=F32)
        m_ref[...] = m_new

    def cold_start():
        load_q(q_ref)
        scores_into(0, buf_a)

    if carry_over:
        pl.when(pl.program_id(2) == 0)(cold_start)
    else:
        cold_start()
    bufs = (buf_a, buf_b)
    n_loop = (n_c - 1) // FLASH_UNROLL

    def body(i, carry):
        for u in range(FLASH_UNROLL):
            step(FLASH_UNROLL * i + u, bufs[u % 2], bufs[(u + 1) % 2])
        return carry

    lax.fori_loop(0, n_loop, body, 0)
    for c in range(FLASH_UNROLL * n_loop, n_c):
        last = c + 1 == n_c
        step(c, bufs[c % 2], bufs[(c + 1) % 2] if (not last or carry_over) else None, wrap=last and carry_over)
    o_t = acc_ref[0:LANES, :] / acc_ref[LANES:LANES + 1, :]
    if not diff:
        o = o_t.T
        for g in range(groups):
            o_ref[:, g * LANES:(g + 1) * LANES] = o[g * tq:(g + 1) * tq].astype(o_ref.dtype)
    else:
        lf = lam_ref[...]
        lam = (jnp.exp(jnp.sum(lf[0:1] * lf[1:2], axis=-1, keepdims=True))
               - jnp.exp(jnp.sum(lf[2:3] * lf[3:4], axis=-1, keepdims=True)) + lambda_init)
        d_t = o_t[:, :tq] - lam * o_t[:, tq:]
        ms = jnp.mean(d_t * d_t, axis=0, keepdims=True)
        y = (d_t * lax.rsqrt(ms + NORM_EPS)).T * sub_ref[...] * (1.0 - lambda_init)
        o_ref[...] = y.astype(o_ref.dtype)


def flash_attention(qk, vt, batch, seq, *, kv_heads, groups, k_slot0, k_stride, diff=False,
                    lambdas=None, subln=None, lambda_init=0.0, name):
    tk = vt.shape[-1]
    n_c = seq // tk
    tq = _tile(seq, ATTN_Q_ROWS // groups)
    n_q = seq // tq
    rows = groups * tq
    out_w = (1 if diff else groups) * LANES
    in_specs = [
        pl.BlockSpec((1, groups, tq, LANES), lambda b, h, i: (b, h, i, 0)),
        pl.BlockSpec((1, groups, tq, LANES), lambda b, h, i: (b, h, jnp.minimum(i + 1, n_q - 1), 0)),
        pl.BlockSpec((1, 1, seq, LANES), lambda b, h, i: (b, k_slot0 + k_stride * h, 0, 0)),
        pl.BlockSpec((1, 1, n_c, V_ROWS, tk), lambda b, h, i: (b, h, 0, 0, 0)),
    ]
    args = [qk, qk, qk, vt]
    if diff:
        in_specs += [pl.BlockSpec(lambdas.shape, lambda b, h, i: (0, 0)),
                     pl.BlockSpec((1, LANES), lambda b, h, i: (0, 0))]
        args += [lambdas, subln.reshape(1, LANES)]
    return pl.pallas_call(
        functools.partial(_flash_kernel, groups=groups, tq=tq, tk=tk, seq=seq, diff=diff, lambda_init=lambda_init),
        grid=(batch, kv_heads, n_q),
        in_specs=in_specs,
        out_specs=pl.BlockSpec((tq, out_w), lambda b, h, i: (b * n_q + i, h)),
        out_shape=jax.ShapeDtypeStruct((batch * seq, kv_heads * out_w), BF16),
        scratch_shapes=[pltpu.VMEM((LANES, rows), BF16), pltpu.VMEM((tk, rows), F32), pltpu.VMEM((tk, rows), F32),
                        pltpu.VMEM((1, rows), F32), pltpu.VMEM((1, rows), F32),
                        pltpu.VMEM((1, rows), F32), pltpu.VMEM((V_ROWS, rows), F32)],
        compiler_params=_cparams(("parallel", "parallel", "arbitrary")),
        name=name,
    )(*args)


def _rope_angles(pos, dim, theta):
    inv = theta ** (-jnp.arange(0, dim, 2, dtype=F32) / dim)
    return pos[:, None] * inv[None, :]


def _axial_tables(seq):
    t = jnp.arange(seq)
    row = (t // GRID_W).astype(F32)
    col = (t % GRID_W).astype(F32)
    half = DB // 2
    ang_r = _rope_angles(row, half, AXIAL_THETA)
    ang_c = _rope_angles(col, half, AXIAL_THETA)
    ang = jnp.concatenate([ang_r, ang_r, ang_c, ang_c], axis=-1)
    cos, sin = jnp.cos(ang), jnp.sin(ang)
    first = (jnp.arange(LANES) % half) < (half // 2)
    return cos, jnp.where(first, -sin, 0.0), jnp.where(first, 0.0, sin)


def _partial_tables(seq):
    pos = jnp.arange(seq, dtype=F32)
    ang8 = _rope_angles(pos, ROT_C, ROPE_THETA)
    lane = jnp.arange(LANES) % DC
    ang = jnp.take(ang8, lane % (ROT_C // 2), axis=1)
    rot = lane < ROT_C
    first = lane < ROT_C // 2
    cos = jnp.where(rot, jnp.cos(ang), 1.0)
    sin = jnp.where(rot, jnp.sin(ang), 0.0)
    return cos, jnp.where(first, -sin, 0.0), jnp.where(first, 0.0, sin)


def _permute_w_in(w):
    k = w.shape[0]
    alpha0 = 4 * A_WIDTH
    beta0 = alpha0 + 2 * HA
    blocks = []
    for d in range(2):
        for g in range(HA // GDN_HEAD_GROUP):
            a0 = alpha0 + d * HA + g * GDN_HEAD_GROUP
            b0 = beta0 + d * HA + g * GDN_HEAD_GROUP
            blocks += [w[:, a0:a0 + GDN_HEAD_GROUP], w[:, b0:b0 + GDN_HEAD_GROUP],
                       jnp.zeros((k, LANES - 2 * GDN_HEAD_GROUP), w.dtype)]
    return jnp.concatenate([w[:, :alpha0]] + blocks + [w[:, beta0 + 2 * HA:]], axis=1)


def _gdn_param_row(p):
    blocks = []
    for d in range(2):
        for g in range(HA // GDN_HEAD_GROUP):
            blocks += [p[d, g * GDN_HEAD_GROUP:(g + 1) * GDN_HEAD_GROUP], jnp.zeros((LANES - GDN_HEAD_GROUP,), p.dtype)]
    return jnp.concatenate(blocks).reshape(1, AB_WIDTH)


def _encoder_layer(x, p_emb, w, li, batch, seq, tabs_b, tabs_c, tag):
    lambda_init = 0.8 - 0.6 * math.exp(-0.3 * li)
    proj = norm_matmul(x, w['norm_mix'], w['w_in'], f"in_proj_{tag}")

    q_a, k_a, v_a, gb = gdn_prep(proj, w['conv_a'], w['alog_row'], w['dtb_row'], batch, seq, f"gdn_prep_{tag}")
    o2 = gdn_scan(q_a, k_a, v_a, gb, batch, seq, f"gdn_scan_{tag}")
    y_a = gdn_post(o2, proj, w['norm_a'], f"gdn_post_{tag}")

    tk = _tile(seq, ATTN_KV_CHUNK)
    qk_b, vt_b = qkv_prep(proj, tabs_b, w['qn_b'], w['kn_b'], batch, seq, col_off=OFF_BQ, n_q=HB, n_k=KVB, n_v=KVB,
                          seg=DB, shift=DB // 4, q_scale=DB ** -0.5 * LOG2_E, split=False, tk=tk,
                          name=f"prep_b_{tag}")
    y_b = flash_attention(qk_b, vt_b, batch, seq, kv_heads=KVB, groups=HB // KVB, k_slot0=HB, k_stride=1,
                          name=f"attn_b_{tag}")

    qk_c, vt_c = qkv_prep(proj, tabs_c, w['qn_c2'], w['kn_c2'], batch, seq, col_off=OFF_CQ, n_q=HC, n_k=HC, n_v=HC,
                          seg=DC, shift=ROT_C // 2, q_scale=DC ** -0.5 * LOG2_E, split=True, tk=tk,
                          name=f"prep_c_{tag}")
    y_c = flash_attention(qk_c, vt_c, batch, seq, kv_heads=HC, groups=2, k_slot0=2 * HC, k_stride=1,
                          diff=True, lambdas=w['lambdas_c'], subln=w['subln_c'],
                          lambda_init=lambda_init, name=f"attn_c_{tag}")

    x = merge_out((y_a, y_b, y_c), proj, (w['w_o_a'], w['w_o_b'], w['w_o_c']), w['w_out'], x, f"merge_out_{tag}")

    u = norm_matmul(x, w['norm_ffn'], w['w_up'], f"ffn_up_{tag}")
    act = ffn_act(u, w['conv_ffn'], w['conv_ffn_b'], batch, seq, f"ffn_act_{tag}")
    return ffn_down_ple(act, w['w_down'], x, p_emb, w['w_ple_gate'], w['w_ple'], f"ffn_down_ple_{tag}")


def kernel(x_prompt, x_sample, p_prompt, p_sample, norm_mix, w_in, conv_a, a_log, dt_bias, norm_a, qn_b, kn_b, qn_c, kn_c, lambdas_c, subln_c, w_o_a, w_o_b, w_o_c, w_out, norm_ffn, w_up, conv_ffn, conv_ffn_b, w_down, w_ple, w_ple_gate):
    depth = w_in.shape[0]
    layers = []
    for i in range(depth):
        layers.append(dict(
            norm_mix=norm_mix[i], w_in=_permute_w_in(w_in[i]).astype(BF16), conv_a=conv_a[i],
            alog_row=_gdn_param_row(a_log[i]), dtb_row=_gdn_param_row(dt_bias[i]), norm_a=norm_a[i],
            qn_b=qn_b[i], kn_b=kn_b[i], qn_c2=jnp.tile(qn_c[i], 2), kn_c2=jnp.tile(kn_c[i], 2),
            lambdas_c=lambdas_c[i], subln_c=subln_c[i],
            w_o_a=w_o_a[i].astype(BF16), w_o_b=w_o_b[i].astype(BF16), w_o_c=w_o_c[i].astype(BF16),
            w_out=w_out[i].astype(BF16), norm_ffn=norm_ffn[i], w_up=w_up[i].astype(BF16),
            conv_ffn=conv_ffn[i], conv_ffn_b=conv_ffn_b[i], w_down=w_down[i].astype(BF16),
            w_ple=w_ple[i].astype(BF16), w_ple_gate=w_ple_gate[i].astype(BF16)))

    outs = []
    for tag, x, p in (("p", x_prompt, p_prompt), ("s", x_sample, p_sample)):
        batch, seq, d = x.shape
        tabs_b = _axial_tables(seq)
        tabs_c = _partial_tables(seq)
        h = x.reshape(batch * seq, d)
        for i in range(depth):
            h = _encoder_layer(h, p[i].reshape(batch * seq, -1), layers[i], i, batch, seq, tabs_b, tabs_c, f"{tag}{i}")
        outs.append(h.reshape(batch, seq, d))
    return tuple(outs)
```

```python
import functools
import math

import jax
import jax.numpy as jnp
from jax import lax
from jax.experimental import pallas as pl
from jax.experimental.pallas import tpu as pltpu

F32 = jnp.float32
BF16 = jnp.bfloat16
HIGHEST = lax.Precision.HIGHEST

D_MODEL = 2048
DEPTH = 2
GRID_W = 64
PLE_DIM = 256
NORM_EPS = 1e-6
HA = 8
DA = 128
A_WIDTH = HA * DA
DELTA_CHUNK = 64
HB = 8
KVB = 2
DB = 128
AXIAL_THETA = 10000.0
HC = 8
DC = 64
DVC = 2 * DC
ROT_C = DC // 4
ROPE_THETA = 500000.0
D_FF = 5632

LANES = 128
SUBLANES = 8
VMEM_LIMIT_BYTES = 56 * 1024 * 1024

OFF_QKV = 0
OFF_Z = 3072
OFF_AB = 4096
AB_WIDTH = 512
OFF_BQ = 4608
OFF_BV = 5888
OFF_CQ = 6144
OFF_CV = 8192
OFF_GATE = 9216
N_PROJ = 15360
GDN_HEAD_GROUP = 4
GDN_BLOCK = 512
ATTN_KV_CHUNK = 1024
ATTN_Q_ROWS = 2048
FLASH_UNROLL = 2
V_ROWS = LANES + 16
LOG2_E = math.log2(math.e)


def _cparams(semantics):
    return pltpu.CompilerParams(dimension_semantics=semantics, vmem_limit_bytes=VMEM_LIMIT_BYTES)


def _tile(n, pref):
    t = min(n, pref)
    while n % t:
        t //= 2
    return t


def _sigmoid(x):
    return 1.0 / (1.0 + jnp.exp(-x))


def _softplus(x):
    return jnp.maximum(x, 0.0) + jnp.log(1.0 + jnp.exp(-jnp.abs(x)))


def _norm_matmul_kernel(x_ref, g_ref, w_ref, o_ref, h_ref, *, rows):
    @pl.when(pl.program_id(1) == 0)
    def _():
        def chunk(r, c):
            rs = pl.ds(pl.multiple_of(r * rows, rows), rows)
            x = x_ref[rs, :]
            ms = jnp.mean(x * x, axis=-1, keepdims=True)
            h_ref[rs, :] = (x * lax.rsqrt(ms + NORM_EPS) * g_ref[...]).astype(BF16)
            return c
        lax.fori_loop(0, x_ref.shape[0] // rows, chunk, 0)

    o_ref[...] = jnp.dot(h_ref[...], w_ref[...], preferred_element_type=F32).astype(o_ref.dtype)


def norm_matmul(x, g, w, name):
    m, k = x.shape
    n = w.shape[1]
    tm = _tile(m, 1024)
    tn = 1536 if n % 1536 == 0 else _tile(n, 1024)
    rows = _tile(tm, 128)
    return pl.pallas_call(
        functools.partial(_norm_matmul_kernel, rows=rows),
        grid=(m // tm, n // tn),
        in_specs=[
            pl.BlockSpec((tm, k), lambda i, j: (i, 0)),
            pl.BlockSpec((1, k), lambda i, j: (0, 0)),
            pl.BlockSpec((k, tn), lambda i, j: (0, j)),
        ],
        out_specs=pl.BlockSpec((tm, tn), lambda i, j: (i, j)),
        out_shape=jax.ShapeDtypeStruct((m, n), BF16),
        scratch_shapes=[pltpu.VMEM((tm, k), BF16)],
        compiler_params=_cparams(("parallel", "arbitrary")),
        name=name,
    )(x, g.reshape(1, k), w)


def _resident(shape):
    return pl.BlockSpec(shape, lambda *_: (0,) * len(shape), pipeline_mode=pl.Buffered(1))


def _merge_out_kernel(*refs, n_branch, n_chunk, tc):
    y_refs = refs[:n_branch]
    g_refs = refs[n_branch:n_branch + n_branch * n_chunk]
    w_refs = refs[n_branch + n_branch * n_chunk:2 * n_branch + n_branch * n_chunk]
    wo_ref, x_ref, o_ref, m_ref = refs[2 * n_branch + n_branch * n_chunk:]
    for c in range(n_chunk):
        cols = slice(c * tc, (c + 1) * tc)
        acc = None
        for i in range(n_branch):
            gate = _sigmoid(g_refs[i * n_chunk + c][...].astype(F32))
            term = gate * jnp.dot(y_refs[i][...], w_refs[i][:, cols], preferred_element_type=F32)
            acc = term if acc is None else acc + term
        m_ref[:, cols] = acc.astype(BF16)
    o_ref[...] = x_ref[...] + jnp.dot(m_ref[...], wo_ref[...], preferred_element_type=F32)


def merge_out(ys, proj, w_os, w_out, x, name):
    m, k = ys[0].shape
    n = w_out.shape[1]
    tm = _tile(m, 256)
    tc = _tile(n, 1024)
    n_chunk = n // tc
    n_branch = len(ys)
    gate_blk = OFF_GATE // tc
    row = lambda i: (i, 0)
    g_specs = [pl.BlockSpec((tm, tc), lambda i, b=b: (i, gate_blk + b)) for b in range(n_branch * n_chunk)]
    return pl.pallas_call(
        functools.partial(_merge_out_kernel, n_branch=n_branch, n_chunk=n_chunk, tc=tc),
        grid=(m // tm,),
        in_specs=([pl.BlockSpec((tm, k), row)] * n_branch + g_specs + [_resident((k, n))] * n_branch
                  + [_resident((n, n)), pl.BlockSpec((tm, n), row)]),
        out_specs=pl.BlockSpec((tm, n), row),
        out_shape=jax.ShapeDtypeStruct((m, n), F32),
        scratch_shapes=[pltpu.VMEM((tm, n), BF16)],
        compiler_params=_cparams(("parallel",)),
        name=name,
    )(*ys, *([proj] * (n_branch * n_chunk)), *w_os, w_out, x)


def _down_ple_kernel(a_ref, wd_ref, x_ref, p_ref, wg_ref, we_ref, o_ref):
    x2 = x_ref[...] + jnp.dot(a_ref[...], wd_ref[...], preferred_element_type=F32)
    gate = _sigmoid(jnp.dot(x2.astype(BF16), wg_ref[...], preferred_element_type=F32))
    emb = jnp.dot(p_ref[...].astype(BF16), we_ref[...], preferred_element_type=F32)
    o_ref[...] = x2 + gate * emb


def ffn_down_ple(act, w_down, x, p, wg, we, name):
    m, k = act.shape
    n = w_down.shape[1]
    kp = p.shape[1]
    tm = _tile(m, 256)
    row = lambda i: (i, 0)
    return pl.pallas_call(
        _down_ple_kernel,
        grid=(m // tm,),
        in_specs=[pl.BlockSpec((tm, k), row), _resident((k, n)), pl.BlockSpec((tm, n), row),
                  pl.BlockSpec((tm, kp), row), _resident((n, n)), _resident((kp, n))],
        out_specs=pl.BlockSpec((tm, n), row),
        out_shape=jax.ShapeDtypeStruct((m, n), F32),
        compiler_params=_cparams(("parallel",)),
        name=name,
    )(act, w_down, x, p, wg, we)


def _shift_mats(ts):
    r = lax.broadcasted_iota(jnp.int32, (ts, ts), 0)
    c = lax.broadcasted_iota(jnp.int32, (ts, ts), 1)
    return (c == r - 1).astype(BF16), (c == r + 1).astype(BF16)


def _conv3(xb, prev_row, next_row, w_ref, lanes, shifts):
    ts = xb.shape[0]
    w0, w1, w2 = w_ref[0:1, lanes], w_ref[1:2, lanes], w_ref[2:3, lanes]
    xp = jnp.dot(shifts[0], xb, preferred_element_type=F32)
    xn = jnp.dot(shifts[1], xb, preferred_element_type=F32)
    y = xp * w0 + xb.astype(F32) * w1 + xn * w2
    r8 = lax.broadcasted_iota(jnp.int32, (SUBLANES, 1), 0)
    first = y[0:SUBLANES] + jnp.where(r8 == 0, prev_row, 0.0) * w0
    last = y[ts - SUBLANES:] + jnp.where(r8 == SUBLANES - 1, next_row, 0.0) * w2
    return jnp.concatenate([first, y[SUBLANES:ts - SUBLANES], last], axis=0)


def _halo_specs(batch, seq, ts, width, col_fn):
    n_s = seq // ts
    per_seq = seq // SUBLANES
    per_tile = ts // SUBLANES
    last = batch * per_seq - 1

    def prev_map(b, i, *rest):
        return (jnp.maximum(b * per_seq + i * per_tile - 1, 0), col_fn(*rest))

    def next_map(b, i, *rest):
        return (jnp.minimum(b * per_seq + (i + 1) * per_tile, last), col_fn(*rest))

    return pl.BlockSpec((SUBLANES, width), prev_map), pl.BlockSpec((SUBLANES, width), next_map), n_s


def _ffn_act_kernel(ug_ref, ugp_ref, ugn_ref, uv_ref, uvp_ref, uvn_ref, wg_ref, wv_ref, bg_ref, bv_ref, o_ref, *, n_s):
    i = pl.program_id(1)
    has_prev = (i > 0).astype(F32)
    has_next = (i < n_s - 1).astype(F32)
    all_lanes = slice(None)
    shifts = _shift_mats(ug_ref.shape[0])

    def conv(u_ref, p_ref, n_ref, w_ref, b_ref):
        prev_row = p_ref[SUBLANES - 1:SUBLANES, :].astype(F32) * has_prev
        next_row = n_ref[0:1, :].astype(F32) * has_next
        return _conv3(u_ref[...], prev_row, next_row, w_ref, all_lanes, shifts) + b_ref[...]

    gate = conv(ug_ref, ugp_ref, ugn_ref, wg_ref, bg_ref)
    val = conv(uv_ref, uvp_ref, uvn_ref, wv_ref, bv_ref)
    o_ref[...] = (gate * _sigmoid(gate) * val).astype(o_ref.dtype)


def ffn_act(u, conv_w, conv_b, batch, seq, name):
    t, two_f = u.shape
    f = two_f // 2
    ts = _tile(seq, 256)
    tc = f // 4 if f % (4 * LANES) == 0 else 512
    nj = f // tc
    prev_g, next_g, n_s = _halo_specs(batch, seq, ts, tc, lambda j: j)
    prev_v, next_v, _ = _halo_specs(batch, seq, ts, tc, lambda j: nj + j)
    main_g = pl.BlockSpec((ts, tc), lambda b, i, j: (b * n_s + i, j))
    main_v = pl.BlockSpec((ts, tc), lambda b, i, j: (b * n_s + i, nj + j))
    w_g = pl.BlockSpec((3, tc), lambda b, i, j: (0, j))
    w_v = pl.BlockSpec((3, tc), lambda b, i, j: (0, nj + j))
    b_g = pl.BlockSpec((1, tc), lambda b, i, j: (0, j))
    b_v = pl.BlockSpec((1, tc), lambda b, i, j: (0, nj + j))
    cb = conv_b.reshape(1, two_f)
    return pl.pallas_call(
        functools.partial(_ffn_act_kernel, n_s=n_s),
        grid=(batch, n_s, nj),
        in_specs=[main_g, prev_g, next_g, main_v, prev_v, next_v, w_g, w_v, b_g, b_v],
        out_specs=pl.BlockSpec((ts, tc), lambda b, i, j: (b * n_s + i, j)),
        out_shape=jax.ShapeDtypeStruct((t, f), BF16),
        compiler_params=_cparams(("parallel", "parallel", "arbitrary")),
        name=name,
    )(u, u, u, u, u, u, conv_w, conv_w, cb, cb)


def _conv3_roll(x, prev_row, next_row, w_ref, lanes):
    ts = x.shape[0]
    r = lax.broadcasted_iota(jnp.int32, (ts, 1), 0)
    xp = jnp.where(r == 0, prev_row, pltpu.roll(x, 1, 0))
    xn = jnp.where(r == ts - 1, next_row, pltpu.roll(x, ts - 1, 0))
    return xp * w_ref[0:1, lanes] + x * w_ref[1:2, lanes] + xn * w_ref[2:3, lanes]


def _ffn_tail_kernel(u_ref, up_ref, un_ref, cw_ref, cb_ref, wd_ref, x_ref, p_ref, wg_ref, we_ref, o_ref, *, n_s, f, tc):
    i = pl.program_id(1)
    has_prev = (i > 0).astype(F32)
    has_next = (i < n_s - 1).astype(F32)

    def conv(lanes):
        prev_row = up_ref[SUBLANES - 1:SUBLANES, lanes].astype(F32) * has_prev
        next_row = un_ref[0:1, lanes].astype(F32) * has_next
        return _conv3_roll(u_ref[:, lanes].astype(F32), prev_row, next_row, cw_ref, lanes) + cb_ref[:, lanes]

    x2 = x_ref[...]
    for c in range(f // tc):
        gate = conv(slice(c * tc, (c + 1) * tc))
        val = conv(slice(f + c * tc, f + (c + 1) * tc))
        act = (gate * _sigmoid(gate) * val).astype(BF16)
        x2 = x2 + jnp.dot(act, wd_ref[c * tc:(c + 1) * tc, :], preferred_element_type=F32)
    gate = _sigmoid(jnp.dot(x2.astype(BF16), wg_ref[...], preferred_element_type=F32))
    emb = jnp.dot(p_ref[...].astype(BF16), we_ref[...], preferred_element_type=F32)
    o_ref[...] = x2 + gate * emb


def ffn_tail(u, conv_w, conv_b, w_down, x, p, wg, we, batch, seq, name):
    t, two_f = u.shape
    f = two_f // 2
    n = w_down.shape[1]
    kp = p.shape[1]
    tm = _tile(seq, 128)
    prev_s, next_s, n_s = _halo_specs(batch, seq, tm, two_f, lambda: 0)
    row = lambda b, i: (b * n_s + i, 0)
    return pl.pallas_call(
        functools.partial(_ffn_tail_kernel, n_s=n_s, f=f, tc=512),
        grid=(batch, n_s),
        in_specs=[pl.BlockSpec((tm, two_f), row), prev_s, next_s, _resident((3, two_f)), _resident((1, two_f)),
                  _resident((f, n)), pl.BlockSpec((tm, n), row), pl.BlockSpec((tm, kp), row),
                  _resident((n, n)), _resident((kp, n))],
        out_specs=pl.BlockSpec((tm, n), row),
        out_shape=jax.ShapeDtypeStruct((t, n), F32),
        compiler_params=_cparams(("parallel", "parallel")),
        name=name,
    )(u, u, u, conv_w, conv_b.reshape(1, two_f), w_down, x, p, wg, we)


def _gdn_prep_kernel(x_ref, xp_ref, xn_ref, ab_ref, w_ref, alog_ref, dtb_ref, q_ref, k_ref, v_ref, gb_ref, *, n_s):
    i = pl.program_id(1)
    has_prev = (i > 0).astype(F32)
    has_next = (i < n_s - 1).astype(F32)
    shifts = _shift_mats(x_ref.shape[0])
    for c in range(3 * HA):
        lanes = slice(c * DA, (c + 1) * DA)
        prev_row = xp_ref[SUBLANES - 1:SUBLANES, lanes].astype(F32) * has_prev
        next_row = xn_ref[0:1, lanes].astype(F32) * has_next
        y = _conv3(x_ref[:, lanes], prev_row, next_row, w_ref, lanes, shifts)
        y = y * _sigmoid(y)
        out_lanes = slice((c % HA) * DA, (c % HA + 1) * DA)
        if c < 2 * HA:
            y = y * lax.rsqrt(jnp.sum(y * y, axis=-1, keepdims=True) + NORM_EPS)
            (q_ref if c < HA else k_ref)[:, out_lanes] = y.astype(BF16)
        else:
            v_ref[:, out_lanes] = y.astype(BF16)
    ab = ab_ref[...].astype(F32)
    lane = lax.broadcasted_iota(jnp.int32, (1, AB_WIDTH), 1) % LANES
    g = -jnp.exp(alog_ref[...]) * _softplus(ab + dtb_ref[...])
    gb_ref[...] = jnp.where(lane < GDN_HEAD_GROUP, g, _sigmoid(ab))


def gdn_prep(proj, conv_w, alog_row, dtb_row, batch, seq, name):
    t = proj.shape[0]
    ts = _tile(seq, 256)
    w3 = 3 * A_WIDTH
    prev_s, next_s, n_s = _halo_specs(batch, seq, ts, w3, lambda: 0)
    row = lambda b, i: (b * n_s + i, 0)
    out_spec = pl.BlockSpec((ts, A_WIDTH), row)
    return pl.pallas_call(
        functools.partial(_gdn_prep_kernel, n_s=n_s),
        grid=(batch, n_s),
        in_specs=[
            pl.BlockSpec((ts, w3), row), prev_s, next_s,
            pl.BlockSpec((ts, AB_WIDTH), lambda b, i: (b * n_s + i, OFF_AB // AB_WIDTH)),
            pl.BlockSpec((3, w3), lambda b, i: (0, 0)),
            pl.BlockSpec((1, AB_WIDTH), lambda b, i: (0, 0)),
            pl.BlockSpec((1, AB_WIDTH), lambda b, i: (0, 0)),
        ],
        out_specs=[out_spec, out_spec, out_spec, pl.BlockSpec((ts, AB_WIDTH), row)],
        out_shape=[jax.ShapeDtypeStruct((t, A_WIDTH), BF16)] * 3 + [jax.ShapeDtypeStruct((t, AB_WIDTH), F32)],
        compiler_params=_cparams(("parallel", "parallel")),
        name=name,
    )(proj, proj, proj, proj, conv_w, alog_row, dtb_row)


def _gdn_scan_kernel(q_ref, k_ref, v_ref, gb_ref, o_ref, s_ref, *, n_chunks):
    c_len = DELTA_CHUNK
    d = pl.program_id(1)

    @pl.when(pl.program_id(3) == 0)
    def _():
        s_ref[...] = jnp.zeros_like(s_ref)

    sgn = 1 - 2 * d
    row = lax.broadcasted_iota(jnp.int32, (c_len, c_len), 0)
    col = lax.broadcasted_iota(jnp.int32, (c_len, c_len), 1)
    order = (row - col) * sgn
    incl = order >= 0
    strict = order > 0
    cum_mat = incl.astype(F32)
    eye = (row == col).astype(F32)
    sel = (lax.broadcasted_iota(jnp.int32, (SUBLANES, LANES), 0)
           == lax.broadcasted_iota(jnp.int32, (SUBLANES, LANES), 1)).astype(F32)
    scale = DA ** -0.5
    nt = (((1,), (1,)), ((), ()))
    tn = (((0,), (0,)), ((), ()))

    heads = range(GDN_HEAD_GROUP)
    chunk_rows, gcols, grows, exp_gs, exp_rests, exp_tots, gblks = [], [], [], [], [], [], []
    for j in range(n_chunks):
        c = j + d * (n_chunks - 1 - 2 * j)
        rows = pl.ds(pl.multiple_of(c * c_len, c_len), c_len)
        gblk = gb_ref[rows, :]
        gcol = jnp.dot(cum_mat, gblk, precision=HIGHEST, preferred_element_type=F32)
        chunk_rows.append(rows)
        gblks.append(gblk)
        gcols.append(gcol)
    for j in range(n_chunks):
        grows.append(lax.dot_general(sel, gcols[j], nt, precision=HIGHEST, preferred_element_type=F32))
        gtot = jnp.sum(gblks[j], axis=0, keepdims=True)
        exp_gs.append(jnp.exp(gcols[j]))
        exp_rests.append(jnp.exp(gtot - gcols[j]))
        exp_tots.append(jnp.exp(gtot))

    units = [(j, h) for j in range(n_chunks) for h in heads]
    low, intra, rhs = {}, {}, {}
    for (j, h) in units:
        rows = chunk_rows[j]
        lanes = slice(h * DA, (h + 1) * DA)
        kbf = k_ref[rows, lanes]
        kf = kbf.astype(F32)
        beta = gblks[j][:, GDN_HEAD_GROUP + h:GDN_HEAD_GROUP + h + 1]
        diff = gcols[j][:, h:h + 1] - grows[j][h:h + 1, :]
        decay = jnp.where(incl, jnp.exp(jnp.where(incl, diff, 0.0)), 0.0)
        kb = kf * beta
        kk = lax.dot_general(kb.astype(BF16), kbf, nt, preferred_element_type=F32)
        qs = (q_ref[rows, lanes].astype(F32) * scale).astype(BF16)
        qk = lax.dot_general(qs, kbf, nt, preferred_element_type=F32)
        low[j, h] = jnp.where(strict, kk * decay, 0.0)
        intra[j, h] = jnp.where(incl, qk * decay, 0.0).astype(BF16)
        rhs[j, h] = jnp.concatenate([v_ref[rows, lanes].astype(F32) * beta, kb * exp_gs[j][:, h:h + 1]],
                                    axis=1).astype(BF16)

    def square(p):
        return {u: jnp.dot(p[u].astype(BF16), p[u].astype(BF16), preferred_element_type=F32) for u in units}

    def pair(first, p_second):
        return {u: first[u] + jnp.dot(first[u].astype(BF16), p_second[u].astype(BF16), preferred_element_type=F32)
                for u in units}

    p2 = square(low)
    p4 = square(p2)
    fac_a = pair({u: eye - low[u] for u in units}, p2)
    p8 = square(p4)
    p16 = square(p8)
    fac_b = pair({u: eye + p4[u] for u in units}, p8)
    p32 = square(p16)
    fac_ab = {u: jnp.dot(fac_a[u].astype(BF16), fac_b[u].astype(BF16), preferred_element_type=F32) for u in units}
    fac_c = pair({u: eye + p16[u] for u in units}, p32)
    inv = {u: jnp.dot(fac_ab[u].astype(BF16), fac_c[u].astype(BF16), preferred_element_type=F32) for u in units}
    uw = {u: jnp.dot(inv[u].astype(BF16), rhs[u], preferred_element_type=F32).astype(BF16) for u in units}

    chain_lhs, out_local, state_add = {}, {}, {}
    for (j, h) in units:
        rows = chunk_rows[j]
        lanes = slice(h * DA, (h + 1) * DA)
        k_dec = (k_ref[rows, lanes].astype(F32) * exp_rests[j][:, h:h + 1]).astype(BF16)
        k_uw = lax.dot_general(k_dec, uw[j, h], tn, preferred_element_type=F32)
        i_uw = jnp.dot(intra[j, h], uw[j, h], preferred_element_type=F32)
        q_dec = q_ref[rows, lanes].astype(F32) * scale * exp_gs[j][:, h:h + 1]
        chain_lhs[j, h] = jnp.concatenate([q_dec - i_uw[:, DA:], k_uw[:, DA:]], axis=0).astype(BF16)
        out_local[j, h] = i_uw[:, :DA]
        state_add[j, h] = k_uw[:, :DA]

    for j in range(n_chunks):
        rows = chunk_rows[j]
        for h in heads:
            lanes = slice(h * DA, (h + 1) * DA)
            state = s_ref[h]
            prod = jnp.dot(chain_lhs[j, h], state.astype(BF16), preferred_element_type=F32)
            s_ref[h] = state * exp_tots[j][:, h:h + 1] - prod[c_len:] + state_add[j, h]
            o_ref[0, rows, lanes] = (prod[:c_len] + out_local[j, h]).astype(o_ref.dtype)


def gdn_scan(q, k, v, gb, batch, seq, name):
    t = q.shape[0]
    blk = _tile(seq, GDN_BLOCK)
    n_blk = seq // blk
    width = GDN_HEAD_GROUP * DA
    n_hg = HA // GDN_HEAD_GROUP

    def rblk(b, d, i):
        return b * n_blk + i + d * (n_blk - 1 - 2 * i)

    qkv_spec = pl.BlockSpec((blk, width), lambda b, d, g, i: (rblk(b, d, i), g))
    return pl.pallas_call(
        functools.partial(_gdn_scan_kernel, n_chunks=blk // DELTA_CHUNK),
        grid=(batch, 2, n_hg, n_blk),
        in_specs=[qkv_spec, qkv_spec, qkv_spec,
                  pl.BlockSpec((blk, LANES), lambda b, d, g, i: (rblk(b, d, i), d * n_hg + g))],
        out_specs=pl.BlockSpec((1, blk, width), lambda b, d, g, i: (d, rblk(b, d, i), g)),
        out_shape=jax.ShapeDtypeStruct((2, t, A_WIDTH), BF16),
        scratch_shapes=[pltpu.VMEM((GDN_HEAD_GROUP, DA, DA), F32)],
        compiler_params=_cparams(("parallel", "parallel", "parallel", "arbitrary")),
        name=name,
    )(q, k, v, gb)


def _gdn_post_kernel(o_ref, z_ref, g_ref, y_ref):
    for h in range(HA):
        lanes = slice(h * DA, (h + 1) * DA)
        o = o_ref[0, :, lanes].astype(F32) + o_ref[1, :, lanes].astype(F32)
        z = z_ref[:, lanes].astype(F32)
        ms = jnp.mean(o * o, axis=-1, keepdims=True)
        y = o * lax.rsqrt(ms + NORM_EPS) * g_ref[...]
        y_ref[:, lanes] = (y * (z * _sigmoid(z))).astype(y_ref.dtype)


def gdn_post(o2, proj, norm_w, name):
    t = proj.shape[0]
    ts = _tile(t, 512)
    return pl.pallas_call(
        _gdn_post_kernel,
        grid=(t // ts,),
        in_specs=[
            pl.BlockSpec((2, ts, A_WIDTH), lambda i: (0, i, 0)),
            pl.BlockSpec((ts, A_WIDTH), lambda i: (i, OFF_Z // A_WIDTH)),
            pl.BlockSpec((1, DA), lambda i: (0, 0)),
        ],
        out_specs=pl.BlockSpec((ts, A_WIDTH), lambda i: (i, 0)),
        out_shape=jax.ShapeDtypeStruct((t, A_WIDTH), BF16),
        compiler_params=_cparams(("parallel",)),
        name=name,
    )(o2, proj, norm_w.reshape(1, DA))


def _qkv_prep_kernel(x_ref, cos_ref, sa_ref, sb_ref, gq_ref, gk_ref, qk_ref, vt_ref, *, n_q, n_k, n_v, seg, shift,
                     q_scale, split):
    grp_r = lax.broadcasted_iota(jnp.int32, (LANES, LANES), 0) // seg
    grp_c = lax.broadcasted_iota(jnp.int32, (LANES, LANES), 1) // seg
    seg_ones = (grp_r == grp_c).astype(BF16)
    lane = lax.broadcasted_iota(jnp.int32, (1, LANES), 1)
    cos, sin_a, sin_b = cos_ref[...], sa_ref[...], sb_ref[...]
    per = 2 if split else 1
    for h in range(n_q + n_k):
        is_q = h < n_q
        x = x_ref[:, h * LANES:(h + 1) * LANES].astype(F32)
        sq = x * x
        hi = sq.astype(BF16)
        lo = (sq - hi.astype(F32)).astype(BF16)
        ssum = (jnp.dot(hi, seg_ones, preferred_element_type=F32) + jnp.dot(lo, seg_ones, preferred_element_type=F32))
        gain = (gq_ref if is_q else gk_ref)[...]
        y = x * lax.rsqrt(ssum * (1.0 / seg) + NORM_EPS) * gain
        y = y * cos + pltpu.roll(y, LANES - shift, 1) * sin_a + pltpu.roll(y, shift, 1) * sin_b
        if is_q:
            y = y * q_scale
        if is_q and split:
            qk_ref[0, per * h] = jnp.where(lane < seg, y, 0.0).astype(qk_ref.dtype)
            qk_ref[0, per * h + 1] = jnp.where(lane >= seg, y, 0.0).astype(qk_ref.dtype)
        else:
            slot = per * h if is_q else per * n_q + (h - n_q)
            qk_ref[0, slot] = y.astype(qk_ref.dtype)
    for h in range(n_v):
        c0 = (n_q + n_k + h) * LANES
        vt_ref[0, h, 0, 0:LANES, :] = x_ref[:, c0:c0 + LANES].astype(F32).T.astype(vt_ref.dtype)
        vt_ref[0, h, 0, LANES:V_ROWS, :] = jnp.ones((V_ROWS - LANES, vt_ref.shape[-1]), vt_ref.dtype)


def qkv_prep(proj, tables, gq, gk, batch, seq, *, col_off, n_q, n_k, n_v, seg, shift, q_scale, split, tk, name):
    ts = tk
    n_s = seq // ts
    width = (n_q + n_k + n_v) * LANES
    slots = (2 if split else 1) * n_q + n_k
    tab = pl.BlockSpec((ts, LANES), lambda b, i: (i, 0))
    gain = pl.BlockSpec((1, LANES), lambda b, i: (0, 0))
    return pl.pallas_call(
        functools.partial(_qkv_prep_kernel, n_q=n_q, n_k=n_k, n_v=n_v, seg=seg, shift=shift, q_scale=q_scale,
                          split=split),
        grid=(batch, n_s),
        in_specs=[pl.BlockSpec((ts, width), lambda b, i: (b * n_s + i, col_off // width)), tab, tab, tab, gain, gain],
        out_specs=[pl.BlockSpec((1, slots, ts, LANES), lambda b, i: (b, 0, i, 0)),
                   pl.BlockSpec((1, n_v, 1, V_ROWS, tk), lambda b, i: (b, 0, i, 0, 0))],
        out_shape=[jax.ShapeDtypeStruct((batch, slots, seq, LANES), BF16),
                   jax.ShapeDtypeStruct((batch, n_v, n_s, V_ROWS, tk), BF16)],
        compiler_params=_cparams(("parallel", "parallel")),
        name=name,
    )(proj, tables[0], tables[1], tables[2], gq.reshape(1, LANES), gk.reshape(1, LANES))


def _flash_kernel(*refs, groups, tq, tk, seq, diff, lambda_init):
    if diff:
        (q_ref, qn_ref, k_ref, vt_ref, lam_ref, sub_ref, o_ref,
         qt_ref, sa_ref, sb_ref, ma_ref, mb_ref, m_ref, acc_ref) = refs
    else:
        q_ref, qn_ref, k_ref, vt_ref, o_ref, qt_ref, sa_ref, sb_ref, ma_ref, mb_ref, m_ref, acc_ref = refs
    rows = groups * tq
    n_c = seq // tk
    carry_over = n_c % 2 == 0

    def load_q(ref):
        qt_ref[...] = ref[0].reshape(rows, LANES).astype(F32).T.astype(BF16)

    m_ref[...] = jnp.full(m_ref.shape, -jnp.inf, F32)
    acc_ref[...] = jnp.zeros(acc_ref.shape, F32)
    buf_a = (sa_ref, ma_ref)
    buf_b = (sb_ref, mb_ref)

    def scores_into(c, buf):
        ks = pl.ds(pl.multiple_of(c * tk, tk), tk)
        s_t = jnp.dot(k_ref[0, 0, ks, :], qt_ref[...], preferred_element_type=F32)
        buf[0][...] = s_t
        buf[1][...] = jnp.max(s_t, axis=0, keepdims=True)

    def step(c, cur, nxt, wrap=False):
        if wrap:
            load_q(qn_ref)
            scores_into(0, nxt)
        elif nxt is not None:
            scores_into(c + 1, nxt)
        m_prev = m_ref[...]
        m_new = jnp.maximum(m_prev, cur[1][...])
        p_t = jnp.exp2(cur[0][...] - m_new)
        alpha = jnp.exp2(m_prev - m_new)
        acc_ref[...] = alpha * acc_ref[...] + jnp.dot(vt_ref[0, 0, c], p_t.astype(BF16),
                                                      preferred_element_type=F32)
        m_ref[...] = m_new

    def cold_start():
        load_q(q_ref)
        scores_into(0, buf_a)

    if carry_over:
        pl.when(pl.program_id(2) == 0)(cold_start)
    else:
        cold_start()
    bufs = (buf_a, buf_b)
    n_loop = (n_c - 1) // FLASH_UNROLL

    def body(i, carry):
        for u in range(FLASH_UNROLL):
            step(FLASH_UNROLL * i + u, bufs[u % 2], bufs[(u + 1) % 2])
        return carry

    lax.fori_loop(0, n_loop, body, 0)
    for c in range(FLASH_UNROLL * n_loop, n_c):
        last = c + 1 == n_c
        step(c, bufs[c % 2], bufs[(c + 1) % 2] if (not last or carry_over) else None, wrap=last and carry_over)
    o_t = acc_ref[0:LANES, :] / acc_ref[LANES:LANES + 1, :]
    if not diff:
        o = o_t.T
        for g in range(groups):
            o_ref[:, g * LANES:(g + 1) * LANES] = o[g * tq:(g + 1) * tq].astype(o_ref.dtype)
    else:
        lf = lam_ref[...]
        lam = (jnp.exp(jnp.sum(lf[0:1] * lf[1:2], axis=-1, keepdims=True))
               - jnp.exp(jnp.sum(lf[2:3] * lf[3:4], axis=-1, keepdims=True)) + lambda_init)
        d_t = o_t[:, :tq] - lam * o_t[:, tq:]
        ms = jnp.mean(d_t * d_t, axis=0, keepdims=True)
        y = (d_t * lax.rsqrt(ms + NORM_EPS)).T * sub_ref[...] * (1.0 - lambda_init)
        o_ref[...] = y.astype(o_ref.dtype)


def flash_attention(qk, vt, batch, seq, *, kv_heads, groups, k_slot0, k_stride, diff=False,
                    lambdas=None, subln=None, lambda_init=0.0, name):
    tk = vt.shape[-1]
    n_c = seq // tk
    tq = _tile(seq, ATTN_Q_ROWS // groups)
    n_q = seq // tq
    rows = groups * tq
    out_w = (1 if diff else groups) * LANES
    in_specs = [
        pl.BlockSpec((1, groups, tq, LANES), lambda b, h, i: (b, h, i, 0)),
        pl.BlockSpec((1, groups, tq, LANES), lambda b, h, i: (b, h, jnp.minimum(i + 1, n_q - 1), 0)),
        pl.BlockSpec((1, 1, seq, LANES), lambda b, h, i: (b, k_slot0 + k_stride * h, 0, 0)),
        pl.BlockSpec((1, 1, n_c, V_ROWS, tk), lambda b, h, i: (b, h, 0, 0, 0)),
    ]
    args = [qk, qk, qk, vt]
    if diff:
        in_specs += [pl.BlockSpec(lambdas.shape, lambda b, h, i: (0, 0)),
                     pl.BlockSpec((1, LANES), lambda b, h, i: (0, 0))]
        args += [lambdas, subln.reshape(1, LANES)]
    return pl.pallas_call(
        functools.partial(_flash_kernel, groups=groups, tq=tq, tk=tk, seq=seq, diff=diff, lambda_init=lambda_init),
        grid=(batch, kv_heads, n_q),
        in_specs=in_specs,
        out_specs=pl.BlockSpec((tq, out_w), lambda b, h, i: (b * n_q + i, h)),
        out_shape=jax.ShapeDtypeStruct((batch * seq, kv_heads * out_w), BF16),
        scratch_shapes=[pltpu.VMEM((LANES, rows), BF16), pltpu.VMEM((tk, rows), F32), pltpu.VMEM((tk, rows), F32),
                        pltpu.VMEM((1, rows), F32), pltpu.VMEM((1, rows), F32),
                        pltpu.VMEM((1, rows), F32), pltpu.VMEM((V_ROWS, rows), F32)],
        compiler_params=_cparams(("parallel", "parallel", "arbitrary")),
        name=name,
    )(*args)


def _rope_angles(pos, dim, theta):
    inv = theta ** (-jnp.arange(0, dim, 2, dtype=F32) / dim)
    return pos[:, None] * inv[None, :]


def _axial_tables(seq):
    t = jnp.arange(seq)
    row = (t // GRID_W).astype(F32)
    col = (t % GRID_W).astype(F32)
    half = DB // 2
    ang_r = _rope_angles(row, half, AXIAL_THETA)
    ang_c = _rope_angles(col, half, AXIAL_THETA)
    ang = jnp.concatenate([ang_r, ang_r, ang_c, ang_c], axis=-1)
    cos, sin = jnp.cos(ang), jnp.sin(ang)
    first = (jnp.arange(LANES) % half) < (half // 2)
    return cos, jnp.where(first, -sin, 0.0), jnp.where(first, 0.0, sin)


def _partial_tables(seq):
    pos = jnp.arange(seq, dtype=F32)
    ang8 = _rope_angles(pos, ROT_C, ROPE_THETA)
    lane = jnp.arange(LANES) % DC
    ang = jnp.take(ang8, lane % (ROT_C // 2), axis=1)
    rot = lane < ROT_C
    first = lane < ROT_C // 2
    cos = jnp.where(rot, jnp.cos(ang), 1.0)
    sin = jnp.where(rot, jnp.sin(ang), 0.0)
    return cos, jnp.where(first, -sin, 0.0), jnp.where(first, 0.0, sin)


def _permute_w_in(w):
    k = w.shape[0]
    alpha0 = 4 * A_WIDTH
    beta0 = alpha0 + 2 * HA
    blocks = []
    for d in range(2):
        for g in range(HA // GDN_HEAD_GROUP):
            a0 = alpha0 + d * HA + g * GDN_HEAD_GROUP
            b0 = beta0 + d * HA + g * GDN_HEAD_GROUP
            blocks += [w[:, a0:a0 + GDN_HEAD_GROUP], w[:, b0:b0 + GDN_HEAD_GROUP],
                       jnp.zeros((k, LANES - 2 * GDN_HEAD_GROUP), w.dtype)]
    return jnp.concatenate([w[:, :alpha0]] + blocks + [w[:, beta0 + 2 * HA:]], axis=1)


def _gdn_param_row(p):
    blocks = []
    for d in range(2):
        for g in range(HA // GDN_HEAD_GROUP):
            blocks += [p[d, g * GDN_HEAD_GROUP:(g + 1) * GDN_HEAD_GROUP], jnp.zeros((LANES - GDN_HEAD_GROUP,), p.dtype)]
    return jnp.concatenate(blocks).reshape(1, AB_WIDTH)


def _encoder_layer(x, p_emb, w, li, batch, seq, tabs_b, tabs_c, tag):
    lambda_init = 0.8 - 0.6 * math.exp(-0.3 * li)
    proj = norm_matmul(x, w['norm_mix'], w['w_in'], f"in_proj_{tag}")

    q_a, k_a, v_a, gb = gdn_prep(proj, w['conv_a'], w['alog_row'], w['dtb_row'], batch, seq, f"gdn_prep_{tag}")
    o2 = gdn_scan(q_a, k_a, v_a, gb, batch, seq, f"gdn_scan_{tag}")
    y_a = gdn_post(o2, proj, w['norm_a'], f"gdn_post_{tag}")

    tk = _tile(seq, ATTN_KV_CHUNK)
    qk_b, vt_b = qkv_prep(proj, tabs_b, w['qn_b'], w['kn_b'], batch, seq, col_off=OFF_BQ, n_q=HB, n_k=KVB, n_v=KVB,
                          seg=DB, shift=DB // 4, q_scale=DB ** -0.5 * LOG2_E, split=False, tk=tk,
                          name=f"prep_b_{tag}")
    y_b = flash_attention(qk_b, vt_b, batch, seq, kv_heads=KVB, groups=HB // KVB, k_slot0=HB, k_stride=1,
                          name=f"attn_b_{tag}")

    qk_c, vt_c = qkv_prep(proj, tabs_c, w['qn_c2'], w['kn_c2'], batch, seq, col_off=OFF_CQ, n_q=HC, n_k=HC, n_v=HC,
                          seg=DC, shift=ROT_C // 2, q_scale=DC ** -0.5 * LOG2_E, split=True, tk=tk,
                          name=f"prep_c_{tag}")
    y_c = flash_attention(qk_c, vt_c, batch, seq, kv_heads=HC, groups=2, k_slot0=2 * HC, k_stride=1,
                          diff=True, lambdas=w['lambdas_c'], subln=w['subln_c'],
                          lambda_init=lambda_init, name=f"attn_c_{tag}")

    x = merge_out((y_a, y_b, y_c), proj, (w['w_o_a'], w['w_o_b'], w['w_o_c']), w['w_out'], x, f"merge_out_{tag}")

    u = norm_matmul(x, w['norm_ffn'], w['w_up'], f"ffn_up_{tag}")
    return ffn_tail(u, w['conv_ffn'], w['conv_ffn_b'], w['w_down'], x, p_emb, w['w_ple_gate'], w['w_ple'],
                    batch, seq, f"ffn_tail_{tag}")


def kernel(x_prompt, x_sample, p_prompt, p_sample, norm_mix, w_in, conv_a, a_log, dt_bias, norm_a, qn_b, kn_b, qn_c, kn_c, lambdas_c, subln_c, w_o_a, w_o_b, w_o_c, w_out, norm_ffn, w_up, conv_ffn, conv_ffn_b, w_down, w_ple, w_ple_gate):
    depth = w_in.shape[0]
    layers = []
    for i in range(depth):
        layers.append(dict(
            norm_mix=norm_mix[i], w_in=_permute_w_in(w_in[i]).astype(BF16), conv_a=conv_a[i],
            alog_row=_gdn_param_row(a_log[i]), dtb_row=_gdn_param_row(dt_bias[i]), norm_a=norm_a[i],
            qn_b=qn_b[i], kn_b=kn_b[i], qn_c2=jnp.tile(qn_c[i], 2), kn_c2=jnp.tile(kn_c[i], 2),
            lambdas_c=lambdas_c[i], subln_c=subln_c[i],
            w_o_a=w_o_a[i].astype(BF16), w_o_b=w_o_b[i].astype(BF16), w_o_c=w_o_c[i].astype(BF16),
            w_out=w_out[i].astype(BF16), norm_ffn=norm_ffn[i], w_up=w_up[i].astype(BF16),
            conv_ffn=conv_ffn[i], conv_ffn_b=conv_ffn_b[i], w_down=w_down[i].astype(BF16),
            w_ple=w_ple[i].astype(BF16), w_ple_gate=w_ple_gate[i].astype(BF16)))

    outs = []
    for tag, x, p in (("p", x_prompt, p_prompt), ("s", x_sample, p_sample)):
        batch, seq, d = x.shape
        tabs_b = _axial_tables(seq)
        tabs_c = _partial_tables(seq)
        h = x.reshape(batch * seq, d)
        for i in range(depth):
            h = _encoder_layer(h, p[i].reshape(batch * seq, -1), layers[i], i, batch, seq, tabs_b, tabs_c, f"{tag}{i}")
        outs.append(h.reshape(batch, seq, d))
    return tuple(outs)
```

```python
import functools
import math

import jax
import jax.numpy as jnp
from jax import lax
from jax.experimental import pallas as pl
from jax.experimental.pallas import tpu as pltpu

F32 = jnp.float32
BF16 = jnp.bfloat16
HIGHEST = lax.Precision.HIGHEST

D_MODEL = 2048
DEPTH = 2
GRID_W = 64
PLE_DIM = 256
NORM_EPS = 1e-6
HA = 8
DA = 128
A_WIDTH = HA * DA
DELTA_CHUNK = 64
HB = 8
KVB = 2
DB = 128
AXIAL_THETA = 10000.0
HC = 8
DC = 64
DVC = 2 * DC
ROT_C = DC // 4
ROPE_THETA = 500000.0
D_FF = 5632

LANES = 128
SUBLANES = 8
VMEM_LIMIT_BYTES = 56 * 1024 * 1024

OFF_QKV = 0
OFF_Z = 3072
OFF_AB = 4096
AB_WIDTH = 512
OFF_BQ = 4608
OFF_BV = 5888
OFF_CQ = 6144
OFF_CV = 8192
OFF_GATE = 9216
N_PROJ = 15360
GDN_HEAD_GROUP = 4
GDN_BLOCK = 512
ATTN_KV_CHUNK = 1024
ATTN_Q_ROWS = 2048
FLASH_UNROLL = 2
V_ROWS = LANES + 16
LOG2_E = math.log2(math.e)


def _cparams(semantics):
    return pltpu.CompilerParams(dimension_semantics=semantics, vmem_limit_bytes=VMEM_LIMIT_BYTES)


def _tile(n, pref):
    t = min(n, pref)
    while n % t:
        t //= 2
    return t


def _sigmoid(x):
    return 1.0 / (1.0 + jnp.exp(-x))


def _softplus(x):
    return jnp.maximum(x, 0.0) + jnp.log(1.0 + jnp.exp(-jnp.abs(x)))


def _norm_matmul_kernel(x_ref, g_ref, w_ref, o_ref, h_ref, *, rows):
    @pl.when(pl.program_id(1) == 0)
    def _():
        def chunk(r, c):
            rs = pl.ds(pl.multiple_of(r * rows, rows), rows)
            x = x_ref[rs, :]
            ms = jnp.mean(x * x, axis=-1, keepdims=True)
            h_ref[rs, :] = (x * lax.rsqrt(ms + NORM_EPS) * g_ref[...]).astype(BF16)
            return c
        lax.fori_loop(0, x_ref.shape[0] // rows, chunk, 0)

    o_ref[...] = jnp.dot(h_ref[...], w_ref[...], preferred_element_type=F32).astype(o_ref.dtype)


def norm_matmul(x, g, w, name):
    m, k = x.shape
    n = w.shape[1]
    tm = _tile(m, 1024)
    tn = 1536 if n % 1536 == 0 else _tile(n, 1024)
    rows = _tile(tm, 128)
    return pl.pallas_call(
        functools.partial(_norm_matmul_kernel, rows=rows),
        grid=(m // tm, n // tn),
        in_specs=[
            pl.BlockSpec((tm, k), lambda i, j: (i, 0)),
            pl.BlockSpec((1, k), lambda i, j: (0, 0)),
            pl.BlockSpec((k, tn), lambda i, j: (0, j)),
        ],
        out_specs=pl.BlockSpec((tm, tn), lambda i, j: (i, j)),
        out_shape=jax.ShapeDtypeStruct((m, n), BF16),
        scratch_shapes=[pltpu.VMEM((tm, k), BF16)],
        compiler_params=_cparams(("parallel", "arbitrary")),
        name=name,
    )(x, g.reshape(1, k), w)


def _resident(shape):
    return pl.BlockSpec(shape, lambda *_: (0,) * len(shape), pipeline_mode=pl.Buffered(1))


def _merge_out_kernel(*refs, n_branch, n_chunk, tc):
    y_refs = refs[:n_branch]
    g_refs = refs[n_branch:n_branch + n_branch * n_chunk]
    w_refs = refs[n_branch + n_branch * n_chunk:2 * n_branch + n_branch * n_chunk]
    wo_ref, x_ref, gn_ref, o_ref, h_ref, m_ref = refs[2 * n_branch + n_branch * n_chunk:]
    for c in range(n_chunk):
        cols = slice(c * tc, (c + 1) * tc)
        acc = None
        for i in range(n_branch):
            gate = _sigmoid(g_refs[i * n_chunk + c][...].astype(F32))
            term = gate * jnp.dot(y_refs[i][...], w_refs[i][:, cols], preferred_element_type=F32)
            acc = term if acc is None else acc + term
        m_ref[:, cols] = acc.astype(BF16)
    x1 = x_ref[...] + jnp.dot(m_ref[...], wo_ref[...], preferred_element_type=F32)
    o_ref[...] = x1
    ms = jnp.mean(x1 * x1, axis=-1, keepdims=True)
    h_ref[...] = (x1 * lax.rsqrt(ms + NORM_EPS) * gn_ref[...]).astype(h_ref.dtype)


def merge_out(ys, proj, w_os, w_out, x, g_next, name):
    m, k = ys[0].shape
    n = w_out.shape[1]
    tm = _tile(m, 256)
    tc = _tile(n, 1024)
    n_chunk = n // tc
    n_branch = len(ys)
    gate_blk = OFF_GATE // tc
    row = lambda i: (i, 0)
    g_specs = [pl.BlockSpec((tm, tc), lambda i, b=b: (i, gate_blk + b)) for b in range(n_branch * n_chunk)]
    return pl.pallas_call(
        functools.partial(_merge_out_kernel, n_branch=n_branch, n_chunk=n_chunk, tc=tc),
        grid=(m // tm,),
        in_specs=([pl.BlockSpec((tm, k), row)] * n_branch + g_specs + [_resident((k, n))] * n_branch
                  + [_resident((n, n)), pl.BlockSpec((tm, n), row), _resident((1, n))]),
        out_specs=[pl.BlockSpec((tm, n), row), pl.BlockSpec((tm, n), row)],
        out_shape=[jax.ShapeDtypeStruct((m, n), F32), jax.ShapeDtypeStruct((m, n), BF16)],
        scratch_shapes=[pltpu.VMEM((tm, n), BF16)],
        compiler_params=_cparams(("parallel",)),
        name=name,
    )(*ys, *([proj] * (n_branch * n_chunk)), *w_os, w_out, x, g_next.reshape(1, n))


def _matmul_kernel(a_ref, w_ref, o_ref):
    o_ref[...] = jnp.dot(a_ref[...], w_ref[...], preferred_element_type=F32).astype(o_ref.dtype)


def matmul_bf16(a, w, name):
    m, k = a.shape
    n = w.shape[1]
    tm = _tile(m, 1024)
    tn = _tile(n, 1024)
    return pl.pallas_call(
        _matmul_kernel,
        grid=(m // tm, n // tn),
        in_specs=[pl.BlockSpec((tm, k), lambda i, j: (i, 0)), pl.BlockSpec((k, tn), lambda i, j: (0, j))],
        out_specs=pl.BlockSpec((tm, tn), lambda i, j: (i, j)),
        out_shape=jax.ShapeDtypeStruct((m, n), BF16),
        compiler_params=_cparams(("parallel", "arbitrary")),
        name=name,
    )(a, w)


def _shift_mats(ts):
    r = lax.broadcasted_iota(jnp.int32, (ts, ts), 0)
    c = lax.broadcasted_iota(jnp.int32, (ts, ts), 1)
    return (c == r - 1).astype(BF16), (c == r + 1).astype(BF16)


def _conv3(xb, prev_row, next_row, w_ref, lanes, shifts):
    ts = xb.shape[0]
    w0, w1, w2 = w_ref[0:1, lanes], w_ref[1:2, lanes], w_ref[2:3, lanes]
    xp = jnp.dot(shifts[0], xb, preferred_element_type=F32)
    xn = jnp.dot(shifts[1], xb, preferred_element_type=F32)
    y = xp * w0 + xb.astype(F32) * w1 + xn * w2
    r8 = lax.broadcasted_iota(jnp.int32, (SUBLANES, 1), 0)
    first = y[0:SUBLANES] + jnp.where(r8 == 0, prev_row, 0.0) * w0
    last = y[ts - SUBLANES:] + jnp.where(r8 == SUBLANES - 1, next_row, 0.0) * w2
    return jnp.concatenate([first, y[SUBLANES:ts - SUBLANES], last], axis=0)


def _halo_specs(batch, seq, ts, width, col_fn):
    n_s = seq // ts
    per_seq = seq // SUBLANES
    per_tile = ts // SUBLANES
    last = batch * per_seq - 1

    def prev_map(b, i, *rest):
        return (jnp.maximum(b * per_seq + i * per_tile - 1, 0), col_fn(*rest))

    def next_map(b, i, *rest):
        return (jnp.minimum(b * per_seq + (i + 1) * per_tile, last), col_fn(*rest))

    return pl.BlockSpec((SUBLANES, width), prev_map), pl.BlockSpec((SUBLANES, width), next_map), n_s


def _conv3_roll(x, prev_row, next_row, w_ref, lanes):
    ts = x.shape[0]
    r = lax.broadcasted_iota(jnp.int32, (ts, 1), 0)
    xp = jnp.where(r == 0, prev_row, pltpu.roll(x, 1, 0))
    xn = jnp.where(r == ts - 1, next_row, pltpu.roll(x, ts - 1, 0))
    return xp * w_ref[0:1, lanes] + x * w_ref[1:2, lanes] + xn * w_ref[2:3, lanes]


def _ffn_tail_kernel(u_ref, up_ref, un_ref, cw_ref, cb_ref, wd_ref, x_ref, p_ref, wg_ref, we_ref, o_ref, *, n_s, f, tc):
    i = pl.program_id(1)
    has_prev = (i > 0).astype(F32)
    has_next = (i < n_s - 1).astype(F32)

    def conv(lanes):
        prev_row = up_ref[SUBLANES - 1:SUBLANES, lanes].astype(F32) * has_prev
        next_row = un_ref[0:1, lanes].astype(F32) * has_next
        return _conv3_roll(u_ref[:, lanes].astype(F32), prev_row, next_row, cw_ref, lanes) + cb_ref[:, lanes]

    x2 = x_ref[...]
    for c in range(f // tc):
        gate = conv(slice(c * tc, (c + 1) * tc))
        val = conv(slice(f + c * tc, f + (c + 1) * tc))
        act = (gate * _sigmoid(gate) * val).astype(BF16)
        x2 = x2 + jnp.dot(act, wd_ref[c * tc:(c + 1) * tc, :], preferred_element_type=F32)
    gate = _sigmoid(jnp.dot(x2.astype(BF16), wg_ref[...], preferred_element_type=F32))
    emb = jnp.dot(p_ref[...].astype(BF16), we_ref[...], preferred_element_type=F32)
    o_ref[...] = x2 + gate * emb


def ffn_tail(u, conv_w, conv_b, w_down, x, p, wg, we, batch, seq, name):
    t, two_f = u.shape
    f = two_f // 2
    n = w_down.shape[1]
    kp = p.shape[1]
    tm = _tile(seq, 128)
    prev_s, next_s, n_s = _halo_specs(batch, seq, tm, two_f, lambda: 0)
    row = lambda b, i: (b * n_s + i, 0)
    return pl.pallas_call(
        functools.partial(_ffn_tail_kernel, n_s=n_s, f=f, tc=512),
        grid=(batch, n_s),
        in_specs=[pl.BlockSpec((tm, two_f), row), prev_s, next_s, _resident((3, two_f)), _resident((1, two_f)),
                  _resident((f, n)), pl.BlockSpec((tm, n), row), pl.BlockSpec((tm, kp), row),
                  _resident((n, n)), _resident((kp, n))],
        out_specs=pl.BlockSpec((tm, n), row),
        out_shape=jax.ShapeDtypeStruct((t, n), F32),
        compiler_params=_cparams(("parallel", "parallel")),
        name=name,
    )(u, u, u, conv_w, conv_b.reshape(1, two_f), w_down, x, p, wg, we)


def _gdn_prep_kernel(x_ref, xp_ref, xn_ref, ab_ref, w_ref, alog_ref, dtb_ref, q_ref, k_ref, v_ref, gb_ref, *, n_s):
    i = pl.program_id(1)
    has_prev = (i > 0).astype(F32)
    has_next = (i < n_s - 1).astype(F32)
    shifts = _shift_mats(x_ref.shape[0])
    for c in range(3 * HA):
        lanes = slice(c * DA, (c + 1) * DA)
        prev_row = xp_ref[SUBLANES - 1:SUBLANES, lanes].astype(F32) * has_prev
        next_row = xn_ref[0:1, lanes].astype(F32) * has_next
        y = _conv3(x_ref[:, lanes], prev_row, next_row, w_ref, lanes, shifts)
        y = y * _sigmoid(y)
        out_lanes = slice((c % HA) * DA, (c % HA + 1) * DA)
        if c < 2 * HA:
            y = y * lax.rsqrt(jnp.sum(y * y, axis=-1, keepdims=True) + NORM_EPS)
            (q_ref if c < HA else k_ref)[:, out_lanes] = y.astype(BF16)
        else:
            v_ref[:, out_lanes] = y.astype(BF16)
    ab = ab_ref[...].astype(F32)
    lane = lax.broadcasted_iota(jnp.int32, (1, AB_WIDTH), 1) % LANES
    g = -jnp.exp(alog_ref[...]) * _softplus(ab + dtb_ref[...])
    gb_ref[...] = jnp.where(lane < GDN_HEAD_GROUP, g, _sigmoid(ab))


def gdn_prep(proj, conv_w, alog_row, dtb_row, batch, seq, name):
    t = proj.shape[0]
    ts = _tile(seq, 256)
    w3 = 3 * A_WIDTH
    prev_s, next_s, n_s = _halo_specs(batch, seq, ts, w3, lambda: 0)
    row = lambda b, i: (b * n_s + i, 0)
    out_spec = pl.BlockSpec((ts, A_WIDTH), row)
    return pl.pallas_call(
        functools.partial(_gdn_prep_kernel, n_s=n_s),
        grid=(batch, n_s),
        in_specs=[
            pl.BlockSpec((ts, w3), row), prev_s, next_s,
            pl.BlockSpec((ts, AB_WIDTH), lambda b, i: (b * n_s + i, OFF_AB // AB_WIDTH)),
            pl.BlockSpec((3, w3), lambda b, i: (0, 0)),
            pl.BlockSpec((1, AB_WIDTH), lambda b, i: (0, 0)),
            pl.BlockSpec((1, AB_WIDTH), lambda b, i: (0, 0)),
        ],
        out_specs=[out_spec, out_spec, out_spec, pl.BlockSpec((ts, AB_WIDTH), row)],
        out_shape=[jax.ShapeDtypeStruct((t, A_WIDTH), BF16)] * 3 + [jax.ShapeDtypeStruct((t, AB_WIDTH), F32)],
        compiler_params=_cparams(("parallel", "parallel")),
        name=name,
    )(proj, proj, proj, proj, conv_w, alog_row, dtb_row)


def _gdn_scan_kernel(q_ref, k_ref, v_ref, gb_ref, o_ref, s_ref, *, n_chunks):
    c_len = DELTA_CHUNK
    d = pl.program_id(1)

    @pl.when(pl.program_id(3) == 0)
    def _():
        s_ref[...] = jnp.zeros_like(s_ref)

    sgn = 1 - 2 * d
    row = lax.broadcasted_iota(jnp.int32, (c_len, c_len), 0)
    col = lax.broadcasted_iota(jnp.int32, (c_len, c_len), 1)
    order = (row - col) * sgn
    incl = order >= 0
    strict = order > 0
    cum_mat = incl.astype(F32)
    eye = (row == col).astype(F32)
    sel = (lax.broadcasted_iota(jnp.int32, (SUBLANES, LANES), 0)
           == lax.broadcasted_iota(jnp.int32, (SUBLANES, LANES), 1)).astype(F32)
    scale = DA ** -0.5
    nt = (((1,), (1,)), ((), ()))
    tn = (((0,), (0,)), ((), ()))

    heads = range(GDN_HEAD_GROUP)
    chunk_rows, gcols, grows, exp_gs, exp_rests, exp_tots, gblks = [], [], [], [], [], [], []
    for j in range(n_chunks):
        c = j + d * (n_chunks - 1 - 2 * j)
        rows = pl.ds(pl.multiple_of(c * c_len, c_len), c_len)
        gblk = gb_ref[rows, :]
        gcol = jnp.dot(cum_mat, gblk, precision=HIGHEST, preferred_element_type=F32)
        chunk_rows.append(rows)
        gblks.append(gblk)
        gcols.append(gcol)
    for j in range(n_chunks):
        grows.append(lax.dot_general(sel, gcols[j], nt, precision=HIGHEST, preferred_element_type=F32))
        gtot = jnp.sum(gblks[j], axis=0, keepdims=True)
        exp_gs.append(jnp.exp(gcols[j]))
        exp_rests.append(jnp.exp(gtot - gcols[j]))
        exp_tots.append(jnp.exp(gtot))

    units = [(j, h) for j in range(n_chunks) for h in heads]
    low, intra, rhs = {}, {}, {}
    for (j, h) in units:
        rows = chunk_rows[j]
        lanes = slice(h * DA, (h + 1) * DA)
        kbf = k_ref[rows, lanes]
        kf = kbf.astype(F32)
        beta = gblks[j][:, GDN_HEAD_GROUP + h:GDN_HEAD_GROUP + h + 1]
        diff = gcols[j][:, h:h + 1] - grows[j][h:h + 1, :]
        decay = jnp.where(incl, jnp.exp(jnp.where(incl, diff, 0.0)), 0.0)
        kb = kf * beta
        kk = lax.dot_general(kb.astype(BF16), kbf, nt, preferred_element_type=F32)
        qs = (q_ref[rows, lanes].astype(F32) * scale).astype(BF16)
        qk = lax.dot_general(qs, kbf, nt, preferred_element_type=F32)
        low[j, h] = jnp.where(strict, kk * decay, 0.0)
        intra[j, h] = jnp.where(incl, qk * decay, 0.0).astype(BF16)
        rhs[j, h] = jnp.concatenate([v_ref[rows, lanes].astype(F32) * beta, kb * exp_gs[j][:, h:h + 1]],
                                    axis=1).astype(BF16)

    def square(p):
        return {u: jnp.dot(p[u].astype(BF16), p[u].astype(BF16), preferred_element_type=F32) for u in units}

    def pair(first, p_second):
        return {u: first[u] + jnp.dot(first[u].astype(BF16), p_second[u].astype(BF16), preferred_element_type=F32)
                for u in units}

    p2 = square(low)
    p4 = square(p2)
    fac_a = pair({u: eye - low[u] for u in units}, p2)
    p8 = square(p4)
    p16 = square(p8)
    fac_b = pair({u: eye + p4[u] for u in units}, p8)
    p32 = square(p16)
    fac_ab = {u: jnp.dot(fac_a[u].astype(BF16), fac_b[u].astype(BF16), preferred_element_type=F32) for u in units}
    fac_c = pair({u: eye + p16[u] for u in units}, p32)
    inv = {u: jnp.dot(fac_ab[u].astype(BF16), fac_c[u].astype(BF16), preferred_element_type=F32) for u in units}
    uw = {u: jnp.dot(inv[u].astype(BF16), rhs[u], preferred_element_type=F32).astype(BF16) for u in units}

    chain_lhs, out_local, state_add = {}, {}, {}
    for (j, h) in units:
        rows = chunk_rows[j]
        lanes = slice(h * DA, (h + 1) * DA)
        k_dec = (k_ref[rows, lanes].astype(F32) * exp_rests[j][:, h:h + 1]).astype(BF16)
        k_uw = lax.dot_general(k_dec, uw[j, h], tn, preferred_element_type=F32)
        i_uw = jnp.dot(intra[j, h], uw[j, h], preferred_element_type=F32)
        q_dec = q_ref[rows, lanes].astype(F32) * scale * exp_gs[j][:, h:h + 1]
        chain_lhs[j, h] = jnp.concatenate([q_dec - i_uw[:, DA:], k_uw[:, DA:]], axis=0).astype(BF16)
        out_local[j, h] = i_uw[:, :DA]
        state_add[j, h] = k_uw[:, :DA]

    for j in range(n_chunks):
        rows = chunk_rows[j]
        for h in heads:
            lanes = slice(h * DA, (h + 1) * DA)
            state = s_ref[h]
            prod = jnp.dot(chain_lhs[j, h], state.astype(BF16), preferred_element_type=F32)
            s_ref[h] = state * exp_tots[j][:, h:h + 1] - prod[c_len:] + state_add[j, h]
            o_ref[0, rows, lanes] = (prod[:c_len] + out_local[j, h]).astype(o_ref.dtype)


def gdn_scan(q, k, v, gb, batch, seq, name):
    t = q.shape[0]
    blk = _tile(seq, GDN_BLOCK)
    n_blk = seq // blk
    width = GDN_HEAD_GROUP * DA
    n_hg = HA // GDN_HEAD_GROUP

    def rblk(b, d, i):
        return b * n_blk + i + d * (n_blk - 1 - 2 * i)

    qkv_spec = pl.BlockSpec((blk, width), lambda b, d, g, i: (rblk(b, d, i), g))
    return pl.pallas_call(
        functools.partial(_gdn_scan_kernel, n_chunks=blk // DELTA_CHUNK),
        grid=(batch, 2, n_hg, n_blk),
        in_specs=[qkv_spec, qkv_spec, qkv_spec,
                  pl.BlockSpec((blk, LANES), lambda b, d, g, i: (rblk(b, d, i), d * n_hg + g))],
        out_specs=pl.BlockSpec((1, blk, width), lambda b, d, g, i: (d, rblk(b, d, i), g)),
        out_shape=jax.ShapeDtypeStruct((2, t, A_WIDTH), BF16),
        scratch_shapes=[pltpu.VMEM((GDN_HEAD_GROUP, DA, DA), F32)],
        compiler_params=_cparams(("parallel", "parallel", "parallel", "arbitrary")),
        name=name,
    )(q, k, v, gb)


def _gdn_post_kernel(o_ref, z_ref, g_ref, y_ref):
    for h in range(HA):
        lanes = slice(h * DA, (h + 1) * DA)
        o = o_ref[0, :, lanes].astype(F32) + o_ref[1, :, lanes].astype(F32)
        z = z_ref[:, lanes].astype(F32)
        ms = jnp.mean(o * o, axis=-1, keepdims=True)
        y = o * lax.rsqrt(ms + NORM_EPS) * g_ref[...]
        y_ref[:, lanes] = (y * (z * _sigmoid(z))).astype(y_ref.dtype)


def gdn_post(o2, proj, norm_w, name):
    t = proj.shape[0]
    ts = _tile(t, 512)
    return pl.pallas_call(
        _gdn_post_kernel,
        grid=(t // ts,),
        in_specs=[
            pl.BlockSpec((2, ts, A_WIDTH), lambda i: (0, i, 0)),
            pl.BlockSpec((ts, A_WIDTH), lambda i: (i, OFF_Z // A_WIDTH)),
            pl.BlockSpec((1, DA), lambda i: (0, 0)),
        ],
        out_specs=pl.BlockSpec((ts, A_WIDTH), lambda i: (i, 0)),
        out_shape=jax.ShapeDtypeStruct((t, A_WIDTH), BF16),
        compiler_params=_cparams(("parallel",)),
        name=name,
    )(o2, proj, norm_w.reshape(1, DA))


def _qkv_prep_kernel(x_ref, cos_ref, sa_ref, sb_ref, gq_ref, gk_ref, qk_ref, vt_ref, *, n_q, n_k, n_v, seg, shift,
                     q_scale, split):
    grp_r = lax.broadcasted_iota(jnp.int32, (LANES, LANES), 0) // seg
    grp_c = lax.broadcasted_iota(jnp.int32, (LANES, LANES), 1) // seg
    seg_ones = (grp_r == grp_c).astype(BF16)
    lane = lax.broadcasted_iota(jnp.int32, (1, LANES), 1)
    cos, sin_a, sin_b = cos_ref[...], sa_ref[...], sb_ref[...]
    per = 2 if split else 1
    for h in range(n_q + n_k):
        is_q = h < n_q
        x = x_ref[:, h * LANES:(h + 1) * LANES].astype(F32)
        sq = x * x
        hi = sq.astype(BF16)
        lo = (sq - hi.astype(F32)).astype(BF16)
        ssum = (jnp.dot(hi, seg_ones, preferred_element_type=F32) + jnp.dot(lo, seg_ones, preferred_element_type=F32))
        gain = (gq_ref if is_q else gk_ref)[...]
        y = x * lax.rsqrt(ssum * (1.0 / seg) + NORM_EPS) * gain
        y = y * cos + pltpu.roll(y, LANES - shift, 1) * sin_a + pltpu.roll(y, shift, 1) * sin_b
        if is_q:
            y = y * q_scale
        if is_q and split:
            qk_ref[0, per * h] = jnp.where(lane < seg, y, 0.0).astype(qk_ref.dtype)
            qk_ref[0, per * h + 1] = jnp.where(lane >= seg, y, 0.0).astype(qk_ref.dtype)
        else:
            slot = per * h if is_q else per * n_q + (h - n_q)
            qk_ref[0, slot] = y.astype(qk_ref.dtype)
    for h in range(n_v):
        c0 = (n_q + n_k + h) * LANES
        vt_ref[0, h, 0, 0:LANES, :] = x_ref[:, c0:c0 + LANES].astype(F32).T.astype(vt_ref.dtype)
        vt_ref[0, h, 0, LANES:V_ROWS, :] = jnp.ones((V_ROWS - LANES, vt_ref.shape[-1]), vt_ref.dtype)


def qkv_prep(proj, tables, gq, gk, batch, seq, *, col_off, n_q, n_k, n_v, seg, shift, q_scale, split, tk, name):
    ts = tk
    n_s = seq // ts
    width = (n_q + n_k + n_v) * LANES
    slots = (2 if split else 1) * n_q + n_k
    tab = pl.BlockSpec((ts, LANES), lambda b, i: (i, 0))
    gain = pl.BlockSpec((1, LANES), lambda b, i: (0, 0))
    return pl.pallas_call(
        functools.partial(_qkv_prep_kernel, n_q=n_q, n_k=n_k, n_v=n_v, seg=seg, shift=shift, q_scale=q_scale,
                          split=split),
        grid=(batch, n_s),
        in_specs=[pl.BlockSpec((ts, width), lambda b, i: (b * n_s + i, col_off // width)), tab, tab, tab, gain, gain],
        out_specs=[pl.BlockSpec((1, slots, ts, LANES), lambda b, i: (b, 0, i, 0)),
                   pl.BlockSpec((1, n_v, 1, V_ROWS, tk), lambda b, i: (b, 0, i, 0, 0))],
        out_shape=[jax.ShapeDtypeStruct((batch, slots, seq, LANES), BF16),
                   jax.ShapeDtypeStruct((batch, n_v, n_s, V_ROWS, tk), BF16)],
        compiler_params=_cparams(("parallel", "parallel")),
        name=name,
    )(proj, tables[0], tables[1], tables[2], gq.reshape(1, LANES), gk.reshape(1, LANES))


def _flash_kernel(*refs, groups, tq, tk, seq, diff, lambda_init):
    if diff:
        (q_ref, qn_ref, k_ref, vt_ref, lam_ref, sub_ref, o_ref,
         qt_ref, sa_ref, sb_ref, ma_ref, mb_ref, m_ref, acc_ref) = refs
    else:
        q_ref, qn_ref, k_ref, vt_ref, o_ref, qt_ref, sa_ref, sb_ref, ma_ref, mb_ref, m_ref, acc_ref = refs
    rows = groups * tq
    n_c = seq // tk
    carry_over = n_c % 2 == 0

    def load_q(ref):
        qt_ref[...] = ref[0].reshape(rows, LANES).astype(F32).T.astype(BF16)

    m_ref[...] = jnp.full(m_ref.shape, -jnp.inf, F32)
    acc_ref[...] = jnp.zeros(acc_ref.shape, F32)
    buf_a = (sa_ref, ma_ref)
    buf_b = (sb_ref, mb_ref)

    def scores_into(c, buf):
        ks = pl.ds(pl.multiple_of(c * tk, tk), tk)
        s_t = jnp.dot(k_ref[0, 0, ks, :], qt_ref[...], preferred_element_type=F32)
        buf[0][...] = s_t
        buf[1][...] = jnp.max(s_t, axis=0, keepdims=True)

    def step(c, cur, nxt, wrap=False):
        if wrap:
            load_q(qn_ref)
            scores_into(0, nxt)
        elif nxt is not None:
            scores_into(c + 1, nxt)
        m_prev = m_ref[...]
        m_new = jnp.maximum(m_prev, cur[1][...])
        p_t = jnp.exp2(cur[0][...] - m_new)
        alpha = jnp.exp2(m_prev - m_new)
        acc_ref[...] = alpha * acc_ref[...] + jnp.dot(vt_ref[0, 0, c], p_t.astype(BF16),
                                                      preferred_element_type=F32)
        m_ref[...] = m_new

    def cold_start():
        load_q(q_ref)
        scores_into(0, buf_a)

    if carry_over:
        pl.when(pl.program_id(2) == 0)(cold_start)
    else:
        cold_start()
    bufs = (buf_a, buf_b)
    n_loop = (n_c - 1) // FLASH_UNROLL

    def body(i, carry):
        for u in range(FLASH_UNROLL):
            step(FLASH_UNROLL * i + u, bufs[u % 2], bufs[(u + 1) % 2])
        return carry

    lax.fori_loop(0, n_loop, body, 0)
    for c in range(FLASH_UNROLL * n_loop, n_c):
        last = c + 1 == n_c
        step(c, bufs[c % 2], bufs[(c + 1) % 2] if (not last or carry_over) else None, wrap=last and carry_over)
    o_t = acc_ref[0:LANES, :] / acc_ref[LANES:LANES + 1, :]
    if not diff:
        o = o_t.T
        for g in range(groups):
            o_ref[:, g * LANES:(g + 1) * LANES] = o[g * tq:(g + 1) * tq].astype(o_ref.dtype)
    else:
        lf = lam_ref[...]
        lam = (jnp.exp(jnp.sum(lf[0:1] * lf[1:2], axis=-1, keepdims=True))
               - jnp.exp(jnp.sum(lf[2:3] * lf[3:4], axis=-1, keepdims=True)) + lambda_init)
        d_t = o_t[:, :tq] - lam * o_t[:, tq:]
        ms = jnp.mean(d_t * d_t, axis=0, keepdims=True)
        y = (d_t * lax.rsqrt(ms + NORM_EPS)).T * sub_ref[...] * (1.0 - lambda_init)
        o_ref[...] = y.astype(o_ref.dtype)


def flash_attention(qk, vt, batch, seq, *, kv_heads, groups, k_slot0, k_stride, diff=False,
                    lambdas=None, subln=None, lambda_init=0.0, name):
    tk = vt.shape[-1]
    n_c = seq // tk
    tq = _tile(seq, ATTN_Q_ROWS // groups)
    n_q = seq // tq
    rows = groups * tq
    out_w = (1 if diff else groups) * LANES
    in_specs = [
        pl.BlockSpec((1, groups, tq, LANES), lambda b, h, i: (b, h, i, 0)),
        pl.BlockSpec((1, groups, tq, LANES), lambda b, h, i: (b, h, jnp.minimum(i + 1, n_q - 1), 0)),
        pl.BlockSpec((1, 1, seq, LANES), lambda b, h, i: (b, k_slot0 + k_stride * h, 0, 0)),
        pl.BlockSpec((1, 1, n_c, V_ROWS, tk), lambda b, h, i: (b, h, 0, 0, 0)),
    ]
    args = [qk, qk, qk, vt]
    if diff:
        in_specs += [pl.BlockSpec(lambdas.shape, lambda b, h, i: (0, 0)),
                     pl.BlockSpec((1, LANES), lambda b, h, i: (0, 0))]
        args += [lambdas, subln.reshape(1, LANES)]
    return pl.pallas_call(
        functools.partial(_flash_kernel, groups=groups, tq=tq, tk=tk, seq=seq, diff=diff, lambda_init=lambda_init),
        grid=(batch, kv_heads, n_q),
        in_specs=in_specs,
        out_specs=pl.BlockSpec((tq, out_w), lambda b, h, i: (b * n_q + i, h)),
        out_shape=jax.ShapeDtypeStruct((batch * seq, kv_heads * out_w), BF16),
        scratch_shapes=[pltpu.VMEM((LANES, rows), BF16), pltpu.VMEM((tk, rows), F32), pltpu.VMEM((tk, rows), F32),
                        pltpu.VMEM((1, rows), F32), pltpu.VMEM((1, rows), F32),
                        pltpu.VMEM((1, rows), F32), pltpu.VMEM((V_ROWS, rows), F32)],
        compiler_params=_cparams(("parallel", "parallel", "arbitrary")),
        name=name,
    )(*args)


def _rope_angles(pos, dim, theta):
    inv = theta ** (-jnp.arange(0, dim, 2, dtype=F32) / dim)
    return pos[:, None] * inv[None, :]


def _axial_tables(seq):
    t = jnp.arange(seq)
    row = (t // GRID_W).astype(F32)
    col = (t % GRID_W).astype(F32)
    half = DB // 2
    ang_r = _rope_angles(row, half, AXIAL_THETA)
    ang_c = _rope_angles(col, half, AXIAL_THETA)
    ang = jnp.concatenate([ang_r, ang_r, ang_c, ang_c], axis=-1)
    cos, sin = jnp.cos(ang), jnp.sin(ang)
    first = (jnp.arange(LANES) % half) < (half // 2)
    return cos, jnp.where(first, -sin, 0.0), jnp.where(first, 0.0, sin)


def _partial_tables(seq):
    pos = jnp.arange(seq, dtype=F32)
    ang8 = _rope_angles(pos, ROT_C, ROPE_THETA)
    lane = jnp.arange(LANES) % DC
    ang = jnp.take(ang8, lane % (ROT_C // 2), axis=1)
    rot = lane < ROT_C
    first = lane < ROT_C // 2
    cos = jnp.where(rot, jnp.cos(ang), 1.0)
    sin = jnp.where(rot, jnp.sin(ang), 0.0)
    return cos, jnp.where(first, -sin, 0.0), jnp.where(first, 0.0, sin)


def _permute_w_in(w):
    k = w.shape[0]
    alpha0 = 4 * A_WIDTH
    beta0 = alpha0 + 2 * HA
    blocks = []
    for d in range(2):
        for g in range(HA // GDN_HEAD_GROUP):
            a0 = alpha0 + d * HA + g * GDN_HEAD_GROUP
            b0 = beta0 + d * HA + g * GDN_HEAD_GROUP
            blocks += [w[:, a0:a0 + GDN_HEAD_GROUP], w[:, b0:b0 + GDN_HEAD_GROUP],
                       jnp.zeros((k, LANES - 2 * GDN_HEAD_GROUP), w.dtype)]
    return jnp.concatenate([w[:, :alpha0]] + blocks + [w[:, beta0 + 2 * HA:]], axis=1)


def _gdn_param_row(p):
    blocks = []
    for d in range(2):
        for g in range(HA // GDN_HEAD_GROUP):
            blocks += [p[d, g * GDN_HEAD_GROUP:(g + 1) * GDN_HEAD_GROUP], jnp.zeros((LANES - GDN_HEAD_GROUP,), p.dtype)]
    return jnp.concatenate(blocks).reshape(1, AB_WIDTH)


def _encoder_layer(x, p_emb, w, li, batch, seq, tabs_b, tabs_c, tag):
    lambda_init = 0.8 - 0.6 * math.exp(-0.3 * li)
    proj = norm_matmul(x, w['norm_mix'], w['w_in'], f"in_proj_{tag}")

    q_a, k_a, v_a, gb = gdn_prep(proj, w['conv_a'], w['alog_row'], w['dtb_row'], batch, seq, f"gdn_prep_{tag}")
    o2 = gdn_scan(q_a, k_a, v_a, gb, batch, seq, f"gdn_scan_{tag}")
    y_a = gdn_post(o2, proj, w['norm_a'], f"gdn_post_{tag}")

    tk = _tile(seq, ATTN_KV_CHUNK)
    qk_b, vt_b = qkv_prep(proj, tabs_b, w['qn_b'], w['kn_b'], batch, seq, col_off=OFF_BQ, n_q=HB, n_k=KVB, n_v=KVB,
                          seg=DB, shift=DB // 4, q_scale=DB ** -0.5 * LOG2_E, split=False, tk=tk,
                          name=f"prep_b_{tag}")
    y_b = flash_attention(qk_b, vt_b, batch, seq, kv_heads=KVB, groups=HB // KVB, k_slot0=HB, k_stride=1,
                          name=f"attn_b_{tag}")

    qk_c, vt_c = qkv_prep(proj, tabs_c, w['qn_c2'], w['kn_c2'], batch, seq, col_off=OFF_CQ, n_q=HC, n_k=HC, n_v=HC,
                          seg=DC, shift=ROT_C // 2, q_scale=DC ** -0.5 * LOG2_E, split=True, tk=tk,
                          name=f"prep_c_{tag}")
    y_c = flash_attention(qk_c, vt_c, batch, seq, kv_heads=HC, groups=2, k_slot0=2 * HC, k_stride=1,
                          diff=True, lambdas=w['lambdas_c'], subln=w['subln_c'],
                          lambda_init=lambda_init, name=f"attn_c_{tag}")

    x, h_ffn = merge_out((y_a, y_b, y_c), proj, (w['w_o_a'], w['w_o_b'], w['w_o_c']), w['w_out'], x,
                         w['norm_ffn'], f"merge_out_{tag}")

    u = matmul_bf16(h_ffn, w['w_up'], f"ffn_up_{tag}")
    return ffn_tail(u, w['conv_ffn'], w['conv_ffn_b'], w['w_down'], x, p_emb, w['w_ple_gate'], w['w_ple'],
                    batch, seq, f"ffn_tail_{tag}")


def kernel(x_prompt, x_sample, p_prompt, p_sample, norm_mix, w_in, conv_a, a_log, dt_bias, norm_a, qn_b, kn_b, qn_c, kn_c, lambdas_c, subln_c, w_o_a, w_o_b, w_o_c, w_out, norm_ffn, w_up, conv_ffn, conv_ffn_b, w_down, w_ple, w_ple_gate):
    depth = w_in.shape[0]
    layers = []
    for i in range(depth):
        layers.append(dict(
            norm_mix=norm_mix[i], w_in=_permute_w_in(w_in[i]).astype(BF16), conv_a=conv_a[i],
            alog_row=_gdn_param_row(a_log[i]), dtb_row=_gdn_param_row(dt_bias[i]), norm_a=norm_a[i],
            qn_b=qn_b[i], kn_b=kn_b[i], qn_c2=jnp.tile(qn_c[i], 2), kn_c2=jnp.tile(kn_c[i], 2),
            lambdas_c=lambdas_c[i], subln_c=subln_c[i],
            w_o_a=w_o_a[i].astype(BF16), w_o_b=w_o_b[i].astype(BF16), w_o_c=w_o_c[i].astype(BF16),
            w_out=w_out[i].astype(BF16), norm_ffn=norm_ffn[i], w_up=w_up[i].astype(BF16),
            conv_ffn=conv_ffn[i], conv_ffn_b=conv_ffn_b[i], w_down=w_down[i].astype(BF16),
            w_ple=w_ple[i].astype(BF16), w_ple_gate=w_ple_gate[i].astype(BF16)))

    outs = []
    for tag, x, p in (("p", x_prompt, p_prompt), ("s", x_sample, p_sample)):
        batch, seq, d = x.shape
        tabs_b = _axial_tables(seq)
        tabs_c = _partial_tables(seq)
        h = x.reshape(batch * seq, d)
        for i in range(depth):
            h = _encoder_layer(h, p[i].reshape(batch * seq, -1), layers[i], i, batch, seq, tabs_b, tabs_c, f"{tag}{i}")
        outs.append(h.reshape(batch, seq, d))
    return tuple(outs)
```

```python
import functools
import math

import jax
import jax.numpy as jnp
from jax import lax
from jax.experimental import pallas as pl
from jax.experimental.pallas import tpu as pltpu

F32 = jnp.float32
BF16 = jnp.bfloat16
HIGHEST = lax.Precision.HIGHEST

D_MODEL = 2048
DEPTH = 2
GRID_W = 64
PLE_DIM = 256
NORM_EPS = 1e-6
HA = 8
DA = 128
A_WIDTH = HA * DA
DELTA_CHUNK = 64
HB = 8
KVB = 2
DB = 128
AXIAL_THETA = 10000.0
HC = 8
DC = 64
DVC = 2 * DC
ROT_C = DC // 4
ROPE_THETA = 500000.0
D_FF = 5632

LANES = 128
SUBLANES = 8
VMEM_LIMIT_BYTES = 56 * 1024 * 1024

OFF_QKV = 0
OFF_Z = 3072
OFF_AB = 4096
AB_WIDTH = 512
OFF_BQ = 4608
OFF_BV = 5888
OFF_CQ = 6144
OFF_CV = 8192
OFF_GATE = 9216
N_PROJ = 15360
GDN_HEAD_GROUP = 4
GDN_BLOCK = 512
ATTN_KV_CHUNK = 1024
ATTN_Q_ROWS = 2048
FLASH_UNROLL = 2
V_ROWS = LANES + 16
LOG2_E = math.log2(math.e)


def _cparams(semantics):
    return pltpu.CompilerParams(dimension_semantics=semantics, vmem_limit_bytes=VMEM_LIMIT_BYTES)


def _tile(n, pref):
    t = min(n, pref)
    while n % t:
        t //= 2
    return t


def _sigmoid(x):
    return 1.0 / (1.0 + jnp.exp(-x))


def _softplus(x):
    return jnp.maximum(x, 0.0) + jnp.log(1.0 + jnp.exp(-jnp.abs(x)))


def _norm_matmul_kernel(x_ref, g_ref, w_ref, o_ref, h_ref, *, rows):
    @pl.when(pl.program_id(1) == 0)
    def _():
        def chunk(r, c):
            rs = pl.ds(pl.multiple_of(r * rows, rows), rows)
            x = x_ref[rs, :]
            ms = jnp.mean(x * x, axis=-1, keepdims=True)
            h_ref[rs, :] = (x * lax.rsqrt(ms + NORM_EPS) * g_ref[...]).astype(BF16)
            return c
        lax.fori_loop(0, x_ref.shape[0] // rows, chunk, 0)

    o_ref[...] = jnp.dot(h_ref[...], w_ref[...], preferred_element_type=F32).astype(o_ref.dtype)


def norm_matmul(x, g, w, name):
    m, k = x.shape
    n = w.shape[1]
    tm = _tile(m, 1024)
    tn = 1536 if n % 1536 == 0 else _tile(n, 1024)
    rows = _tile(tm, 128)
    return pl.pallas_call(
        functools.partial(_norm_matmul_kernel, rows=rows),
        grid=(m // tm, n // tn),
        in_specs=[
            pl.BlockSpec((tm, k), lambda i, j: (i, 0)),
            pl.BlockSpec((1, k), lambda i, j: (0, 0)),
            pl.BlockSpec((k, tn), lambda i, j: (0, j)),
        ],
        out_specs=pl.BlockSpec((tm, tn), lambda i, j: (i, j)),
        out_shape=jax.ShapeDtypeStruct((m, n), BF16),
        scratch_shapes=[pltpu.VMEM((tm, k), BF16)],
        compiler_params=_cparams(("parallel", "arbitrary")),
        name=name,
    )(x, g.reshape(1, k), w)


def _resident(shape):
    return pl.BlockSpec(shape, lambda *_: (0,) * len(shape), pipeline_mode=pl.Buffered(1))


def _merge_out_kernel(*refs, n_branch, n_chunk, tc):
    o2_ref, z_ref, na_ref = refs[:3]
    refs = refs[3:]
    y_refs = refs[:n_branch - 1]
    g_refs = refs[n_branch - 1:n_branch - 1 + n_branch * n_chunk]
    w_refs = refs[n_branch - 1 + n_branch * n_chunk:2 * n_branch - 1 + n_branch * n_chunk]
    wo_ref, x_ref, gn_ref, o_ref, h_ref, m_ref, ya_ref = refs[2 * n_branch - 1 + n_branch * n_chunk:]
    _gdn_gate_norm(o2_ref, z_ref, na_ref, ya_ref)
    y_refs = (ya_ref,) + tuple(y_refs)
    for c in range(n_chunk):
        cols = slice(c * tc, (c + 1) * tc)
        acc = None
        for i in range(n_branch):
            gate = _sigmoid(g_refs[i * n_chunk + c][...].astype(F32))
            term = gate * jnp.dot(y_refs[i][...], w_refs[i][:, cols], preferred_element_type=F32)
            acc = term if acc is None else acc + term
        m_ref[:, cols] = acc.astype(BF16)
    x1 = x_ref[...] + jnp.dot(m_ref[...], wo_ref[...], preferred_element_type=F32)
    o_ref[...] = x1
    ms = jnp.mean(x1 * x1, axis=-1, keepdims=True)
    h_ref[...] = (x1 * lax.rsqrt(ms + NORM_EPS) * gn_ref[...]).astype(h_ref.dtype)


def merge_out(o2, norm_a, ys, proj, w_os, w_out, x, g_next, name):
    m, k = ys[0].shape
    n = w_out.shape[1]
    tm = _tile(m, 256)
    tc = _tile(n, 1024)
    n_chunk = n // tc
    n_branch = len(ys) + 1
    gate_blk = OFF_GATE // tc
    row = lambda i: (i, 0)
    g_specs = [pl.BlockSpec((tm, tc), lambda i, b=b: (i, gate_blk + b)) for b in range(n_branch * n_chunk)]
    return pl.pallas_call(
        functools.partial(_merge_out_kernel, n_branch=n_branch, n_chunk=n_chunk, tc=tc),
        grid=(m // tm,),
        in_specs=([pl.BlockSpec((2, tm, k), lambda i: (0, i, 0)), pl.BlockSpec((tm, k), lambda i: (i, OFF_Z // k)),
                   _resident((1, DA))]
                  + [pl.BlockSpec((tm, k), row)] * len(ys) + g_specs + [_resident((k, n))] * n_branch
                  + [_resident((n, n)), pl.BlockSpec((tm, n), row), _resident((1, n))]),
        out_specs=[pl.BlockSpec((tm, n), row), pl.BlockSpec((tm, n), row)],
        out_shape=[jax.ShapeDtypeStruct((m, n), F32), jax.ShapeDtypeStruct((m, n), BF16)],
        scratch_shapes=[pltpu.VMEM((tm, n), BF16), pltpu.VMEM((tm, k), BF16)],
        compiler_params=_cparams(("parallel",)),
        name=name,
    )(o2, proj, norm_a.reshape(1, DA), *ys, *([proj] * (n_branch * n_chunk)), *w_os, w_out, x, g_next.reshape(1, n))


def _matmul_kernel(a_ref, w_ref, o_ref):
    o_ref[...] = jnp.dot(a_ref[...], w_ref[...], preferred_element_type=F32).astype(o_ref.dtype)


def matmul_bf16(a, w, name):
    m, k = a.shape
    n = w.shape[1]
    tm = _tile(m, 1024)
    tn = _tile(n, 1024)
    return pl.pallas_call(
        _matmul_kernel,
        grid=(m // tm, n // tn),
        in_specs=[pl.BlockSpec((tm, k), lambda i, j: (i, 0)), pl.BlockSpec((k, tn), lambda i, j: (0, j))],
        out_specs=pl.BlockSpec((tm, tn), lambda i, j: (i, j)),
        out_shape=jax.ShapeDtypeStruct((m, n), BF16),
        compiler_params=_cparams(("parallel", "arbitrary")),
        name=name,
    )(a, w)


def _shift_mats(ts):
    r = lax.broadcasted_iota(jnp.int32, (ts, ts), 0)
    c = lax.broadcasted_iota(jnp.int32, (ts, ts), 1)
    return (c == r - 1).astype(BF16), (c == r + 1).astype(BF16)


def _conv3(xb, prev_row, next_row, w_ref, lanes, shifts):
    ts = xb.shape[0]
    w0, w1, w2 = w_ref[0:1, lanes], w_ref[1:2, lanes], w_ref[2:3, lanes]
    xp = jnp.dot(shifts[0], xb, preferred_element_type=F32)
    xn = jnp.dot(shifts[1], xb, preferred_element_type=F32)
    y = xp * w0 + xb.astype(F32) * w1 + xn * w2
    r8 = lax.broadcasted_iota(jnp.int32, (SUBLANES, 1), 0)
    first = y[0:SUBLANES] + jnp.where(r8 == 0, prev_row, 0.0) * w0
    last = y[ts - SUBLANES:] + jnp.where(r8 == SUBLANES - 1, next_row, 0.0) * w2
    return jnp.concatenate([first, y[SUBLANES:ts - SUBLANES], last], axis=0)


def _halo_specs(batch, seq, ts, width, col_fn):
    n_s = seq // ts
    per_seq = seq // SUBLANES
    per_tile = ts // SUBLANES
    last = batch * per_seq - 1

    def prev_map(b, i, *rest):
        return (jnp.maximum(b * per_seq + i * per_tile - 1, 0), col_fn(*rest))

    def next_map(b, i, *rest):
        return (jnp.minimum(b * per_seq + (i + 1) * per_tile, last), col_fn(*rest))

    return pl.BlockSpec((SUBLANES, width), prev_map), pl.BlockSpec((SUBLANES, width), next_map), n_s


def _conv3_roll(x, prev_row, next_row, w_ref, lanes):
    ts = x.shape[0]
    r = lax.broadcasted_iota(jnp.int32, (ts, 1), 0)
    xp = jnp.where(r == 0, prev_row, pltpu.roll(x, 1, 0))
    xn = jnp.where(r == ts - 1, next_row, pltpu.roll(x, ts - 1, 0))
    return xp * w_ref[0:1, lanes] + x * w_ref[1:2, lanes] + xn * w_ref[2:3, lanes]


def _ffn_tail_kernel(u_ref, up_ref, un_ref, cw_ref, cb_ref, wd_ref, x_ref, p_ref, wg_ref, we_ref, o_ref, *, n_s, f, tc):
    i = pl.program_id(1)
    has_prev = (i > 0).astype(F32)
    has_next = (i < n_s - 1).astype(F32)

    def conv(lanes):
        prev_row = up_ref[SUBLANES - 1:SUBLANES, lanes].astype(F32) * has_prev
        next_row = un_ref[0:1, lanes].astype(F32) * has_next
        return _conv3_roll(u_ref[:, lanes].astype(F32), prev_row, next_row, cw_ref, lanes) + cb_ref[:, lanes]

    x2 = x_ref[...]
    for c in range(f // tc):
        gate = conv(slice(c * tc, (c + 1) * tc))
        val = conv(slice(f + c * tc, f + (c + 1) * tc))
        act = (gate * _sigmoid(gate) * val).astype(BF16)
        x2 = x2 + jnp.dot(act, wd_ref[c * tc:(c + 1) * tc, :], preferred_element_type=F32)
    gate = _sigmoid(jnp.dot(x2.astype(BF16), wg_ref[...], preferred_element_type=F32))
    emb = jnp.dot(p_ref[...].astype(BF16), we_ref[...], preferred_element_type=F32)
    o_ref[...] = x2 + gate * emb


def ffn_tail(u, conv_w, conv_b, w_down, x, p, wg, we, batch, seq, name):
    t, two_f = u.shape
    f = two_f // 2
    n = w_down.shape[1]
    kp = p.shape[1]
    tm = _tile(seq, 256)
    prev_s, next_s, n_s = _halo_specs(batch, seq, tm, two_f, lambda: 0)
    row = lambda b, i: (b * n_s + i, 0)
    return pl.pallas_call(
        functools.partial(_ffn_tail_kernel, n_s=n_s, f=f, tc=512),
        grid=(batch, n_s),
        in_specs=[pl.BlockSpec((tm, two_f), row), prev_s, next_s, _resident((3, two_f)), _resident((1, two_f)),
                  _resident((f, n)), pl.BlockSpec((tm, n), row), pl.BlockSpec((tm, kp), row),
                  _resident((n, n)), _resident((kp, n))],
        out_specs=pl.BlockSpec((tm, n), row),
        out_shape=jax.ShapeDtypeStruct((t, n), F32),
        compiler_params=_cparams(("parallel", "parallel")),
        name=name,
    )(u, u, u, conv_w, conv_b.reshape(1, two_f), w_down, x, p, wg, we)


def _gdn_prep_kernel(x_ref, xp_ref, xn_ref, ab_ref, w_ref, alog_ref, dtb_ref, q_ref, k_ref, v_ref, gb_ref, *, n_s):
    i = pl.program_id(1)
    has_prev = (i > 0).astype(F32)
    has_next = (i < n_s - 1).astype(F32)
    shifts = _shift_mats(x_ref.shape[0])
    for c in range(3 * HA):
        lanes = slice(c * DA, (c + 1) * DA)
        prev_row = xp_ref[SUBLANES - 1:SUBLANES, lanes].astype(F32) * has_prev
        next_row = xn_ref[0:1, lanes].astype(F32) * has_next
        y = _conv3(x_ref[:, lanes], prev_row, next_row, w_ref, lanes, shifts)
        y = y * _sigmoid(y)
        out_lanes = slice((c % HA) * DA, (c % HA + 1) * DA)
        if c < 2 * HA:
            y = y * lax.rsqrt(jnp.sum(y * y, axis=-1, keepdims=True) + NORM_EPS)
            (q_ref if c < HA else k_ref)[:, out_lanes] = y.astype(BF16)
        else:
            v_ref[:, out_lanes] = y.astype(BF16)
    ab = ab_ref[...].astype(F32)
    lane = lax.broadcasted_iota(jnp.int32, (1, AB_WIDTH), 1) % LANES
    g = -jnp.exp(alog_ref[...]) * _softplus(ab + dtb_ref[...])
    gb_ref[...] = jnp.where(lane < GDN_HEAD_GROUP, g, _sigmoid(ab))


def gdn_prep(proj, conv_w, alog_row, dtb_row, batch, seq, name):
    t = proj.shape[0]
    ts = _tile(seq, 256)
    w3 = 3 * A_WIDTH
    prev_s, next_s, n_s = _halo_specs(batch, seq, ts, w3, lambda: 0)
    row = lambda b, i: (b * n_s + i, 0)
    out_spec = pl.BlockSpec((ts, A_WIDTH), row)
    return pl.pallas_call(
        functools.partial(_gdn_prep_kernel, n_s=n_s),
        grid=(batch, n_s),
        in_specs=[
            pl.BlockSpec((ts, w3), row), prev_s, next_s,
            pl.BlockSpec((ts, AB_WIDTH), lambda b, i: (b * n_s + i, OFF_AB // AB_WIDTH)),
            pl.BlockSpec((3, w3), lambda b, i: (0, 0)),
            pl.BlockSpec((1, AB_WIDTH), lambda b, i: (0, 0)),
            pl.BlockSpec((1, AB_WIDTH), lambda b, i: (0, 0)),
        ],
        out_specs=[out_spec, out_spec, out_spec, pl.BlockSpec((ts, AB_WIDTH), row)],
        out_shape=[jax.ShapeDtypeStruct((t, A_WIDTH), BF16)] * 3 + [jax.ShapeDtypeStruct((t, AB_WIDTH), F32)],
        compiler_params=_cparams(("parallel", "parallel")),
        name=name,
    )(proj, proj, proj, proj, conv_w, alog_row, dtb_row)


def _gdn_scan_kernel(q_ref, k_ref, v_ref, gb_ref, o_ref, s_ref, *, n_chunks):
    c_len = DELTA_CHUNK
    d = pl.program_id(1)

    @pl.when(pl.program_id(3) == 0)
    def _():
        s_ref[...] = jnp.zeros_like(s_ref)

    sgn = 1 - 2 * d
    row = lax.broadcasted_iota(jnp.int32, (c_len, c_len), 0)
    col = lax.broadcasted_iota(jnp.int32, (c_len, c_len), 1)
    order = (row - col) * sgn
    incl = order >= 0
    strict = order > 0
    cum_mat = incl.astype(F32)
    eye = (row == col).astype(F32)
    sel = (lax.broadcasted_iota(jnp.int32, (SUBLANES, LANES), 0)
           == lax.broadcasted_iota(jnp.int32, (SUBLANES, LANES), 1)).astype(F32)
    scale = DA ** -0.5
    nt = (((1,), (1,)), ((), ()))
    tn = (((0,), (0,)), ((), ()))

    heads = range(GDN_HEAD_GROUP)
    chunk_rows, gcols, grows, exp_gs, exp_rests, exp_tots, gblks = [], [], [], [], [], [], []
    for j in range(n_chunks):
        c = j + d * (n_chunks - 1 - 2 * j)
        rows = pl.ds(pl.multiple_of(c * c_len, c_len), c_len)
        gblk = gb_ref[rows, :]
        gcol = jnp.dot(cum_mat, gblk, precision=HIGHEST, preferred_element_type=F32)
        chunk_rows.append(rows)
        gblks.append(gblk)
        gcols.append(gcol)
    for j in range(n_chunks):
        grows.append(lax.dot_general(sel, gcols[j], nt, precision=HIGHEST, preferred_element_type=F32))
        gtot = jnp.sum(gblks[j], axis=0, keepdims=True)
        exp_gs.append(jnp.exp(gcols[j]))
        exp_rests.append(jnp.exp(gtot - gcols[j]))
        exp_tots.append(jnp.exp(gtot))

    units = [(j, h) for j in range(n_chunks) for h in heads]
    low, intra, rhs = {}, {}, {}
    for (j, h) in units:
        rows = chunk_rows[j]
        lanes = slice(h * DA, (h + 1) * DA)
        kbf = k_ref[rows, lanes]
        kf = kbf.astype(F32)
        beta = gblks[j][:, GDN_HEAD_GROUP + h:GDN_HEAD_GROUP + h + 1]
        diff = gcols[j][:, h:h + 1] - grows[j][h:h + 1, :]
        decay = jnp.where(incl, jnp.exp(jnp.where(incl, diff, 0.0)), 0.0)
        kb = kf * beta
        kk = lax.dot_general(kb.astype(BF16), kbf, nt, preferred_element_type=F32)
        qs = (q_ref[rows, lanes].astype(F32) * scale).astype(BF16)
        qk = lax.dot_general(qs, kbf, nt, preferred_element_type=F32)
        low[j, h] = jnp.where(strict, kk * decay, 0.0)
        intra[j, h] = jnp.where(incl, qk * decay, 0.0).astype(BF16)
        rhs[j, h] = jnp.concatenate([v_ref[rows, lanes].astype(F32) * beta, kb * exp_gs[j][:, h:h + 1]],
                                    axis=1).astype(BF16)

    def square(p):
        return {u: jnp.dot(p[u].astype(BF16), p[u].astype(BF16), preferred_element_type=F32) for u in units}

    def pair(first, p_second):
        return {u: first[u] + jnp.dot(first[u].astype(BF16), p_second[u].astype(BF16), preferred_element_type=F32)
                for u in units}

    p2 = square(low)
    p4 = square(p2)
    fac_a = pair({u: eye - low[u] for u in units}, p2)
    p8 = square(p4)
    p16 = square(p8)
    fac_b = pair({u: eye + p4[u] for u in units}, p8)
    p32 = square(p16)
    fac_ab = {u: jnp.dot(fac_a[u].astype(BF16), fac_b[u].astype(BF16), preferred_element_type=F32) for u in units}
    fac_c = pair({u: eye + p16[u] for u in units}, p32)
    inv = {u: jnp.dot(fac_ab[u].astype(BF16), fac_c[u].astype(BF16), preferred_element_type=F32) for u in units}
    uw = {u: jnp.dot(inv[u].astype(BF16), rhs[u], preferred_element_type=F32).astype(BF16) for u in units}

    chain_lhs, out_local, state_add = {}, {}, {}
    for (j, h) in units:
        rows = chunk_rows[j]
        lanes = slice(h * DA, (h + 1) * DA)
        k_dec = (k_ref[rows, lanes].astype(F32) * exp_rests[j][:, h:h + 1]).astype(BF16)
        k_uw = lax.dot_general(k_dec, uw[j, h], tn, preferred_element_type=F32)
        i_uw = jnp.dot(intra[j, h], uw[j, h], preferred_element_type=F32)
        q_dec = q_ref[rows, lanes].astype(F32) * scale * exp_gs[j][:, h:h + 1]
        chain_lhs[j, h] = jnp.concatenate([q_dec - i_uw[:, DA:], k_uw[:, DA:]], axis=0).astype(BF16)
        out_local[j, h] = i_uw[:, :DA]
        state_add[j, h] = k_uw[:, :DA]

    for j in range(n_chunks):
        rows = chunk_rows[j]
        for h in heads:
            lanes = slice(h * DA, (h + 1) * DA)
            state = s_ref[h]
            prod = jnp.dot(chain_lhs[j, h], state.astype(BF16), preferred_element_type=F32)
            s_ref[h] = state * exp_tots[j][:, h:h + 1] - prod[c_len:] + state_add[j, h]
            o_ref[0, rows, lanes] = (prod[:c_len] + out_local[j, h]).astype(o_ref.dtype)


def gdn_scan(q, k, v, gb, batch, seq, name):
    t = q.shape[0]
    blk = _tile(seq, GDN_BLOCK)
    n_blk = seq // blk
    width = GDN_HEAD_GROUP * DA
    n_hg = HA // GDN_HEAD_GROUP

    def rblk(b, d, i):
        return b * n_blk + i + d * (n_blk - 1 - 2 * i)

    qkv_spec = pl.BlockSpec((blk, width), lambda b, d, g, i: (rblk(b, d, i), g))
    return pl.pallas_call(
        functools.partial(_gdn_scan_kernel, n_chunks=blk // DELTA_CHUNK),
        grid=(batch, 2, n_hg, n_blk),
        in_specs=[qkv_spec, qkv_spec, qkv_spec,
                  pl.BlockSpec((blk, LANES), lambda b, d, g, i: (rblk(b, d, i), d * n_hg + g))],
        out_specs=pl.BlockSpec((1, blk, width), lambda b, d, g, i: (d, rblk(b, d, i), g)),
        out_shape=jax.ShapeDtypeStruct((2, t, A_WIDTH), BF16),
        scratch_shapes=[pltpu.VMEM((GDN_HEAD_GROUP, DA, DA), F32)],
        compiler_params=_cparams(("parallel", "parallel", "parallel", "arbitrary")),
        name=name,
    )(q, k, v, gb)


def _gdn_gate_norm(o_ref, z_ref, g_ref, y_ref):
    for h in range(HA):
        lanes = slice(h * DA, (h + 1) * DA)
        o = o_ref[0, :, lanes].astype(F32) + o_ref[1, :, lanes].astype(F32)
        z = z_ref[:, lanes].astype(F32)
        ms = jnp.mean(o * o, axis=-1, keepdims=True)
        y = o * lax.rsqrt(ms + NORM_EPS) * g_ref[...]
        y_ref[:, lanes] = (y * (z * _sigmoid(z))).astype(y_ref.dtype)


def _qkv_prep_kernel(x_ref, cos_ref, sa_ref, sb_ref, gq_ref, gk_ref, qk_ref, vt_ref, *, n_q, n_k, n_v, seg, shift,
                     q_scale, split):
    grp_r = lax.broadcasted_iota(jnp.int32, (LANES, LANES), 0) // seg
    grp_c = lax.broadcasted_iota(jnp.int32, (LANES, LANES), 1) // seg
    seg_ones = (grp_r == grp_c).astype(BF16)
    lane = lax.broadcasted_iota(jnp.int32, (1, LANES), 1)
    cos, sin_a, sin_b = cos_ref[...], sa_ref[...], sb_ref[...]
    per = 2 if split else 1
    for h in range(n_q + n_k):
        is_q = h < n_q
        x = x_ref[:, h * LANES:(h + 1) * LANES].astype(F32)
        sq = x * x
        hi = sq.astype(BF16)
        lo = (sq - hi.astype(F32)).astype(BF16)
        ssum = (jnp.dot(hi, seg_ones, preferred_element_type=F32) + jnp.dot(lo, seg_ones, preferred_element_type=F32))
        gain = (gq_ref if is_q else gk_ref)[...]
        y = x * lax.rsqrt(ssum * (1.0 / seg) + NORM_EPS) * gain
        y = y * cos + pltpu.roll(y, LANES - shift, 1) * sin_a + pltpu.roll(y, shift, 1) * sin_b
        if is_q:
            y = y * q_scale
        if is_q and split:
            qk_ref[0, per * h] = jnp.where(lane < seg, y, 0.0).astype(qk_ref.dtype)
            qk_ref[0, per * h + 1] = jnp.where(lane >= seg, y, 0.0).astype(qk_ref.dtype)
        else:
            slot = per * h if is_q else per * n_q + (h - n_q)
            qk_ref[0, slot] = y.astype(qk_ref.dtype)
    for h in range(n_v):
        c0 = (n_q + n_k + h) * LANES
        vt_ref[0, h, 0, 0:LANES, :] = x_ref[:, c0:c0 + LANES].astype(F32).T.astype(vt_ref.dtype)
        vt_ref[0, h, 0, LANES:V_ROWS, :] = jnp.ones((V_ROWS - LANES, vt_ref.shape[-1]), vt_ref.dtype)


def qkv_prep(proj, tables, gq, gk, batch, seq, *, col_off, n_q, n_k, n_v, seg, shift, q_scale, split, tk, name):
    ts = tk
    n_s = seq // ts
    width = (n_q + n_k + n_v) * LANES
    slots = (2 if split else 1) * n_q + n_k
    tab = pl.BlockSpec((ts, LANES), lambda b, i: (i, 0))
    gain = pl.BlockSpec((1, LANES), lambda b, i: (0, 0))
    return pl.pallas_call(
        functools.partial(_qkv_prep_kernel, n_q=n_q, n_k=n_k, n_v=n_v, seg=seg, shift=shift, q_scale=q_scale,
                          split=split),
        grid=(batch, n_s),
        in_specs=[pl.BlockSpec((ts, width), lambda b, i: (b * n_s + i, col_off // width)), tab, tab, tab, gain, gain],
        out_specs=[pl.BlockSpec((1, slots, ts, LANES), lambda b, i: (b, 0, i, 0)),
                   pl.BlockSpec((1, n_v, 1, V_ROWS, tk), lambda b, i: (b, 0, i, 0, 0))],
        out_shape=[jax.ShapeDtypeStruct((batch, slots, seq, LANES), BF16),
                   jax.ShapeDtypeStruct((batch, n_v, n_s, V_ROWS, tk), BF16)],
        compiler_params=_cparams(("parallel", "parallel")),
        name=name,
    )(proj, tables[0], tables[1], tables[2], gq.reshape(1, LANES), gk.reshape(1, LANES))


def _flash_kernel(*refs, groups, tq, tk, seq, diff, lambda_init):
    if diff:
        (q_ref, qn_ref, k_ref, vt_ref, lam_ref, sub_ref, o_ref,
         qt_ref, sa_ref, sb_ref, ma_ref, mb_ref, m_ref, acc_ref) = refs
    else:
        q_ref, qn_ref, k_ref, vt_ref, o_ref, qt_ref, sa_ref, sb_ref, ma_ref, mb_ref, m_ref, acc_ref = refs
    rows = groups * tq
    n_c = seq // tk
    carry_over = n_c % 2 == 0

    def load_q(ref):
        qt_ref[...] = ref[0].reshape(rows, LANES).astype(F32).T.astype(BF16)

    m_ref[...] = jnp.full(m_ref.shape, -jnp.inf, F32)
    acc_ref[...] = jnp.zeros(acc_ref.shape, F32)
    buf_a = (sa_ref, ma_ref)
    buf_b = (sb_ref, mb_ref)

    def scores_into(c, buf):
        ks = pl.ds(pl.multiple_of(c * tk, tk), tk)
        s_t = jnp.dot(k_ref[0, 0, ks, :], qt_ref[...], preferred_element_type=F32)
        buf[0][...] = s_t
        buf[1][...] = jnp.max(s_t, axis=0, keepdims=True)

    def step(c, cur, nxt, wrap=False):
        if wrap:
            load_q(qn_ref)
            scores_into(0, nxt)
        elif nxt is not None:
            scores_into(c + 1, nxt)
        m_prev = m_ref[...]
        m_new = jnp.maximum(m_prev, cur[1][...])
        p_t = jnp.exp2(cur[0][...] - m_new)
        alpha = jnp.exp2(m_prev - m_new)
        acc_ref[...] = alpha * acc_ref[...] + jnp.dot(vt_ref[0, 0, c], p_t.astype(BF16),
                                                      preferred_element_type=F32)
        m_ref[...] = m_new

    def cold_start():
        load_q(q_ref)
        scores_into(0, buf_a)

    if carry_over:
        pl.when(pl.program_id(2) == 0)(cold_start)
    else:
        cold_start()
    bufs = (buf_a, buf_b)
    n_loop = (n_c - 1) // FLASH_UNROLL

    def body(i, carry):
        for u in range(FLASH_UNROLL):
            step(FLASH_UNROLL * i + u, bufs[u % 2], bufs[(u + 1) % 2])
        return carry

    lax.fori_loop(0, n_loop, body, 0)
    for c in range(FLASH_UNROLL * n_loop, n_c):
        last = c + 1 == n_c
        step(c, bufs[c % 2], bufs[(c + 1) % 2] if (not last or carry_over) else None, wrap=last and carry_over)
    o_t = acc_ref[0:LANES, :] / acc_ref[LANES:LANES + 1, :]
    if not diff:
        o = o_t.T
        for g in range(groups):
            o_ref[:, g * LANES:(g + 1) * LANES] = o[g * tq:(g + 1) * tq].astype(o_ref.dtype)
    else:
        lf = lam_ref[...]
        lam = (jnp.exp(jnp.sum(lf[0:1] * lf[1:2], axis=-1, keepdims=True))
               - jnp.exp(jnp.sum(lf[2:3] * lf[3:4], axis=-1, keepdims=True)) + lambda_init)
        d_t = o_t[:, :tq] - lam * o_t[:, tq:]
        ms = jnp.mean(d_t * d_t, axis=0, keepdims=True)
        y = (d_t * lax.rsqrt(ms + NORM_EPS)).T * sub_ref[...] * (1.0 - lambda_init)
        o_ref[...] = y.astype(o_ref.dtype)


def flash_attention(qk, vt, batch, seq, *, kv_heads, groups, k_slot0, k_stride, diff=False,
                    lambdas=None, subln=None, lambda_init=0.0, name):
    tk = vt.shape[-1]
    n_c = seq // tk
    tq = _tile(seq, ATTN_Q_ROWS // groups)
    n_q = seq // tq
    rows = groups * tq
    out_w = (1 if diff else groups) * LANES
    in_specs = [
        pl.BlockSpec((1, groups, tq, LANES), lambda b, h, i: (b, h, i, 0)),
        pl.BlockSpec((1, groups, tq, LANES), lambda b, h, i: (b, h, jnp.minimum(i + 1, n_q - 1), 0)),
        pl.BlockSpec((1, 1, seq, LANES), lambda b, h, i: (b, k_slot0 + k_stride * h, 0, 0)),
        pl.BlockSpec((1, 1, n_c, V_ROWS, tk), lambda b, h, i: (b, h, 0, 0, 0)),
    ]
    args = [qk, qk, qk, vt]
    if diff:
        in_specs += [pl.BlockSpec(lambdas.shape, lambda b, h, i: (0, 0)),
                     pl.BlockSpec((1, LANES), lambda b, h, i: (0, 0))]
        args += [lambdas, subln.reshape(1, LANES)]
    return pl.pallas_call(
        functools.partial(_flash_kernel, groups=groups, tq=tq, tk=tk, seq=seq, diff=diff, lambda_init=lambda_init),
        grid=(batch, kv_heads, n_q),
        in_specs=in_specs,
        out_specs=pl.BlockSpec((tq, out_w), lambda b, h, i: (b * n_q + i, h)),
        out_shape=jax.ShapeDtypeStruct((batch * seq, kv_heads * out_w), BF16),
        scratch_shapes=[pltpu.VMEM((LANES, rows), BF16), pltpu.VMEM((tk, rows), F32), pltpu.VMEM((tk, rows), F32),
                        pltpu.VMEM((1, rows), F32), pltpu.VMEM((1, rows), F32),
                        pltpu.VMEM((1, rows), F32), pltpu.VMEM((V_ROWS, rows), F32)],
        compiler_params=_cparams(("parallel", "parallel", "arbitrary")),
        name=name,
    )(*args)


def _rope_angles(pos, dim, theta):
    inv = theta ** (-jnp.arange(0, dim, 2, dtype=F32) / dim)
    return pos[:, None] * inv[None, :]


def _axial_tables(seq):
    t = jnp.arange(seq)
    row = (t // GRID_W).astype(F32)
    col = (t % GRID_W).astype(F32)
    half = DB // 2
    ang_r = _rope_angles(row, half, AXIAL_THETA)
    ang_c = _rope_angles(col, half, AXIAL_THETA)
    ang = jnp.concatenate([ang_r, ang_r, ang_c, ang_c], axis=-1)
    cos, sin = jnp.cos(ang), jnp.sin(ang)
    first = (jnp.arange(LANES) % half) < (half // 2)
    return cos, jnp.where(first, -sin, 0.0), jnp.where(first, 0.0, sin)


def _partial_tables(seq):
    pos = jnp.arange(seq, dtype=F32)
    ang8 = _rope_angles(pos, ROT_C, ROPE_THETA)
    lane = jnp.arange(LANES) % DC
    ang = jnp.take(ang8, lane % (ROT_C // 2), axis=1)
    rot = lane < ROT_C
    first = lane < ROT_C // 2
    cos = jnp.where(rot, jnp.cos(ang), 1.0)
    sin = jnp.where(rot, jnp.sin(ang), 0.0)
    return cos, jnp.where(first, -sin, 0.0), jnp.where(first, 0.0, sin)


def _permute_w_in(w):
    k = w.shape[0]
    alpha0 = 4 * A_WIDTH
    beta0 = alpha0 + 2 * HA
    blocks = []
    for d in range(2):
        for g in range(HA // GDN_HEAD_GROUP):
            a0 = alpha0 + d * HA + g * GDN_HEAD_GROUP
            b0 = beta0 + d * HA + g * GDN_HEAD_GROUP
            blocks += [w[:, a0:a0 + GDN_HEAD_GROUP], w[:, b0:b0 + GDN_HEAD_GROUP],
                       jnp.zeros((k, LANES - 2 * GDN_HEAD_GROUP), w.dtype)]
    return jnp.concatenate([w[:, :alpha0]] + blocks + [w[:, beta0 + 2 * HA:]], axis=1)


def _gdn_param_row(p):
    blocks = []
    for d in range(2):
        for g in range(HA // GDN_HEAD_GROUP):
            blocks += [p[d, g * GDN_HEAD_GROUP:(g + 1) * GDN_HEAD_GROUP], jnp.zeros((LANES - GDN_HEAD_GROUP,), p.dtype)]
    return jnp.concatenate(blocks).reshape(1, AB_WIDTH)


def _encoder_layer(x, p_emb, w, li, batch, seq, tabs_b, tabs_c, tag):
    lambda_init = 0.8 - 0.6 * math.exp(-0.3 * li)
    proj = norm_matmul(x, w['norm_mix'], w['w_in'], f"in_proj_{tag}")

    q_a, k_a, v_a, gb = gdn_prep(proj, w['conv_a'], w['alog_row'], w['dtb_row'], batch, seq, f"gdn_prep_{tag}")
    o2 = gdn_scan(q_a, k_a, v_a, gb, batch, seq, f"gdn_scan_{tag}")

    tk = _tile(seq, ATTN_KV_CHUNK)
    qk_b, vt_b = qkv_prep(proj, tabs_b, w['qn_b'], w['kn_b'], batch, seq, col_off=OFF_BQ, n_q=HB, n_k=KVB, n_v=KVB,
                          seg=DB, shift=DB // 4, q_scale=DB ** -0.5 * LOG2_E, split=False, tk=tk,
                          name=f"prep_b_{tag}")
    y_b = flash_attention(qk_b, vt_b, batch, seq, kv_heads=KVB, groups=HB // KVB, k_slot0=HB, k_stride=1,
                          name=f"attn_b_{tag}")

    qk_c, vt_c = qkv_prep(proj, tabs_c, w['qn_c2'], w['kn_c2'], batch, seq, col_off=OFF_CQ, n_q=HC, n_k=HC, n_v=HC,
                          seg=DC, shift=ROT_C // 2, q_scale=DC ** -0.5 * LOG2_E, split=True, tk=tk,
                          name=f"prep_c_{tag}")
    y_c = flash_attention(qk_c, vt_c, batch, seq, kv_heads=HC, groups=2, k_slot0=2 * HC, k_stride=1,
                          diff=True, lambdas=w['lambdas_c'], subln=w['subln_c'],
                          lambda_init=lambda_init, name=f"attn_c_{tag}")

    x, h_ffn = merge_out(o2, w['norm_a'], (y_b, y_c), proj, (w['w_o_a'], w['w_o_b'], w['w_o_c']), w['w_out'], x,
                         w['norm_ffn'], f"merge_out_{tag}")

    u = matmul_bf16(h_ffn, w['w_up'], f"ffn_up_{tag}")
    return ffn_tail(u, w['conv_ffn'], w['conv_ffn_b'], w['w_down'], x, p_emb, w['w_ple_gate'], w['w_ple'],
                    batch, seq, f"ffn_tail_{tag}")


def kernel(x_prompt, x_sample, p_prompt, p_sample, norm_mix, w_in, conv_a, a_log, dt_bias, norm_a, qn_b, kn_b, qn_c, kn_c, lambdas_c, subln_c, w_o_a, w_o_b, w_o_c, w_out, norm_ffn, w_up, conv_ffn, conv_ffn_b, w_down, w_ple, w_ple_gate):
    depth = w_in.shape[0]
    layers = []
    for i in range(depth):
        layers.append(dict(
            norm_mix=norm_mix[i], w_in=_permute_w_in(w_in[i]).astype(BF16), conv_a=conv_a[i],
            alog_row=_gdn_param_row(a_log[i]), dtb_row=_gdn_param_row(dt_bias[i]), norm_a=norm_a[i],
            qn_b=qn_b[i], kn_b=kn_b[i], qn_c2=jnp.tile(qn_c[i], 2), kn_c2=jnp.tile(kn_c[i], 2),
            lambdas_c=lambdas_c[i], subln_c=subln_c[i],
            w_o_a=w_o_a[i].astype(BF16), w_o_b=w_o_b[i].astype(BF16), w_o_c=w_o_c[i].astype(BF16),
            w_out=w_out[i].astype(BF16), norm_ffn=norm_ffn[i], w_up=w_up[i].astype(BF16),
            conv_ffn=conv_ffn[i], conv_ffn_b=conv_ffn_b[i], w_down=w_down[i].astype(BF16),
            w_ple=w_ple[i].astype(BF16), w_ple_gate=w_ple_gate[i].astype(BF16)))

    outs = []
    for tag, x, p in (("p", x_prompt, p_prompt), ("s", x_sample, p_sample)):
        batch, seq, d = x.shape
        tabs_b = _axial_tables(seq)
        tabs_c = _partial_tables(seq)
        h = x.reshape(batch * seq, d)
        for i in range(depth):
            h = _encoder_layer(h, p[i].reshape(batch * seq, -1), layers[i], i, batch, seq, tabs_b, tabs_c, f"{tag}{i}")
        outs.append(h.reshape(batch, seq, d))
    return tuple(outs)
```

```python
import functools
import math

import jax
import jax.numpy as jnp
from jax import lax
from jax.experimental import pallas as pl
from jax.experimental.pallas import tpu as pltpu

F32 = jnp.float32
BF16 = jnp.bfloat16
HIGHEST = lax.Precision.HIGHEST

GRID_W = 64
NORM_EPS = 1e-6
HA = 8
DA = 128
A_WIDTH = HA * DA
DELTA_CHUNK = 64
HB = 8
KVB = 2
DB = 128
AXIAL_THETA = 10000.0
HC = 8
DC = 64
ROT_C = DC // 4
ROPE_THETA = 500000.0

LANES = 128
SUBLANES = 8
VMEM_LIMIT_BYTES = 56 * 1024 * 1024

OFF_Z = 3072
OFF_AB = 4096
AB_WIDTH = 512
OFF_BQ = 4608
OFF_BV = 5888
OFF_CQ = 6144
OFF_CV = 8192
OFF_GATE = 9216
GDN_HEAD_GROUP = 4
GDN_BLOCK = 512
ATTN_KV_CHUNK = 1024
ATTN_Q_ROWS = 2048
FLASH_UNROLL = 2
V_ROWS = LANES + 16
LOG2_E = math.log2(math.e)


def _cparams(semantics):
    return pltpu.CompilerParams(dimension_semantics=semantics, vmem_limit_bytes=VMEM_LIMIT_BYTES)


def _tile(n, pref):
    t = min(n, pref)
    while n % t:
        t //= 2
    return t


def _sigmoid(x):
    return 1.0 / (1.0 + jnp.exp(-x))


def _softplus(x):
    return jnp.maximum(x, 0.0) + jnp.log(1.0 + jnp.exp(-jnp.abs(x)))


def _norm_matmul_kernel(x_ref, g_ref, w_ref, o_ref, h_ref, *, rows):
    @pl.when(pl.program_id(1) == 0)
    def _():
        def chunk(r, c):
            rs = pl.ds(pl.multiple_of(r * rows, rows), rows)
            x = x_ref[rs, :]
            ms = jnp.mean(x * x, axis=-1, keepdims=True)
            h_ref[rs, :] = (x * lax.rsqrt(ms + NORM_EPS) * g_ref[...]).astype(BF16)
            return c
        lax.fori_loop(0, x_ref.shape[0] // rows, chunk, 0)

    o_ref[...] = jnp.dot(h_ref[...], w_ref[...], preferred_element_type=F32).astype(o_ref.dtype)


def norm_matmul(x, g, w, name):
    m, k = x.shape
    n = w.shape[1]
    tm = _tile(m, 1024)
    tn = 1536 if n % 1536 == 0 else _tile(n, 1024)
    rows = _tile(tm, 128)
    return pl.pallas_call(
        functools.partial(_norm_matmul_kernel, rows=rows),
        grid=(m // tm, n // tn),
        in_specs=[
            pl.BlockSpec((tm, k), lambda i, j: (i, 0)),
            pl.BlockSpec((1, k), lambda i, j: (0, 0)),
            pl.BlockSpec((k, tn), lambda i, j: (0, j)),
        ],
        out_specs=pl.BlockSpec((tm, tn), lambda i, j: (i, j)),
        out_shape=jax.ShapeDtypeStruct((m, n), BF16),
        scratch_shapes=[pltpu.VMEM((tm, k), BF16)],
        compiler_params=_cparams(("parallel", "arbitrary")),
        name=name,
    )(x, g.reshape(1, k), w)


def _resident(shape):
    return pl.BlockSpec(shape, lambda *_: (0,) * len(shape), pipeline_mode=pl.Buffered(1))


def _merge_out_kernel(*refs, n_branch, n_chunk, tc):
    o2_ref, z_ref, na_ref = refs[:3]
    refs = refs[3:]
    y_refs = refs[:n_branch - 1]
    g_refs = refs[n_branch - 1:n_branch - 1 + n_branch * n_chunk]
    w_refs = refs[n_branch - 1 + n_branch * n_chunk:2 * n_branch - 1 + n_branch * n_chunk]
    wo_ref, x_ref, gn_ref, o_ref, h_ref, m_ref, ya_ref = refs[2 * n_branch - 1 + n_branch * n_chunk:]
    _gdn_gate_norm(o2_ref, z_ref, na_ref, ya_ref)
    y_refs = (ya_ref,) + tuple(y_refs)
    for c in range(n_chunk):
        cols = slice(c * tc, (c + 1) * tc)
        acc = None
        for i in range(n_branch):
            gate = _sigmoid(g_refs[i * n_chunk + c][...].astype(F32))
            term = gate * jnp.dot(y_refs[i][...], w_refs[i][:, cols], preferred_element_type=F32)
            acc = term if acc is None else acc + term
        m_ref[:, cols] = acc.astype(BF16)
    x1 = x_ref[...] + jnp.dot(m_ref[...], wo_ref[...], preferred_element_type=F32)
    o_ref[...] = x1
    ms = jnp.mean(x1 * x1, axis=-1, keepdims=True)
    h_ref[...] = (x1 * lax.rsqrt(ms + NORM_EPS) * gn_ref[...]).astype(h_ref.dtype)


def merge_out(o2, norm_a, ys, proj, w_os, w_out, x, g_next, name):
    m, k = ys[0].shape
    n = w_out.shape[1]
    tm = _tile(m, 256)
    tc = _tile(n, 1024)
    n_chunk = n // tc
    n_branch = len(ys) + 1
    gate_blk = OFF_GATE // tc
    row = lambda i: (i, 0)
    g_specs = [pl.BlockSpec((tm, tc), lambda i, b=b: (i, gate_blk + b)) for b in range(n_branch * n_chunk)]
    return pl.pallas_call(
        functools.partial(_merge_out_kernel, n_branch=n_branch, n_chunk=n_chunk, tc=tc),
        grid=(m // tm,),
        in_specs=([pl.BlockSpec((2, tm, k), lambda i: (0, i, 0)), pl.BlockSpec((tm, k), lambda i: (i, OFF_Z // k)),
                   _resident((1, DA))]
                  + [pl.BlockSpec((tm, k), row)] * len(ys) + g_specs + [_resident((k, n))] * n_branch
                  + [_resident((n, n)), pl.BlockSpec((tm, n), row), _resident((1, n))]),
        out_specs=[pl.BlockSpec((tm, n), row), pl.BlockSpec((tm, n), row)],
        out_shape=[jax.ShapeDtypeStruct((m, n), F32), jax.ShapeDtypeStruct((m, n), BF16)],
        scratch_shapes=[pltpu.VMEM((tm, n), BF16), pltpu.VMEM((tm, k), BF16)],
        compiler_params=_cparams(("parallel",)),
        name=name,
    )(o2, proj, norm_a.reshape(1, DA), *ys, *([proj] * (n_branch * n_chunk)), *w_os, w_out, x, g_next.reshape(1, n))


def _matmul_kernel(a_ref, w_ref, o_ref):
    o_ref[...] = jnp.dot(a_ref[...], w_ref[...], preferred_element_type=F32).astype(o_ref.dtype)


def matmul_bf16(a, w, name):
    m, k = a.shape
    n = w.shape[1]
    tm = _tile(m, 1024)
    tn = _tile(n, 1024)
    return pl.pallas_call(
        _matmul_kernel,
        grid=(m // tm, n // tn),
        in_specs=[pl.BlockSpec((tm, k), lambda i, j: (i, 0)), pl.BlockSpec((k, tn), lambda i, j: (0, j))],
        out_specs=pl.BlockSpec((tm, tn), lambda i, j: (i, j)),
        out_shape=jax.ShapeDtypeStruct((m, n), BF16),
        compiler_params=_cparams(("parallel", "arbitrary")),
        name=name,
    )(a, w)


def _shift_mats(ts):
    r = lax.broadcasted_iota(jnp.int32, (ts, ts), 0)
    c = lax.broadcasted_iota(jnp.int32, (ts, ts), 1)
    return (c == r - 1).astype(BF16), (c == r + 1).astype(BF16)


def _conv3(xb, prev_row, next_row, w_ref, lanes, shifts):
    ts = xb.shape[0]
    w0, w1, w2 = w_ref[0:1, lanes], w_ref[1:2, lanes], w_ref[2:3, lanes]
    xp = jnp.dot(shifts[0], xb, preferred_element_type=F32)
    xn = jnp.dot(shifts[1], xb, preferred_element_type=F32)
    y = xp * w0 + xb.astype(F32) * w1 + xn * w2
    r8 = lax.broadcasted_iota(jnp.int32, (SUBLANES, 1), 0)
    first = y[0:SUBLANES] + jnp.where(r8 == 0, prev_row, 0.0) * w0
    last = y[ts - SUBLANES:] + jnp.where(r8 == SUBLANES - 1, next_row, 0.0) * w2
    return jnp.concatenate([first, y[SUBLANES:ts - SUBLANES], last], axis=0)


def _halo_specs(batch, seq, ts, width, col_fn):
    n_s = seq // ts
    per_seq = seq // SUBLANES
    per_tile = ts // SUBLANES
    last = batch * per_seq - 1

    def prev_map(b, i, *rest):
        return (jnp.maximum(b * per_seq + i * per_tile - 1, 0), col_fn(*rest))

    def next_map(b, i, *rest):
        return (jnp.minimum(b * per_seq + (i + 1) * per_tile, last), col_fn(*rest))

    return pl.BlockSpec((SUBLANES, width), prev_map), pl.BlockSpec((SUBLANES, width), next_map), n_s


def _conv3_roll(x, prev_row, next_row, w_ref, lanes):
    ts = x.shape[0]
    r = lax.broadcasted_iota(jnp.int32, (ts, 1), 0)
    xp = jnp.where(r == 0, prev_row, pltpu.roll(x, 1, 0))
    xn = jnp.where(r == ts - 1, next_row, pltpu.roll(x, ts - 1, 0))
    return xp * w_ref[0:1, lanes] + x * w_ref[1:2, lanes] + xn * w_ref[2:3, lanes]


def _ffn_tail_kernel(u_ref, up_ref, un_ref, cw_ref, cb_ref, wd_ref, x_ref, p_ref, wg_ref, we_ref, o_ref, *, n_s, f, tc):
    i = pl.program_id(1)
    has_prev = (i > 0).astype(F32)
    has_next = (i < n_s - 1).astype(F32)

    def conv(lanes):
        prev_row = up_ref[SUBLANES - 1:SUBLANES, lanes].astype(F32) * has_prev
        next_row = un_ref[0:1, lanes].astype(F32) * has_next
        return _conv3_roll(u_ref[:, lanes].astype(F32), prev_row, next_row, cw_ref, lanes) + cb_ref[:, lanes]

    x2 = x_ref[...]
    for c in range(f // tc):
        gate = conv(slice(c * tc, (c + 1) * tc))
        val = conv(slice(f + c * tc, f + (c + 1) * tc))
        act = (gate * _sigmoid(gate) * val).astype(BF16)
        x2 = x2 + jnp.dot(act, wd_ref[c * tc:(c + 1) * tc, :], preferred_element_type=F32)
    gate = _sigmoid(jnp.dot(x2.astype(BF16), wg_ref[...], preferred_element_type=F32))
    emb = jnp.dot(p_ref[...].astype(BF16), we_ref[...], preferred_element_type=F32)
    o_ref[...] = x2 + gate * emb


def ffn_tail(u, conv_w, conv_b, w_down, x, p, wg, we, batch, seq, name):
    t, two_f = u.shape
    f = two_f // 2
    n = w_down.shape[1]
    kp = p.shape[1]
    tm = _tile(seq, 256)
    prev_s, next_s, n_s = _halo_specs(batch, seq, tm, two_f, lambda: 0)
    row = lambda b, i: (b * n_s + i, 0)
    return pl.pallas_call(
        functools.partial(_ffn_tail_kernel, n_s=n_s, f=f, tc=256),
        grid=(batch, n_s),
        in_specs=[pl.BlockSpec((tm, two_f), row), prev_s, next_s, _resident((3, two_f)), _resident((1, two_f)),
                  _resident((f, n)), pl.BlockSpec((tm, n), row), pl.BlockSpec((tm, kp), row),
                  _resident((n, n)), _resident((kp, n))],
        out_specs=pl.BlockSpec((tm, n), row),
        out_shape=jax.ShapeDtypeStruct((t, n), F32),
        compiler_params=_cparams(("parallel", "parallel")),
        name=name,
    )(u, u, u, conv_w, conv_b.reshape(1, two_f), w_down, x, p, wg, we)


def _gdn_prep_kernel(x_ref, xp_ref, xn_ref, ab_ref, w_ref, alog_ref, dtb_ref, q_ref, k_ref, v_ref, gb_ref, *, n_s):
    i = pl.program_id(1)
    has_prev = (i > 0).astype(F32)
    has_next = (i < n_s - 1).astype(F32)
    shifts = _shift_mats(x_ref.shape[0])
    for c in range(3 * HA):
        lanes = slice(c * DA, (c + 1) * DA)
        prev_row = xp_ref[SUBLANES - 1:SUBLANES, lanes].astype(F32) * has_prev
        next_row = xn_ref[0:1, lanes].astype(F32) * has_next
        y = _conv3(x_ref[:, lanes], prev_row, next_row, w_ref, lanes, shifts)
        y = y * _sigmoid(y)
        out_lanes = slice((c % HA) * DA, (c % HA + 1) * DA)
        if c < 2 * HA:
            y = y * lax.rsqrt(jnp.sum(y * y, axis=-1, keepdims=True) + NORM_EPS)
            (q_ref if c < HA else k_ref)[:, out_lanes] = y.astype(BF16)
        else:
            v_ref[:, out_lanes] = y.astype(BF16)
    ab = ab_ref[...].astype(F32)
    lane = lax.broadcasted_iota(jnp.int32, (1, AB_WIDTH), 1) % LANES
    g = -jnp.exp(alog_ref[...]) * _softplus(ab + dtb_ref[...])
    gb_ref[...] = jnp.where(lane < GDN_HEAD_GROUP, g, _sigmoid(ab))


def gdn_prep(proj, conv_w, alog_row, dtb_row, batch, seq, name):
    t = proj.shape[0]
    ts = _tile(seq, 256)
    w3 = 3 * A_WIDTH
    prev_s, next_s, n_s = _halo_specs(batch, seq, ts, w3, lambda: 0)
    row = lambda b, i: (b * n_s + i, 0)
    out_spec = pl.BlockSpec((ts, A_WIDTH), row)
    return pl.pallas_call(
        functools.partial(_gdn_prep_kernel, n_s=n_s),
        grid=(batch, n_s),
        in_specs=[
            pl.BlockSpec((ts, w3), row), prev_s, next_s,
            pl.BlockSpec((ts, AB_WIDTH), lambda b, i: (b * n_s + i, OFF_AB // AB_WIDTH)),
            pl.BlockSpec((3, w3), lambda b, i: (0, 0)),
            pl.BlockSpec((1, AB_WIDTH), lambda b, i: (0, 0)),
            pl.BlockSpec((1, AB_WIDTH), lambda b, i: (0, 0)),
        ],
        out_specs=[out_spec, out_spec, out_spec, pl.BlockSpec((ts, AB_WIDTH), row)],
        out_shape=[jax.ShapeDtypeStruct((t, A_WIDTH), BF16)] * 3 + [jax.ShapeDtypeStruct((t, AB_WIDTH), F32)],
        compiler_params=_cparams(("parallel", "parallel")),
        name=name,
    )(proj, proj, proj, proj, conv_w, alog_row, dtb_row)


def _gdn_scan_kernel(q_ref, k_ref, v_ref, gb_ref, o_ref, s_ref, *, n_chunks):
    c_len = DELTA_CHUNK
    d = pl.program_id(1)

    @pl.when(pl.program_id(3) == 0)
    def _():
        s_ref[...] = jnp.zeros_like(s_ref)

    sgn = 1 - 2 * d
    row = lax.broadcasted_iota(jnp.int32, (c_len, c_len), 0)
    col = lax.broadcasted_iota(jnp.int32, (c_len, c_len), 1)
    order = (row - col) * sgn
    incl = order >= 0
    strict = order > 0
    cum_mat = incl.astype(F32)
    eye = (row == col).astype(F32)
    sel = (lax.broadcasted_iota(jnp.int32, (SUBLANES, LANES), 0)
           == lax.broadcasted_iota(jnp.int32, (SUBLANES, LANES), 1)).astype(F32)
    scale = DA ** -0.5
    nt = (((1,), (1,)), ((), ()))
    tn = (((0,), (0,)), ((), ()))

    heads = range(GDN_HEAD_GROUP)
    chunk_rows, gcols, grows, exp_gs, exp_rests, exp_tots, gblks = [], [], [], [], [], [], []
    for j in range(n_chunks):
        c = j + d * (n_chunks - 1 - 2 * j)
        rows = pl.ds(pl.multiple_of(c * c_len, c_len), c_len)
        gblk = gb_ref[rows, :]
        gcol = jnp.dot(cum_mat, gblk, precision=HIGHEST, preferred_element_type=F32)
        chunk_rows.append(rows)
        gblks.append(gblk)
        gcols.append(gcol)
    for j in range(n_chunks):
        grows.append(lax.dot_general(sel, gcols[j], nt, precision=HIGHEST, preferred_element_type=F32))
        gtot = jnp.sum(gblks[j], axis=0, keepdims=True)
        exp_gs.append(jnp.exp(gcols[j]))
        exp_rests.append(jnp.exp(gtot - gcols[j]))
        exp_tots.append(jnp.exp(gtot))

    units = [(j, h) for j in range(n_chunks) for h in heads]
    low, intra, rhs = {}, {}, {}
    for (j, h) in units:
        rows = chunk_rows[j]
        lanes = slice(h * DA, (h + 1) * DA)
        kbf = k_ref[rows, lanes]
        kf = kbf.astype(F32)
        beta = gblks[j][:, GDN_HEAD_GROUP + h:GDN_HEAD_GROUP + h + 1]
        diff = gcols[j][:, h:h + 1] - grows[j][h:h + 1, :]
        decay = jnp.where(incl, jnp.exp(jnp.where(incl, diff, 0.0)), 0.0)
        kb = kf * beta
        kk = lax.dot_general(kb.astype(BF16), kbf, nt, preferred_element_type=F32)
        qs = (q_ref[rows, lanes].astype(F32) * scale).astype(BF16)
        qk = lax.dot_general(qs, kbf, nt, preferred_element_type=F32)
        low[j, h] = jnp.where(strict, kk * decay, 0.0)
        intra[j, h] = jnp.where(incl, qk * decay, 0.0).astype(BF16)
        rhs[j, h] = jnp.concatenate([v_ref[rows, lanes].astype(F32) * beta, kb * exp_gs[j][:, h:h + 1]],
                                    axis=1).astype(BF16)

    def square(p):
        return {u: jnp.dot(p[u].astype(BF16), p[u].astype(BF16), preferred_element_type=F32) for u in units}

    def pair(first, p_second):
        return {u: first[u] + jnp.dot(first[u].astype(BF16), p_second[u].astype(BF16), preferred_element_type=F32)
                for u in units}

    p2 = square(low)
    p4 = square(p2)
    fac_a = pair({u: eye - low[u] for u in units}, p2)
    p8 = square(p4)
    p16 = square(p8)
    fac_b = pair({u: eye + p4[u] for u in units}, p8)
    p32 = square(p16)
    fac_ab = {u: jnp.dot(fac_a[u].astype(BF16), fac_b[u].astype(BF16), preferred_element_type=F32) for u in units}
    fac_c = pair({u: eye + p16[u] for u in units}, p32)
    inv = {u: jnp.dot(fac_ab[u].astype(BF16), fac_c[u].astype(BF16), preferred_element_type=F32) for u in units}
    uw = {u: jnp.dot(inv[u].astype(BF16), rhs[u], preferred_element_type=F32).astype(BF16) for u in units}

    chain_lhs, out_local, state_add = {}, {}, {}
    for (j, h) in units:
        rows = chunk_rows[j]
        lanes = slice(h * DA, (h + 1) * DA)
        k_dec = (k_ref[rows, lanes].astype(F32) * exp_rests[j][:, h:h + 1]).astype(BF16)
        k_uw = lax.dot_general(k_dec, uw[j, h], tn, preferred_element_type=F32)
        i_uw = jnp.dot(intra[j, h], uw[j, h], preferred_element_type=F32)
        q_dec = q_ref[rows, lanes].astype(F32) * scale * exp_gs[j][:, h:h + 1]
        chain_lhs[j, h] = jnp.concatenate([q_dec - i_uw[:, DA:], k_uw[:, DA:]], axis=0).astype(BF16)
        out_local[j, h] = i_uw[:, :DA]
        state_add[j, h] = k_uw[:, :DA]

    for j in range(n_chunks):
        rows = chunk_rows[j]
        for h in heads:
            lanes = slice(h * DA, (h + 1) * DA)
            state = s_ref[h]
            prod = jnp.dot(chain_lhs[j, h], state.astype(BF16), preferred_element_type=F32)
            s_ref[h] = state * exp_tots[j][:, h:h + 1] - prod[c_len:] + state_add[j, h]
            o_ref[0, rows, lanes] = (prod[:c_len] + out_local[j, h]).astype(o_ref.dtype)


def gdn_scan(q, k, v, gb, batch, seq, name):
    t = q.shape[0]
    blk = _tile(seq, GDN_BLOCK)
    n_blk = seq // blk
    width = GDN_HEAD_GROUP * DA
    n_hg = HA // GDN_HEAD_GROUP

    def rblk(b, d, i):
        return b * n_blk + i + d * (n_blk - 1 - 2 * i)

    qkv_spec = pl.BlockSpec((blk, width), lambda b, d, g, i: (rblk(b, d, i), g))
    return pl.pallas_call(
        functools.partial(_gdn_scan_kernel, n_chunks=blk // DELTA_CHUNK),
        grid=(batch, 2, n_hg, n_blk),
        in_specs=[qkv_spec, qkv_spec, qkv_spec,
                  pl.BlockSpec((blk, LANES), lambda b, d, g, i: (rblk(b, d, i), d * n_hg + g))],
        out_specs=pl.BlockSpec((1, blk, width), lambda b, d, g, i: (d, rblk(b, d, i), g)),
        out_shape=jax.ShapeDtypeStruct((2, t, A_WIDTH), BF16),
        scratch_shapes=[pltpu.VMEM((GDN_HEAD_GROUP, DA, DA), F32)],
        compiler_params=_cparams(("parallel", "parallel", "parallel", "arbitrary")),
        name=name,
    )(q, k, v, gb)


def _gdn_gate_norm(o_ref, z_ref, g_ref, y_ref):
    for h in range(HA):
        lanes = slice(h * DA, (h + 1) * DA)
        o = o_ref[0, :, lanes].astype(F32) + o_ref[1, :, lanes].astype(F32)
        z = z_ref[:, lanes].astype(F32)
        ms = jnp.mean(o * o, axis=-1, keepdims=True)
        y = o * lax.rsqrt(ms + NORM_EPS) * g_ref[...]
        y_ref[:, lanes] = (y * (z * _sigmoid(z))).astype(y_ref.dtype)


def _qkv_prep_kernel(x_ref, cos_ref, sa_ref, sb_ref, gq_ref, gk_ref, qk_ref, vt_ref, *, n_q, n_k, n_v, seg, shift,
                     q_scale, split):
    grp_r = lax.broadcasted_iota(jnp.int32, (LANES, LANES), 0) // seg
    grp_c = lax.broadcasted_iota(jnp.int32, (LANES, LANES), 1) // seg
    seg_ones = (grp_r == grp_c).astype(BF16)
    lane = lax.broadcasted_iota(jnp.int32, (1, LANES), 1)
    cos, sin_a, sin_b = cos_ref[...], sa_ref[...], sb_ref[...]
    per = 2 if split else 1
    for h in range(n_q + n_k):
        is_q = h < n_q
        x = x_ref[:, h * LANES:(h + 1) * LANES].astype(F32)
        sq = x * x
        hi = sq.astype(BF16)
        lo = (sq - hi.astype(F32)).astype(BF16)
        ssum = (jnp.dot(hi, seg_ones, preferred_element_type=F32) + jnp.dot(lo, seg_ones, preferred_element_type=F32))
        gain = (gq_ref if is_q else gk_ref)[...]
        y = x * lax.rsqrt(ssum * (1.0 / seg) + NORM_EPS) * gain
        y = y * cos + pltpu.roll(y, LANES - shift, 1) * sin_a + pltpu.roll(y, shift, 1) * sin_b
        if is_q:
            y = y * q_scale
        if is_q and split:
            qk_ref[0, per * h] = jnp.where(lane < seg, y, 0.0).astype(qk_ref.dtype)
            qk_ref[0, per * h + 1] = jnp.where(lane >= seg, y, 0.0).astype(qk_ref.dtype)
        else:
            slot = per * h if is_q else per * n_q + (h - n_q)
            qk_ref[0, slot] = y.astype(qk_ref.dtype)
    for h in range(n_v):
        c0 = (n_q + n_k + h) * LANES
        vt_ref[0, h, 0, 0:LANES, :] = x_ref[:, c0:c0 + LANES].astype(F32).T.astype(vt_ref.dtype)
        vt_ref[0, h, 0, LANES:V_ROWS, :] = jnp.ones((V_ROWS - LANES, vt_ref.shape[-1]), vt_ref.dtype)


def qkv_prep(proj, tables, gq, gk, batch, seq, *, col_off, n_q, n_k, n_v, seg, shift, q_scale, split, tk, name):
    ts = tk
    n_s = seq // ts
    width = (n_q + n_k + n_v) * LANES
    slots = (2 if split else 1) * n_q + n_k
    tab = pl.BlockSpec((ts, LANES), lambda b, i: (i, 0))
    gain = pl.BlockSpec((1, LANES), lambda b, i: (0, 0))
    return pl.pallas_call(
        functools.partial(_qkv_prep_kernel, n_q=n_q, n_k=n_k, n_v=n_v, seg=seg, shift=shift, q_scale=q_scale,
                          split=split),
        grid=(batch, n_s),
        in_specs=[pl.BlockSpec((ts, width), lambda b, i: (b * n_s + i, col_off // width)), tab, tab, tab, gain, gain],
        out_specs=[pl.BlockSpec((1, slots, ts, LANES), lambda b, i: (b, 0, i, 0)),
                   pl.BlockSpec((1, n_v, 1, V_ROWS, tk), lambda b, i: (b, 0, i, 0, 0))],
        out_shape=[jax.ShapeDtypeStruct((batch, slots, seq, LANES), BF16),
                   jax.ShapeDtypeStruct((batch, n_v, n_s, V_ROWS, tk), BF16)],
        compiler_params=_cparams(("parallel", "parallel")),
        name=name,
    )(proj, tables[0], tables[1], tables[2], gq.reshape(1, LANES), gk.reshape(1, LANES))


def _flash_kernel(*refs, groups, tq, tk, seq, diff, lambda_init):
    if diff:
        (q_ref, qn_ref, k_ref, vt_ref, lam_ref, sub_ref, o_ref,
         qt_ref, sa_ref, sb_ref, ma_ref, mb_ref, m_ref, acc_ref) = refs
    else:
        q_ref, qn_ref, k_ref, vt_ref, o_ref, qt_ref, sa_ref, sb_ref, ma_ref, mb_ref, m_ref, acc_ref = refs
    rows = groups * tq
    n_c = seq // tk
    carry_over = n_c % 2 == 0

    def load_q(ref):
        qt_ref[...] = ref[0].reshape(rows, LANES).astype(F32).T.astype(BF16)

    m_ref[...] = jnp.full(m_ref.shape, -jnp.inf, F32)
    acc_ref[...] = jnp.zeros(acc_ref.shape, F32)
    buf_a = (sa_ref, ma_ref)
    buf_b = (sb_ref, mb_ref)

    def scores_into(c, buf):
        ks = pl.ds(pl.multiple_of(c * tk, tk), tk)
        s_t = jnp.dot(k_ref[0, 0, ks, :], qt_ref[...], preferred_element_type=F32)
        buf[0][...] = s_t
        buf[1][...] = jnp.max(s_t, axis=0, keepdims=True)

    def step(c, cur, nxt, wrap=False):
        if wrap:
            load_q(qn_ref)
            scores_into(0, nxt)
        elif nxt is not None:
            scores_into(c + 1, nxt)
        m_prev = m_ref[...]
        m_new = jnp.maximum(m_prev, cur[1][...])
        p_t = jnp.exp2(cur[0][...] - m_new)
        alpha = jnp.exp2(m_prev - m_new)
        acc_ref[...] = alpha * acc_ref[...] + jnp.dot(vt_ref[0, 0, c], p_t.astype(BF16),
                                                      preferred_element_type=F32)
        m_ref[...] = m_new

    def cold_start():
        load_q(q_ref)
        scores_into(0, buf_a)

    if carry_over:
        pl.when(pl.program_id(2) == 0)(cold_start)
    else:
        cold_start()
    bufs = (buf_a, buf_b)
    n_loop = (n_c - 1) // FLASH_UNROLL

    def body(i, carry):
        for u in range(FLASH_UNROLL):
            step(FLASH_UNROLL * i + u, bufs[u % 2], bufs[(u + 1) % 2])
        return carry

    lax.fori_loop(0, n_loop, body, 0)
    for c in range(FLASH_UNROLL * n_loop, n_c):
        last = c + 1 == n_c
        step(c, bufs[c % 2], bufs[(c + 1) % 2] if (not last or carry_over) else None, wrap=last and carry_over)
    o_t = acc_ref[0:LANES, :] / acc_ref[LANES:LANES + 1, :]
    if not diff:
        o = o_t.T
        for g in range(groups):
            o_ref[:, g * LANES:(g + 1) * LANES] = o[g * tq:(g + 1) * tq].astype(o_ref.dtype)
    else:
        lf = lam_ref[...]
        lam = (jnp.exp(jnp.sum(lf[0:1] * lf[1:2], axis=-1, keepdims=True))
               - jnp.exp(jnp.sum(lf[2:3] * lf[3:4], axis=-1, keepdims=True)) + lambda_init)
        d_t = o_t[:, :tq] - lam * o_t[:, tq:]
        ms = jnp.mean(d_t * d_t, axis=0, keepdims=True)
        y = (d_t * lax.rsqrt(ms + NORM_EPS)).T * sub_ref[...] * (1.0 - lambda_init)
        o_ref[...] = y.astype(o_ref.dtype)


def flash_attention(qk, vt, batch, seq, *, kv_heads, groups, k_slot0, k_stride, diff=False,
                    lambdas=None, subln=None, lambda_init=0.0, name):
    tk = vt.shape[-1]
    n_c = seq // tk
    tq = _tile(seq, ATTN_Q_ROWS // groups)
    n_q = seq // tq
    rows = groups * tq
    out_w = (1 if diff else groups) * LANES
    in_specs = [
        pl.BlockSpec((1, groups, tq, LANES), lambda b, h, i: (b, h, i, 0)),
        pl.BlockSpec((1, groups, tq, LANES), lambda b, h, i: (b, h, jnp.minimum(i + 1, n_q - 1), 0)),
        pl.BlockSpec((1, 1, seq, LANES), lambda b, h, i: (b, k_slot0 + k_stride * h, 0, 0)),
        pl.BlockSpec((1, 1, n_c, V_ROWS, tk), lambda b, h, i: (b, h, 0, 0, 0)),
    ]
    args = [qk, qk, qk, vt]
    if diff:
        in_specs += [pl.BlockSpec(lambdas.shape, lambda b, h, i: (0, 0)),
                     pl.BlockSpec((1, LANES), lambda b, h, i: (0, 0))]
        args += [lambdas, subln.reshape(1, LANES)]
    return pl.pallas_call(
        functools.partial(_flash_kernel, groups=groups, tq=tq, tk=tk, seq=seq, diff=diff, lambda_init=lambda_init),
        grid=(batch, kv_heads, n_q),
        in_specs=in_specs,
        out_specs=pl.BlockSpec((tq, out_w), lambda b, h, i: (b * n_q + i, h)),
        out_shape=jax.ShapeDtypeStruct((batch * seq, kv_heads * out_w), BF16),
        scratch_shapes=[pltpu.VMEM((LANES, rows), BF16), pltpu.VMEM((tk, rows), F32), pltpu.VMEM((tk, rows), F32),
                        pltpu.VMEM((1, rows), F32), pltpu.VMEM((1, rows), F32),
                        pltpu.VMEM((1, rows), F32), pltpu.VMEM((V_ROWS, rows), F32)],
        compiler_params=_cparams(("parallel", "parallel", "arbitrary")),
        name=name,
    )(*args)


def _rope_angles(pos, dim, theta):
    inv = theta ** (-jnp.arange(0, dim, 2, dtype=F32) / dim)
    return pos[:, None] * inv[None, :]


def _axial_tables(seq):
    t = jnp.arange(seq)
    row = (t // GRID_W).astype(F32)
    col = (t % GRID_W).astype(F32)
    half = DB // 2
    ang_r = _rope_angles(row, half, AXIAL_THETA)
    ang_c = _rope_angles(col, half, AXIAL_THETA)
    ang = jnp.concatenate([ang_r, ang_r, ang_c, ang_c], axis=-1)
    cos, sin = jnp.cos(ang), jnp.sin(ang)
    first = (jnp.arange(LANES) % half) < (half // 2)
    return cos, jnp.where(first, -sin, 0.0), jnp.where(first, 0.0, sin)


def _partial_tables(seq):
    pos = jnp.arange(seq, dtype=F32)
    ang8 = _rope_angles(pos, ROT_C, ROPE_THETA)
    lane = jnp.arange(LANES) % DC
    ang = jnp.take(ang8, lane % (ROT_C // 2), axis=1)
    rot = lane < ROT_C
    first = lane < ROT_C // 2
    cos = jnp.where(rot, jnp.cos(ang), 1.0)
    sin = jnp.where(rot, jnp.sin(ang), 0.0)
    return cos, jnp.where(first, -sin, 0.0), jnp.where(first, 0.0, sin)


def _permute_w_in(w):
    k = w.shape[0]
    alpha0 = 4 * A_WIDTH
    beta0 = alpha0 + 2 * HA
    blocks = []
    for d in range(2):
        for g in range(HA // GDN_HEAD_GROUP):
            a0 = alpha0 + d * HA + g * GDN_HEAD_GROUP
            b0 = beta0 + d * HA + g * GDN_HEAD_GROUP
            blocks += [w[:, a0:a0 + GDN_HEAD_GROUP], w[:, b0:b0 + GDN_HEAD_GROUP],
                       jnp.zeros((k, LANES - 2 * GDN_HEAD_GROUP), w.dtype)]
    return jnp.concatenate([w[:, :alpha0]] + blocks + [w[:, beta0 + 2 * HA:]], axis=1)


def _gdn_param_row(p):
    blocks = []
    for d in range(2):
        for g in range(HA // GDN_HEAD_GROUP):
            blocks += [p[d, g * GDN_HEAD_GROUP:(g + 1) * GDN_HEAD_GROUP], jnp.zeros((LANES - GDN_HEAD_GROUP,), p.dtype)]
    return jnp.concatenate(blocks).reshape(1, AB_WIDTH)


def _encoder_layer(x, p_emb, w, li, batch, seq, tabs_b, tabs_c, tag):
    lambda_init = 0.8 - 0.6 * math.exp(-0.3 * li)
    proj = norm_matmul(x, w['norm_mix'], w['w_in'], f"in_proj_{tag}")

    q_a, k_a, v_a, gb = gdn_prep(proj, w['conv_a'], w['alog_row'], w['dtb_row'], batch, seq, f"gdn_prep_{tag}")
    o2 = gdn_scan(q_a, k_a, v_a, gb, batch, seq, f"gdn_scan_{tag}")

    tk = _tile(seq, ATTN_KV_CHUNK)
    qk_b, vt_b = qkv_prep(proj, tabs_b, w['qn_b'], w['kn_b'], batch, seq, col_off=OFF_BQ, n_q=HB, n_k=KVB, n_v=KVB,
                          seg=DB, shift=DB // 4, q_scale=DB ** -0.5 * LOG2_E, split=False, tk=tk,
                          name=f"prep_b_{tag}")
    y_b = flash_attention(qk_b, vt_b, batch, seq, kv_heads=KVB, groups=HB // KVB, k_slot0=HB, k_stride=1,
                          name=f"attn_b_{tag}")

    qk_c, vt_c = qkv_prep(proj, tabs_c, w['qn_c2'], w['kn_c2'], batch, seq, col_off=OFF_CQ, n_q=HC, n_k=HC, n_v=HC,
                          seg=DC, shift=ROT_C // 2, q_scale=DC ** -0.5 * LOG2_E, split=True, tk=tk,
                          name=f"prep_c_{tag}")
    y_c = flash_attention(qk_c, vt_c, batch, seq, kv_heads=HC, groups=2, k_slot0=2 * HC, k_stride=1,
                          diff=True, lambdas=w['lambdas_c'], subln=w['subln_c'],
                          lambda_init=lambda_init, name=f"attn_c_{tag}")

    x, h_ffn = merge_out(o2, w['norm_a'], (y_b, y_c), proj, (w['w_o_a'], w['w_o_b'], w['w_o_c']), w['w_out'], x,
                         w['norm_ffn'], f"merge_out_{tag}")

    u = matmul_bf16(h_ffn, w['w_up'], f"ffn_up_{tag}")
    return ffn_tail(u, w['conv_ffn'], w['conv_ffn_b'], w['w_down'], x, p_emb, w['w_ple_gate'], w['w_ple'],
                    batch, seq, f"ffn_tail_{tag}")


def kernel(x_prompt, x_sample, p_prompt, p_sample, norm_mix, w_in, conv_a, a_log, dt_bias, norm_a, qn_b, kn_b, qn_c, kn_c, lambdas_c, subln_c, w_o_a, w_o_b, w_o_c, w_out, norm_ffn, w_up, conv_ffn, conv_ffn_b, w_down, w_ple, w_ple_gate):
    depth = w_in.shape[0]
    layers = []
    for i in range(depth):
        layers.append(dict(
            norm_mix=norm_mix[i], w_in=_permute_w_in(w_in[i]).astype(BF16), conv_a=conv_a[i],
            alog_row=_gdn_param_row(a_log[i]), dtb_row=_gdn_param_row(dt_bias[i]), norm_a=norm_a[i],
            qn_b=qn_b[i], kn_b=kn_b[i], qn_c2=jnp.tile(qn_c[i], 2), kn_c2=jnp.tile(kn_c[i], 2),
            lambdas_c=lambdas_c[i], subln_c=subln_c[i],
            w_o_a=w_o_a[i].astype(BF16), w_o_b=w_o_b[i].astype(BF16), w_o_c=w_o_c[i].astype(BF16),
            w_out=w_out[i].astype(BF16), norm_ffn=norm_ffn[i], w_up=w_up[i].astype(BF16),
            conv_ffn=conv_ffn[i], conv_ffn_b=conv_ffn_b[i], w_down=w_down[i].astype(BF16),
            w_ple=w_ple[i].astype(BF16), w_ple_gate=w_ple_gate[i].astype(BF16)))

    outs = []
    for tag, x, p in (("p", x_prompt, p_prompt), ("s", x_sample, p_sample)):
        batch, seq, d = x.shape
        tabs_b = _axial_tables(seq)
        tabs_c = _partial_tables(seq)
        h = x.reshape(batch * seq, d)
        for i in range(depth):
            h = _encoder_layer(h, p[i].reshape(batch * seq, -1), layers[i], i, batch, seq, tabs_b, tabs_c, f"{tag}{i}")
        outs.append(h.reshape(batch, seq, d))
    return tuple(outs)
```

```python
import functools
import math

import jax
import jax.numpy as jnp
from jax import lax
from jax.experimental import pallas as pl
from jax.experimental.pallas import tpu as pltpu

F32 = jnp.float32
BF16 = jnp.bfloat16
HIGHEST = lax.Precision.HIGHEST

GRID_W = 64
NORM_EPS = 1e-6
HA = 8
DA = 128
A_WIDTH = HA * DA
DELTA_CHUNK = 64
HB = 8
KVB = 2
DB = 128
AXIAL_THETA = 10000.0
HC = 8
DC = 64
ROT_C = DC // 4
ROPE_THETA = 500000.0

LANES = 128
SUBLANES = 8
VMEM_LIMIT_BYTES = 56 * 1024 * 1024

OFF_Z = 3072
OFF_AB = 4096
AB_WIDTH = 512
OFF_BQ = 4608
OFF_BV = 5888
OFF_CQ = 6144
OFF_CV = 8192
OFF_GATE = 9216
GDN_HEAD_GROUP = 4
GDN_BLOCK = 512
ATTN_KV_CHUNK = 1024
ATTN_Q_ROWS = 2048
FLASH_UNROLL = 2
V_ROWS = LANES + 16
LOG2_E = math.log2(math.e)


def _cparams(semantics):
    return pltpu.CompilerParams(dimension_semantics=semantics, vmem_limit_bytes=VMEM_LIMIT_BYTES)


def _tile(n, pref):
    t = min(n, pref)
    while n % t:
        t //= 2
    return t


def _sigmoid(x):
    return 1.0 / (1.0 + jnp.exp(-x))


def _softplus(x):
    return jnp.maximum(x, 0.0) + jnp.log(1.0 + jnp.exp(-jnp.abs(x)))


def _norm_matmul_kernel(x_ref, g_ref, w_ref, o_ref, h_ref, *, rows):
    @pl.when(pl.program_id(1) == 0)
    def _():
        def chunk(r, c):
            rs = pl.ds(pl.multiple_of(r * rows, rows), rows)
            x = x_ref[rs, :]
            ms = jnp.mean(x * x, axis=-1, keepdims=True)
            h_ref[rs, :] = (x * lax.rsqrt(ms + NORM_EPS) * g_ref[...]).astype(BF16)
            return c
        lax.fori_loop(0, x_ref.shape[0] // rows, chunk, 0)

    o_ref[...] = jnp.dot(h_ref[...], w_ref[...], preferred_element_type=F32).astype(o_ref.dtype)


def norm_matmul(x, g, w, name):
    m, k = x.shape
    n = w.shape[1]
    tm = _tile(m, 1024)
    tn = 1536 if n % 1536 == 0 else _tile(n, 1024)
    rows = _tile(tm, 128)
    return pl.pallas_call(
        functools.partial(_norm_matmul_kernel, rows=rows),
        grid=(m // tm, n // tn),
        in_specs=[
            pl.BlockSpec((tm, k), lambda i, j: (i, 0)),
            pl.BlockSpec((1, k), lambda i, j: (0, 0)),
            pl.BlockSpec((k, tn), lambda i, j: (0, j)),
        ],
        out_specs=pl.BlockSpec((tm, tn), lambda i, j: (i, j)),
        out_shape=jax.ShapeDtypeStruct((m, n), BF16),
        scratch_shapes=[pltpu.VMEM((tm, k), BF16)],
        compiler_params=_cparams(("parallel", "arbitrary")),
        name=name,
    )(x, g.reshape(1, k), w)


def _resident(shape):
    return pl.BlockSpec(shape, lambda *_: (0,) * len(shape), pipeline_mode=pl.Buffered(1))


def _merge_out_kernel(*refs, n_branch, n_chunk, tc):
    o2_ref, z_ref, na_ref = refs[:3]
    refs = refs[3:]
    y_refs = refs[:n_branch - 1]
    g_refs = refs[n_branch - 1:n_branch - 1 + n_branch * n_chunk]
    w_refs = refs[n_branch - 1 + n_branch * n_chunk:2 * n_branch - 1 + n_branch * n_chunk]
    wo_ref, x_ref, gn_ref, o_ref, h_ref, m_ref, ya_ref = refs[2 * n_branch - 1 + n_branch * n_chunk:]
    _gdn_gate_norm(o2_ref, z_ref, na_ref, ya_ref)
    y_refs = (ya_ref,) + tuple(y_refs)
    for c in range(n_chunk):
        cols = slice(c * tc, (c + 1) * tc)
        acc = None
        for i in range(n_branch):
            gate = _sigmoid(g_refs[i * n_chunk + c][...].astype(F32))
            term = gate * jnp.dot(y_refs[i][...], w_refs[i][:, cols], preferred_element_type=F32)
            acc = term if acc is None else acc + term
        m_ref[:, cols] = acc.astype(BF16)
    x1 = x_ref[...] + jnp.dot(m_ref[...], wo_ref[...], preferred_element_type=F32)
    o_ref[...] = x1
    ms = jnp.mean(x1 * x1, axis=-1, keepdims=True)
    h_ref[...] = (x1 * lax.rsqrt(ms + NORM_EPS) * gn_ref[...]).astype(h_ref.dtype)


def merge_out(o2, norm_a, ys, proj, w_os, w_out, x, g_next, name):
    m, k = ys[0].shape
    n = w_out.shape[1]
    tm = _tile(m, 256)
    tc = _tile(n, 1024)
    n_chunk = n // tc
    n_branch = len(ys) + 1
    gate_blk = OFF_GATE // tc
    row = lambda i: (i, 0)
    g_specs = [pl.BlockSpec((tm, tc), lambda i, b=b: (i, gate_blk + b)) for b in range(n_branch * n_chunk)]
    return pl.pallas_call(
        functools.partial(_merge_out_kernel, n_branch=n_branch, n_chunk=n_chunk, tc=tc),
        grid=(m // tm,),
        in_specs=([pl.BlockSpec((2, tm, k), lambda i: (0, i, 0)), pl.BlockSpec((tm, k), lambda i: (i, OFF_Z // k)),
                   _resident((1, DA))]
                  + [pl.BlockSpec((tm, k), row)] * len(ys) + g_specs + [_resident((k, n))] * n_branch
                  + [_resident((n, n)), pl.BlockSpec((tm, n), row), _resident((1, n))]),
        out_specs=[pl.BlockSpec((tm, n), row), pl.BlockSpec((tm, n), row)],
        out_shape=[jax.ShapeDtypeStruct((m, n), F32), jax.ShapeDtypeStruct((m, n), BF16)],
        scratch_shapes=[pltpu.VMEM((tm, n), BF16), pltpu.VMEM((tm, k), BF16)],
        compiler_params=_cparams(("parallel",)),
        name=name,
    )(o2, proj, norm_a.reshape(1, DA), *ys, *([proj] * (n_branch * n_chunk)), *w_os, w_out, x, g_next.reshape(1, n))


def _matmul_kernel(a_ref, w_ref, o_ref):
    o_ref[...] = jnp.dot(a_ref[...], w_ref[...], preferred_element_type=F32).astype(o_ref.dtype)


def matmul_bf16(a, w, name):
    m, k = a.shape
    n = w.shape[1]
    tm = _tile(m, 1024)
    tn = _tile(n, 1024)
    return pl.pallas_call(
        _matmul_kernel,
        grid=(m // tm, n // tn),
        in_specs=[pl.BlockSpec((tm, k), lambda i, j: (i, 0)), pl.BlockSpec((k, tn), lambda i, j: (0, j))],
        out_specs=pl.BlockSpec((tm, tn), lambda i, j: (i, j)),
        out_shape=jax.ShapeDtypeStruct((m, n), BF16),
        compiler_params=_cparams(("parallel", "arbitrary")),
        name=name,
    )(a, w)


def _shift_mats(ts):
    r = lax.broadcasted_iota(jnp.int32, (ts, ts), 0)
    c = lax.broadcasted_iota(jnp.int32, (ts, ts), 1)
    return (c == r - 1).astype(BF16), (c == r + 1).astype(BF16)


def _conv3(xb, prev_row, next_row, w_ref, lanes, shifts):
    ts = xb.shape[0]
    w0, w1, w2 = w_ref[0:1, lanes], w_ref[1:2, lanes], w_ref[2:3, lanes]
    xp = jnp.dot(shifts[0], xb, preferred_element_type=F32)
    xn = jnp.dot(shifts[1], xb, preferred_element_type=F32)
    y = xp * w0 + xb.astype(F32) * w1 + xn * w2
    r8 = lax.broadcasted_iota(jnp.int32, (SUBLANES, 1), 0)
    first = y[0:SUBLANES] + jnp.where(r8 == 0, prev_row, 0.0) * w0
    last = y[ts - SUBLANES:] + jnp.where(r8 == SUBLANES - 1, next_row, 0.0) * w2
    return jnp.concatenate([first, y[SUBLANES:ts - SUBLANES], last], axis=0)


def _halo_specs(batch, seq, ts, width, col_fn):
    n_s = seq // ts
    per_seq = seq // SUBLANES
    per_tile = ts // SUBLANES
    last = batch * per_seq - 1

    def prev_map(b, i, *rest):
        return (jnp.maximum(b * per_seq + i * per_tile - 1, 0), col_fn(*rest))

    def next_map(b, i, *rest):
        return (jnp.minimum(b * per_seq + (i + 1) * per_tile, last), col_fn(*rest))

    return pl.BlockSpec((SUBLANES, width), prev_map), pl.BlockSpec((SUBLANES, width), next_map), n_s


def _conv3_roll(x, prev_row, next_row, w_ref, lanes):
    ts = x.shape[0]
    r = lax.broadcasted_iota(jnp.int32, (ts, 1), 0)
    xp = jnp.where(r == 0, prev_row, pltpu.roll(x, 1, 0))
    xn = jnp.where(r == ts - 1, next_row, pltpu.roll(x, ts - 1, 0))
    return xp * w_ref[0:1, lanes] + x * w_ref[1:2, lanes] + xn * w_ref[2:3, lanes]


def _ffn_tail_kernel(u_ref, up_ref, un_ref, cw_ref, cb_ref, wd_ref, x_ref, p_ref, wg_ref, we_ref, o_ref, *, n_s, f, tc):
    i = pl.program_id(1)
    has_prev = (i > 0).astype(F32)
    has_next = (i < n_s - 1).astype(F32)

    def conv(lanes):
        prev_row = up_ref[SUBLANES - 1:SUBLANES, lanes].astype(F32) * has_prev
        next_row = un_ref[0:1, lanes].astype(F32) * has_next
        return _conv3_roll(u_ref[:, lanes].astype(F32), prev_row, next_row, cw_ref, lanes) + cb_ref[:, lanes]

    x2 = x_ref[...]
    for c in range(f // tc):
        gate = conv(slice(c * tc, (c + 1) * tc))
        val = conv(slice(f + c * tc, f + (c + 1) * tc))
        act = (gate * _sigmoid(gate) * val).astype(BF16)
        x2 = x2 + jnp.dot(act, wd_ref[c * tc:(c + 1) * tc, :], preferred_element_type=F32)
    gate = _sigmoid(jnp.dot(x2.astype(BF16), wg_ref[...], preferred_element_type=F32))
    emb = jnp.dot(p_ref[...].astype(BF16), we_ref[...], preferred_element_type=F32)
    o_ref[...] = x2 + gate * emb


def ffn_tail(u, conv_w, conv_b, w_down, x, p, wg, we, batch, seq, name):
    t, two_f = u.shape
    f = two_f // 2
    n = w_down.shape[1]
    kp = p.shape[1]
    tm = _tile(seq, 256)
    prev_s, next_s, n_s = _halo_specs(batch, seq, tm, two_f, lambda: 0)
    row = lambda b, i: (b * n_s + i, 0)
    return pl.pallas_call(
        functools.partial(_ffn_tail_kernel, n_s=n_s, f=f, tc=512),
        grid=(batch, n_s),
        in_specs=[pl.BlockSpec((tm, two_f), row), prev_s, next_s, _resident((3, two_f)), _resident((1, two_f)),
                  _resident((f, n)), pl.BlockSpec((tm, n), row), pl.BlockSpec((tm, kp), row),
                  _resident((n, n)), _resident((kp, n))],
        out_specs=pl.BlockSpec((tm, n), row),
        out_shape=jax.ShapeDtypeStruct((t, n), F32),
        compiler_params=_cparams(("parallel", "parallel")),
        name=name,
    )(u, u, u, conv_w, conv_b.reshape(1, two_f), w_down, x, p, wg, we)


def _gdn_prep_kernel(x_ref, xp_ref, xn_ref, ab_ref, w_ref, alog_ref, dtb_ref, q_ref, k_ref, v_ref, gb_ref, *, n_s):
    i = pl.program_id(1)
    has_prev = (i > 0).astype(F32)
    has_next = (i < n_s - 1).astype(F32)
    shifts = _shift_mats(x_ref.shape[0])
    for c in range(3 * HA):
        lanes = slice(c * DA, (c + 1) * DA)
        prev_row = xp_ref[SUBLANES - 1:SUBLANES, lanes].astype(F32) * has_prev
        next_row = xn_ref[0:1, lanes].astype(F32) * has_next
        y = _conv3(x_ref[:, lanes], prev_row, next_row, w_ref, lanes, shifts)
        y = y * _sigmoid(y)
        out_lanes = slice((c % HA) * DA, (c % HA + 1) * DA)
        if c < 2 * HA:
            y = y * lax.rsqrt(jnp.sum(y * y, axis=-1, keepdims=True) + NORM_EPS)
            (q_ref if c < HA else k_ref)[:, out_lanes] = y.astype(BF16)
        else:
            v_ref[:, out_lanes] = y.astype(BF16)
    ab = ab_ref[...].astype(F32)
    lane = lax.broadcasted_iota(jnp.int32, (1, AB_WIDTH), 1) % LANES
    g = -jnp.exp(alog_ref[...]) * _softplus(ab + dtb_ref[...])
    gb_ref[...] = jnp.where(lane < GDN_HEAD_GROUP, g, _sigmoid(ab))


def gdn_prep(proj, conv_w, alog_row, dtb_row, batch, seq, name):
    t = proj.shape[0]
    ts = _tile(seq, 256)
    w3 = 3 * A_WIDTH
    prev_s, next_s, n_s = _halo_specs(batch, seq, ts, w3, lambda: 0)
    row = lambda b, i: (b * n_s + i, 0)
    out_spec = pl.BlockSpec((ts, A_WIDTH), row)
    return pl.pallas_call(
        functools.partial(_gdn_prep_kernel, n_s=n_s),
        grid=(batch, n_s),
        in_specs=[
            pl.BlockSpec((ts, w3), row), prev_s, next_s,
            pl.BlockSpec((ts, AB_WIDTH), lambda b, i: (b * n_s + i, OFF_AB // AB_WIDTH)),
            pl.BlockSpec((3, w3), lambda b, i: (0, 0)),
            pl.BlockSpec((1, AB_WIDTH), lambda b, i: (0, 0)),
            pl.BlockSpec((1, AB_WIDTH), lambda b, i: (0, 0)),
        ],
        out_specs=[out_spec, out_spec, out_spec, pl.BlockSpec((ts, AB_WIDTH), row)],
        out_shape=[jax.ShapeDtypeStruct((t, A_WIDTH), BF16)] * 3 + [jax.ShapeDtypeStruct((t, AB_WIDTH), F32)],
        compiler_params=_cparams(("parallel", "parallel")),
        name=name,
    )(proj, proj, proj, proj, conv_w, alog_row, dtb_row)


def _gdn_scan_kernel(q_ref, k_ref, v_ref, gb_ref, o_ref, s_ref, *, n_chunks):
    c_len = DELTA_CHUNK
    d = pl.program_id(1)

    @pl.when(pl.program_id(3) == 0)
    def _():
        s_ref[...] = jnp.zeros_like(s_ref)

    sgn = 1 - 2 * d
    row = lax.broadcasted_iota(jnp.int32, (c_len, c_len), 0)
    col = lax.broadcasted_iota(jnp.int32, (c_len, c_len), 1)
    order = (row - col) * sgn
    incl = order >= 0
    strict = order > 0
    cum_mat = incl.astype(F32)
    eye = (row == col).astype(F32)
    sel = (lax.broadcasted_iota(jnp.int32, (SUBLANES, LANES), 0)
           == lax.broadcasted_iota(jnp.int32, (SUBLANES, LANES), 1)).astype(F32)
    scale = DA ** -0.5
    nt = (((1,), (1,)), ((), ()))
    tn = (((0,), (0,)), ((), ()))

    heads = range(GDN_HEAD_GROUP)
    chunk_rows, gcols, grows, exp_gs, exp_rests, exp_tots, gblks = [], [], [], [], [], [], []
    for j in range(n_chunks):
        c = j + d * (n_chunks - 1 - 2 * j)
        rows = pl.ds(pl.multiple_of(c * c_len, c_len), c_len)
        gblk = gb_ref[rows, :]
        gcol = jnp.dot(cum_mat, gblk, precision=HIGHEST, preferred_element_type=F32)
        chunk_rows.append(rows)
        gblks.append(gblk)
        gcols.append(gcol)
    for j in range(n_chunks):
        grows.append(lax.dot_general(sel, gcols[j], nt, precision=HIGHEST, preferred_element_type=F32))
        gtot = jnp.sum(gblks[j], axis=0, keepdims=True)
        exp_gs.append(jnp.exp(gcols[j]))
        exp_rests.append(jnp.exp(gtot - gcols[j]))
        exp_tots.append(jnp.exp(gtot))

    units = [(j, h) for j in range(n_chunks) for h in heads]
    low, intra, rhs = {}, {}, {}
    for (j, h) in units:
        rows = chunk_rows[j]
        lanes = slice(h * DA, (h + 1) * DA)
        kbf = k_ref[rows, lanes]
        kf = kbf.astype(F32)
        beta = gblks[j][:, GDN_HEAD_GROUP + h:GDN_HEAD_GROUP + h + 1]
        diff = gcols[j][:, h:h + 1] - grows[j][h:h + 1, :]
        decay = jnp.where(incl, jnp.exp(jnp.where(incl, diff, 0.0)), 0.0)
        kb = kf * beta
        kk = lax.dot_general(kb.astype(BF16), kbf, nt, preferred_element_type=F32)
        qs = (q_ref[rows, lanes].astype(F32) * scale).astype(BF16)
        qk = lax.dot_general(qs, kbf, nt, preferred_element_type=F32)
        low[j, h] = jnp.where(strict, kk * decay, 0.0)
        intra[j, h] = jnp.where(incl, qk * decay, 0.0).astype(BF16)
        rhs[j, h] = jnp.concatenate([v_ref[rows, lanes].astype(F32) * beta, kb * exp_gs[j][:, h:h + 1]],
                                    axis=1).astype(BF16)

    def square(p):
        return {u: jnp.dot(p[u].astype(BF16), p[u].astype(BF16), preferred_element_type=F32) for u in units}

    def pair(first, p_second):
        return {u: first[u] + jnp.dot(first[u].astype(BF16), p_second[u].astype(BF16), preferred_element_type=F32)
                for u in units}

    p2 = square(low)
    p4 = square(p2)
    fac_a = pair({u: eye - low[u] for u in units}, p2)
    p8 = square(p4)
    p16 = square(p8)
    fac_b = pair({u: eye + p4[u] for u in units}, p8)
    p32 = square(p16)
    fac_ab = {u: jnp.dot(fac_a[u].astype(BF16), fac_b[u].astype(BF16), preferred_element_type=F32) for u in units}
    fac_c = pair({u: eye + p16[u] for u in units}, p32)
    inv = {u: jnp.dot(fac_ab[u].astype(BF16), fac_c[u].astype(BF16), preferred_element_type=F32) for u in units}
    uw = {u: jnp.dot(inv[u].astype(BF16), rhs[u], preferred_element_type=F32).astype(BF16) for u in units}

    chain_lhs, out_local, state_add = {}, {}, {}
    for (j, h) in units:
        rows = chunk_rows[j]
        lanes = slice(h * DA, (h + 1) * DA)
        k_dec = (k_ref[rows, lanes].astype(F32) * exp_rests[j][:, h:h + 1]).astype(BF16)
        k_uw = lax.dot_general(k_dec, uw[j, h], tn, preferred_element_type=F32)
        i_uw = jnp.dot(intra[j, h], uw[j, h], preferred_element_type=F32)
        q_dec = q_ref[rows, lanes].astype(F32) * scale * exp_gs[j][:, h:h + 1]
        chain_lhs[j, h] = jnp.concatenate([q_dec - i_uw[:, DA:], k_uw[:, DA:]], axis=0).astype(BF16)
        out_local[j, h] = i_uw[:, :DA]
        state_add[j, h] = k_uw[:, :DA]

    for j in range(n_chunks):
        rows = chunk_rows[j]
        for h in heads:
            lanes = slice(h * DA, (h + 1) * DA)
            state = s_ref[h]
            prod = jnp.dot(chain_lhs[j, h], state.astype(BF16), preferred_element_type=F32)
            s_ref[h] = state * exp_tots[j][:, h:h + 1] - prod[c_len:] + state_add[j, h]
            o_ref[0, rows, lanes] = (prod[:c_len] + out_local[j, h]).astype(o_ref.dtype)


def gdn_scan(q, k, v, gb, batch, seq, name):
    t = q.shape[0]
    blk = _tile(seq, GDN_BLOCK)
    n_blk = seq // blk
    width = GDN_HEAD_GROUP * DA
    n_hg = HA // GDN_HEAD_GROUP

    def rblk(b, d, i):
        return b * n_blk + i + d * (n_blk - 1 - 2 * i)

    qkv_spec = pl.BlockSpec((blk, width), lambda b, d, g, i: (rblk(b, d, i), g))
    return pl.pallas_call(
        functools.partial(_gdn_scan_kernel, n_chunks=blk // DELTA_CHUNK),
        grid=(batch, 2, n_hg, n_blk),
        in_specs=[qkv_spec, qkv_spec, qkv_spec,
                  pl.BlockSpec((blk, LANES), lambda b, d, g, i: (rblk(b, d, i), d * n_hg + g))],
        out_specs=pl.BlockSpec((1, blk, width), lambda b, d, g, i: (d, rblk(b, d, i), g)),
        out_shape=jax.ShapeDtypeStruct((2, t, A_WIDTH), BF16),
        scratch_shapes=[pltpu.VMEM((GDN_HEAD_GROUP, DA, DA), F32)],
        compiler_params=_cparams(("parallel", "parallel", "parallel", "arbitrary")),
        name=name,
    )(q, k, v, gb)


def _gdn_gate_norm(o_ref, z_ref, g_ref, y_ref):
    for h in range(HA):
        lanes = slice(h * DA, (h + 1) * DA)
        o = o_ref[0, :, lanes].astype(F32) + o_ref[1, :, lanes].astype(F32)
        z = z_ref[:, lanes].astype(F32)
        ms = jnp.mean(o * o, axis=-1, keepdims=True)
        y = o * lax.rsqrt(ms + NORM_EPS) * g_ref[...]
        y_ref[:, lanes] = (y * (z * _sigmoid(z))).astype(y_ref.dtype)


def _qkv_prep_kernel(x_ref, cos_ref, sa_ref, sb_ref, gq_ref, gk_ref, qk_ref, vt_ref, *, n_q, n_k, n_v, seg, shift,
                     q_scale, split):
    grp_r = lax.broadcasted_iota(jnp.int32, (LANES, LANES), 0) // seg
    grp_c = lax.broadcasted_iota(jnp.int32, (LANES, LANES), 1) // seg
    seg_ones = (grp_r == grp_c).astype(BF16)
    lane = lax.broadcasted_iota(jnp.int32, (1, LANES), 1)
    cos, sin_a, sin_b = cos_ref[...], sa_ref[...], sb_ref[...]
    per = 2 if split else 1
    for h in range(n_q + n_k):
        is_q = h < n_q
        x = x_ref[:, h * LANES:(h + 1) * LANES].astype(F32)
        sq = x * x
        hi = sq.astype(BF16)
        lo = (sq - hi.astype(F32)).astype(BF16)
        ssum = (jnp.dot(hi, seg_ones, preferred_element_type=F32) + jnp.dot(lo, seg_ones, preferred_element_type=F32))
        gain = (gq_ref if is_q else gk_ref)[...]
        y = x * lax.rsqrt(ssum * (1.0 / seg) + NORM_EPS) * gain
        y = y * cos + pltpu.roll(y, LANES - shift, 1) * sin_a + pltpu.roll(y, shift, 1) * sin_b
        if is_q:
            y = y * q_scale
        if is_q and split:
            qk_ref[0, per * h] = jnp.where(lane < seg, y, 0.0).astype(qk_ref.dtype)
            qk_ref[0, per * h + 1] = jnp.where(lane >= seg, y, 0.0).astype(qk_ref.dtype)
        else:
            slot = per * h if is_q else per * n_q + (h - n_q)
            qk_ref[0, slot] = y.astype(qk_ref.dtype)
    for h in range(n_v):
        c0 = (n_q + n_k + h) * LANES
        vt_ref[0, h, 0, 0:LANES, :] = x_ref[:, c0:c0 + LANES].astype(F32).T.astype(vt_ref.dtype)
        vt_ref[0, h, 0, LANES:V_ROWS, :] = jnp.ones((V_ROWS - LANES, vt_ref.shape[-1]), vt_ref.dtype)


def qkv_prep(proj, tables, gq, gk, batch, seq, *, col_off, n_q, n_k, n_v, seg, shift, q_scale, split, tk, name):
    ts = tk
    n_s = seq // ts
    width = (n_q + n_k + n_v) * LANES
    slots = (2 if split else 1) * n_q + n_k
    tab = pl.BlockSpec((ts, LANES), lambda b, i: (i, 0))
    gain = pl.BlockSpec((1, LANES), lambda b, i: (0, 0))
    return pl.pallas_call(
        functools.partial(_qkv_prep_kernel, n_q=n_q, n_k=n_k, n_v=n_v, seg=seg, shift=shift, q_scale=q_scale,
                          split=split),
        grid=(batch, n_s),
        in_specs=[pl.BlockSpec((ts, width), lambda b, i: (b * n_s + i, col_off // width)), tab, tab, tab, gain, gain],
        out_specs=[pl.BlockSpec((1, slots, ts, LANES), lambda b, i: (b, 0, i, 0)),
                   pl.BlockSpec((1, n_v, 1, V_ROWS, tk), lambda b, i: (b, 0, i, 0, 0))],
        out_shape=[jax.ShapeDtypeStruct((batch, slots, seq, LANES), BF16),
                   jax.ShapeDtypeStruct((batch, n_v, n_s, V_ROWS, tk), BF16)],
        compiler_params=_cparams(("parallel", "parallel")),
        name=name,
    )(proj, tables[0], tables[1], tables[2], gq.reshape(1, LANES), gk.reshape(1, LANES))


def _flash_kernel(*refs, groups, tq, tk, seq, diff, lambda_init):
    if diff:
        (q_ref, qn_ref, k_ref, vt_ref, lam_ref, sub_ref, o_ref,
         qt_ref, sa_ref, sb_ref, ma_ref, mb_ref, m_ref, acc_ref) = refs
    else:
        q_ref, qn_ref, k_ref, vt_ref, o_ref, qt_ref, sa_ref, sb_ref, ma_ref, mb_ref, m_ref, acc_ref = refs
    rows = groups * tq
    n_c = seq // tk
    carry_over = n_c % 2 == 0

    def load_q(ref):
        qt_ref[...] = ref[0].reshape(rows, LANES).astype(F32).T.astype(BF16)

    m_ref[...] = jnp.full(m_ref.shape, -jnp.inf, F32)
    acc_ref[...] = jnp.zeros(acc_ref.shape, F32)
    buf_a = (sa_ref, ma_ref)
    buf_b = (sb_ref, mb_ref)

    def scores_into(c, buf):
        ks = pl.ds(pl.multiple_of(c * tk, tk), tk)
        s_t = jnp.dot(k_ref[0, 0, ks, :], qt_ref[...], preferred_element_type=F32)
        buf[0][...] = s_t
        buf[1][...] = jnp.max(s_t, axis=0, keepdims=True)

    def step(c, cur, nxt, wrap=False):
        if wrap:
            load_q(qn_ref)
            scores_into(0, nxt)
        elif nxt is not None:
            scores_into(c + 1, nxt)
        m_prev = m_ref[...]
        m_new = jnp.maximum(m_prev, cur[1][...])
        p_t = jnp.exp2(cur[0][...] - m_new)
        alpha = jnp.exp2(m_prev - m_new)
        acc_ref[...] = alpha * acc_ref[...] + jnp.dot(vt_ref[0, 0, c], p_t.astype(BF16),
                                                      preferred_element_type=F32)
        m_ref[...] = m_new

    def cold_start():
        load_q(q_ref)
        scores_into(0, buf_a)

    if carry_over:
        pl.when(pl.program_id(2) == 0)(cold_start)
    else:
        cold_start()
    bufs = (buf_a, buf_b)
    n_loop = (n_c - 1) // FLASH_UNROLL

    def body(i, carry):
        for u in range(FLASH_UNROLL):
            step(FLASH_UNROLL * i + u, bufs[u % 2], bufs[(u + 1) % 2])
        return carry

    lax.fori_loop(0, n_loop, body, 0)
    for c in range(FLASH_UNROLL * n_loop, n_c):
        last = c + 1 == n_c
        step(c, bufs[c % 2], bufs[(c + 1) % 2] if (not last or carry_over) else None, wrap=last and carry_over)
    o_t = acc_ref[0:LANES, :] / acc_ref[LANES:LANES + 1, :]
    if not diff:
        o = o_t.T
        for g in range(groups):
            o_ref[:, g * LANES:(g + 1) * LANES] = o[g * tq:(g + 1) * tq].astype(o_ref.dtype)
    else:
        lf = lam_ref[...]
        lam = (jnp.exp(jnp.sum(lf[0:1] * lf[1:2], axis=-1, keepdims=True))
               - jnp.exp(jnp.sum(lf[2:3] * lf[3:4], axis=-1, keepdims=True)) + lambda_init)
        d_t = o_t[:, :tq] - lam * o_t[:, tq:]
        ms = jnp.mean(d_t * d_t, axis=0, keepdims=True)
        y = (d_t * lax.rsqrt(ms + NORM_EPS)).T * sub_ref[...] * (1.0 - lambda_init)
        o_ref[...] = y.astype(o_ref.dtype)


def flash_attention(qk, vt, batch, seq, *, kv_heads, groups, k_slot0, k_stride, diff=False,
                    lambdas=None, subln=None, lambda_init=0.0, name):
    tk = vt.shape[-1]
    n_c = seq // tk
    tq = _tile(seq, ATTN_Q_ROWS // groups)
    n_q = seq // tq
    rows = groups * tq
    out_w = (1 if diff else groups) * LANES
    in_specs = [
        pl.BlockSpec((1, groups, tq, LANES), lambda b, h, i: (b, h, i, 0)),
        pl.BlockSpec((1, groups, tq, LANES), lambda b, h, i: (b, h, jnp.minimum(i + 1, n_q - 1), 0)),
        pl.BlockSpec((1, 1, seq, LANES), lambda b, h, i: (b, k_slot0 + k_stride * h, 0, 0)),
        pl.BlockSpec((1, 1, n_c, V_ROWS, tk), lambda b, h, i: (b, h, 0, 0, 0)),
    ]
    args = [qk, qk, qk, vt]
    if diff:
        in_specs += [pl.BlockSpec(lambdas.shape, lambda b, h, i: (0, 0)),
                     pl.BlockSpec((1, LANES), lambda b, h, i: (0, 0))]
        args += [lambdas, subln.reshape(1, LANES)]
    return pl.pallas_call(
        functools.partial(_flash_kernel, groups=groups, tq=tq, tk=tk, seq=seq, diff=diff, lambda_init=lambda_init),
        grid=(batch, kv_heads, n_q),
        in_specs=in_specs,
        out_specs=pl.BlockSpec((tq, out_w), lambda b, h, i: (b * n_q + i, h)),
        out_shape=jax.ShapeDtypeStruct((batch * seq, kv_heads * out_w), BF16),
        scratch_shapes=[pltpu.VMEM((LANES, rows), BF16), pltpu.VMEM((tk, rows), F32), pltpu.VMEM((tk, rows), F32),
                        pltpu.VMEM((1, rows), F32), pltpu.VMEM((1, rows), F32),
                        pltpu.VMEM((1, rows), F32), pltpu.VMEM((V_ROWS, rows), F32)],
        compiler_params=_cparams(("parallel", "parallel", "arbitrary")),
        name=name,
    )(*args)


def _rope_angles(pos, dim, theta):
    inv = theta ** (-jnp.arange(0, dim, 2, dtype=F32) / dim)
    return pos[:, None] * inv[None, :]


def _axial_tables(seq):
    t = jnp.arange(seq)
    row = (t // GRID_W).astype(F32)
    col = (t % GRID_W).astype(F32)
    half = DB // 2
    ang_r = _rope_angles(row, half, AXIAL_THETA)
    ang_c = _rope_angles(col, half, AXIAL_THETA)
    ang = jnp.concatenate([ang_r, ang_r, ang_c, ang_c], axis=-1)
    cos, sin = jnp.cos(ang), jnp.sin(ang)
    first = (jnp.arange(LANES) % half) < (half // 2)
    return cos, jnp.where(first, -sin, 0.0), jnp.where(first, 0.0, sin)


def _partial_tables(seq):
    pos = jnp.arange(seq, dtype=F32)
    ang8 = _rope_angles(pos, ROT_C, ROPE_THETA)
    lane = jnp.arange(LANES) % DC
    ang = jnp.take(ang8, lane % (ROT_C // 2), axis=1)
    rot = lane < ROT_C
    first = lane < ROT_C // 2
    cos = jnp.where(rot, jnp.cos(ang), 1.0)
    sin = jnp.where(rot, jnp.sin(ang), 0.0)
    return cos, jnp.where(first, -sin, 0.0), jnp.where(first, 0.0, sin)


def _permute_w_in(w):
    k = w.shape[0]
    alpha0 = 4 * A_WIDTH
    beta0 = alpha0 + 2 * HA
    blocks = []
    for d in range(2):
        for g in range(HA // GDN_HEAD_GROUP):
            a0 = alpha0 + d * HA + g * GDN_HEAD_GROUP
            b0 = beta0 + d * HA + g * GDN_HEAD_GROUP
            blocks += [w[:, a0:a0 + GDN_HEAD_GROUP], w[:, b0:b0 + GDN_HEAD_GROUP],
                       jnp.zeros((k, LANES - 2 * GDN_HEAD_GROUP), w.dtype)]
    return jnp.concatenate([w[:, :alpha0]] + blocks + [w[:, beta0 + 2 * HA:]], axis=1)


def _gdn_param_row(p):
    blocks = []
    for d in range(2):
        for g in range(HA // GDN_HEAD_GROUP):
            blocks += [p[d, g * GDN_HEAD_GROUP:(g + 1) * GDN_HEAD_GROUP], jnp.zeros((LANES - GDN_HEAD_GROUP,), p.dtype)]
    return jnp.concatenate(blocks).reshape(1, AB_WIDTH)


def _encoder_layer(x, p_emb, w, li, batch, seq, tabs_b, tabs_c, tag):
    lambda_init = 0.8 - 0.6 * math.exp(-0.3 * li)
    proj = norm_matmul(x, w['norm_mix'], w['w_in'], f"in_proj_{tag}")

    q_a, k_a, v_a, gb = gdn_prep(proj, w['conv_a'], w['alog_row'], w['dtb_row'], batch, seq, f"gdn_prep_{tag}")
    o2 = gdn_scan(q_a, k_a, v_a, gb, batch, seq, f"gdn_scan_{tag}")

    tk = _tile(seq, ATTN_KV_CHUNK)
    qk_b, vt_b = qkv_prep(proj, tabs_b, w['qn_b'], w['kn_b'], batch, seq, col_off=OFF_BQ, n_q=HB, n_k=KVB, n_v=KVB,
                          seg=DB, shift=DB // 4, q_scale=DB ** -0.5 * LOG2_E, split=False, tk=tk,
                          name=f"prep_b_{tag}")
    y_b = flash_attention(qk_b, vt_b, batch, seq, kv_heads=KVB, groups=HB // KVB, k_slot0=HB, k_stride=1,
                          name=f"attn_b_{tag}")

    qk_c, vt_c = qkv_prep(proj, tabs_c, w['qn_c2'], w['kn_c2'], batch, seq, col_off=OFF_CQ, n_q=HC, n_k=HC, n_v=HC,
                          seg=DC, shift=ROT_C // 2, q_scale=DC ** -0.5 * LOG2_E, split=True, tk=tk,
                          name=f"prep_c_{tag}")
    y_c = flash_attention(qk_c, vt_c, batch, seq, kv_heads=HC, groups=2, k_slot0=2 * HC, k_stride=1,
                          diff=True, lambdas=w['lambdas_c'], subln=w['subln_c'],
                          lambda_init=lambda_init, name=f"attn_c_{tag}")

    x, h_ffn = merge_out(o2, w['norm_a'], (y_b, y_c), proj, (w['w_o_a'], w['w_o_b'], w['w_o_c']), w['w_out'], x,
                         w['norm_ffn'], f"merge_out_{tag}")

    u = matmul_bf16(h_ffn, w['w_up'], f"ffn_up_{tag}")
    return ffn_tail(u, w['conv_ffn'], w['conv_ffn_b'], w['w_down'], x, p_emb, w['w_ple_gate'], w['w_ple'],
                    batch, seq, f"ffn_tail_{tag}")


def kernel(x_prompt, x_sample, p_prompt, p_sample, norm_mix, w_in, conv_a, a_log, dt_bias, norm_a, qn_b, kn_b, qn_c, kn_c, lambdas_c, subln_c, w_o_a, w_o_b, w_o_c, w_out, norm_ffn, w_up, conv_ffn, conv_ffn_b, w_down, w_ple, w_ple_gate):
    depth = w_in.shape[0]
    layers = []
    for i in range(depth):
        layers.append(dict(
            norm_mix=norm_mix[i], w_in=_permute_w_in(w_in[i]).astype(BF16), conv_a=conv_a[i],
            alog_row=_gdn_param_row(a_log[i]), dtb_row=_gdn_param_row(dt_bias[i]), norm_a=norm_a[i],
            qn_b=qn_b[i], kn_b=kn_b[i], qn_c2=jnp.tile(qn_c[i], 2), kn_c2=jnp.tile(kn_c[i], 2),
            lambdas_c=lambdas_c[i], subln_c=subln_c[i],
            w_o_a=w_o_a[i].astype(BF16), w_o_b=w_o_b[i].astype(BF16), w_o_c=w_o_c[i].astype(BF16),
            w_out=w_out[i].astype(BF16), norm_ffn=norm_ffn[i], w_up=w_up[i].astype(BF16),
            conv_ffn=conv_ffn[i], conv_ffn_b=conv_ffn_b[i], w_down=w_down[i].astype(BF16),
            w_ple=w_ple[i].astype(BF16), w_ple_gate=w_ple_gate[i].astype(BF16)))

    outs = []
    for tag, x, p in (("p", x_prompt, p_prompt), ("s", x_sample, p_sample)):
        batch, seq, d = x.shape
        tabs_b = _axial_tables(seq)
        tabs_c = _partial_tables(seq)
        h = x.reshape(batch * seq, d)
        for i in range(depth):
            h = _encoder_layer(h, p[i].reshape(batch * seq, -1), layers[i], i, batch, seq, tabs_b, tabs_c, f"{tag}{i}")
        outs.append(h.reshape(batch, seq, d))
    return tuple(outs)
```
